```python
import jax, jax.numpy as jnp
from jax import lax
import numpy as np

D_MODEL = 1024
BATCH = 16
SEQ = 256
DEPTH = 2
DEC_BATCH = 8
DEC_SEQ = 4096
PAST_LEN = 512

GRID_W = 64
N_HEADS = 8
N_KV_HEADS = 2
HEAD_DIM = 64
Q_W = N_HEADS * HEAD_DIM
KV_W = N_KV_HEADS * HEAD_DIM
AXIS_DIM = HEAD_DIM // 2
ROPE_THETA = 10000.0
Q_BLOCK = 128
CONV_W = 512
CONV_K = 31
POOL_W = 512
POOL_GROUPS = 4
POOL_GC = POOL_W // POOL_GROUPS
POOL_WINDOWS = (2, 4, 8, 16)
SGU_W = 512
SGU_GROUPS = 4
SGU_GC = SGU_W // SGU_GROUPS
SGU_CHUNK = 128
BRANCH_W = 512
N_BRANCH = 4
D_FF = 4 * D_MODEL
IN_W = Q_W + 2 * KV_W + 2 * CONV_W + POOL_W + 2 * SGU_W
EPS = 1e-6

kernel_name = "hybrid_diffusion_prefix_trunk_step"


def rmsnorm(x, g):
    xf = x.astype(jnp.float32)
    y = xf * lax.rsqrt(jnp.mean(xf * xf, axis=-1, keepdims=True) + EPS)
    return (y * g.astype(jnp.float32)).astype(x.dtype)


def layernorm(x, g, b):
    xf = x.astype(jnp.float32)
    mu = jnp.mean(xf, axis=-1, keepdims=True)
    var = jnp.mean(jnp.square(xf - mu), axis=-1, keepdims=True)
    y = (xf - mu) * lax.rsqrt(var + EPS)
    return (y * g.astype(jnp.float32) + b.astype(jnp.float32)).astype(x.dtype)


def axial_rope(n):
    rows = n // GRID_W
    row = jnp.repeat(jnp.arange(rows, dtype=jnp.float32), GRID_W)
    col = jnp.tile(jnp.arange(GRID_W, dtype=jnp.float32), rows)
    inv = ROPE_THETA ** (-jnp.arange(0, AXIS_DIM, 2, dtype=jnp.float32) / AXIS_DIM)
    ang = jnp.concatenate([row[:, None] * inv, col[:, None] * inv], axis=-1)
    return jnp.cos(ang), jnp.sin(ang)


def apply_rope(x, cos, sin):
    xf = x.astype(jnp.float32).reshape(x.shape[:-1] + (HEAD_DIM // 2, 2))
    x0, x1 = xf[..., 0], xf[..., 1]
    c = cos[None, :, None, :]
    s = sin[None, :, None, :]
    out = jnp.stack([x0 * c - x1 * s, x0 * s + x1 * c], axis=-1).reshape(x.shape)
    return out.astype(x.dtype)


def block_attention(q, k, v):
    b, n = q.shape[0], q.shape[1]
    nb = n // Q_BLOCK
    g = N_HEADS // N_KV_HEADS
    qb = q.reshape(b, nb, Q_BLOCK, N_KV_HEADS, g, HEAD_DIM).transpose(1, 0, 2, 3, 4, 5)
    scale = HEAD_DIM ** -0.5

    def one_block(qi):
        s = jnp.einsum('bqkgd,btkd->bkgqt', qi, k).astype(jnp.float32) * scale
        p = jax.nn.softmax(s, axis=-1).astype(v.dtype)
        return jnp.einsum('bkgqt,btkd->bqkgd', p, v)

    o = lax.map(one_block, qb)
    return o.transpose(1, 0, 2, 3, 4, 5).reshape(b, n, Q_W)


def conformer_conv(a, conv_w, conv_b, ln_g, ln_b):
    a = a[..., :CONV_W] * jax.nn.sigmoid(a[..., CONV_W:])
    y = lax.conv_general_dilated(a, conv_w[:, None, :].astype(a.dtype), (1,),
                                 [(CONV_K // 2, CONV_K // 2)],
                                 dimension_numbers=('NWC', 'WIO', 'NWC'),
                                 feature_group_count=CONV_W) + conv_b
    return jax.nn.silu(layernorm(y, ln_g, ln_b))


def multiscale_pool(x, pool_w, pool_scale):
    b, n = x.shape[0], x.shape[1]
    xg = x.reshape(b, n, POOL_GROUPS, POOL_GC)
    cs = jnp.cumsum(xg.astype(jnp.float32), axis=1)
    cs = jnp.concatenate([jnp.zeros_like(cs[:, :1]), cs], axis=1)
    t = jnp.arange(n)[:, None]
    w = jnp.array(POOL_WINDOWS, dtype=jnp.int32)[None, :]
    lo = jnp.clip(t - w // 2, 0, n)
    hi = jnp.clip(t - w // 2 + w, 0, n)
    gidx = jnp.arange(POOL_GROUPS)[None, :]
    s = cs[:, hi, gidx, :] - cs[:, lo, gidx, :]
    mean = s / (hi - lo).astype(jnp.float32)[None, :, :, None]
    pooled = (mean - xg.astype(jnp.float32)).astype(x.dtype)
    y = jnp.einsum('bsgc,gcd->bsgd', pooled, pool_w)
    return y.reshape(b, n, POOL_W) * pool_scale


def spatial_gating(z, norm_g, sgu_w, sgu_b):
    b, n = z.shape[0], z.shape[1]
    z = jax.nn.gelu(z)
    u, v = z[..., :SGU_W], z[..., SGU_W:]
    v = rmsnorm(v, norm_g).reshape(b, n // SGU_CHUNK, SGU_CHUNK, SGU_GROUPS, SGU_GC)
    sv = jnp.einsum('gij,bnjgc->bnigc', sgu_w, v) + sgu_b.T[None, None, :, :, None]
    return u * sv.reshape(b, n, SGU_W)


def mixer(h, lp, rope, ctx_k, ctx_v):
    b, n = h.shape[0], h.shape[1]
    z = h @ lp['w_in']
    o1 = Q_W
    o2 = o1 + KV_W
    o3 = o2 + KV_W
    o4 = o3 + 2 * CONV_W
    o5 = o4 + POOL_W
    q, k, v, a, pl, sg = jnp.split(z, [o1, o2, o3, o4, o5], axis=-1)
    q = rmsnorm(q.reshape(b, n, N_HEADS, HEAD_DIM), lp['q_norm_g'])
    k = rmsnorm(k.reshape(b, n, N_KV_HEADS, HEAD_DIM), lp['k_norm_g'])
    v = v.reshape(b, n, N_KV_HEADS, HEAD_DIM)
    if rope is None:
        keys, vals = k, v
    else:
        cos, sin = rope
        q = apply_rope(q, cos, sin)
        k = apply_rope(k, cos, sin)
        keys = jnp.concatenate([ctx_k.astype(k.dtype), k], axis=1)
        vals = jnp.concatenate([ctx_v.astype(v.dtype), v], axis=1)
    attn = block_attention(q, keys, vals)
    conv = conformer_conv(a, lp['conv_w'], lp['conv_b'], lp['conv_ln_g'], lp['conv_ln_b'])
    pool = multiscale_pool(pl, lp['pool_w'], lp['pool_scale'])
    sgu = spatial_gating(sg, lp['sgu_norm_g'], lp['sgu_w'], lp['sgu_b'])
    branches = (attn, conv, pool, sgu)
    merged = None
    for i in range(N_BRANCH):
        gate = jax.nn.sigmoid(h @ lp['w_gate'][:, i * D_MODEL:(i + 1) * D_MODEL]
                              + lp['b_gate'][i * D_MODEL:(i + 1) * D_MODEL])
        term = gate * (branches[i] @ lp['w_branch'][i])
        merged = term if merged is None else merged + term
    return merged @ lp['w_out'], k, v


def layer(x, mod, lp, rope, ctx_k, ctx_v):
    sh1, sc1, g1, sh2, sc2, g2 = jnp.split(mod, 6, axis=-1)
    h = rmsnorm(x, lp['norm1_g']) * (1.0 + sc1) + sh1
    m, k, v = mixer(h, lp, rope, ctx_k, ctx_v)
    x = x + g1 * m
    h = rmsnorm(x, lp['norm2_g']) * (1.0 + sc2) + sh2
    f = jnp.square(jax.nn.relu(h @ lp['w_mlp_in'])) @ lp['w_mlp_out']
    return x + g2 * f, k, v


def setup_inputs(seed: int = 0) -> dict:
    key = jax.random.key(seed)
    ks = jax.random.split(key, 32)
    f32 = jnp.float32

    def nrm(k, shape, scale):
        return jax.random.normal(k, shape, f32) * scale

    D = D_MODEL
    return {
        'x_prompt': nrm(ks[0], (BATCH, SEQ, D), 1.0),
        'x_sample': nrm(ks[1], (DEC_BATCH, DEC_SEQ, D), 1.0),
        'cache_k': nrm(ks[2], (DEC_BATCH, DEPTH, PAST_LEN, N_KV_HEADS, HEAD_DIM), 1.0),
        'cache_v': nrm(ks[3], (DEC_BATCH, DEPTH, PAST_LEN, N_KV_HEADS, HEAD_DIM), 1.0),
        'c': nrm(ks[4], (DEC_BATCH, D), 1.0),
        'c_ctx': nrm(ks[5], (D,), 1.0),
        'w_mod': nrm(ks[6], (DEPTH, D, 6 * D), 0.5 * D ** -0.5),
        'b_mod': nrm(ks[7], (DEPTH, 6 * D), 0.01),
        'norm1_g': 1.0 + nrm(ks[8], (DEPTH, D), 0.05),
        'w_in': nrm(ks[9], (DEPTH, D, IN_W), D ** -0.5),
        'q_norm_g': 1.0 + nrm(ks[10], (DEPTH, HEAD_DIM), 0.05),
        'k_norm_g': 1.0 + nrm(ks[11], (DEPTH, HEAD_DIM), 0.05),
        'conv_w': nrm(ks[12], (DEPTH, CONV_K, CONV_W), CONV_K ** -0.5),
        'conv_b': nrm(ks[13], (DEPTH, CONV_W), 0.01),
        'conv_ln_g': 1.0 + nrm(ks[14], (DEPTH, CONV_W), 0.05),
        'conv_ln_b': nrm(ks[15], (DEPTH, CONV_W), 0.01),
        'pool_w': nrm(ks[16], (DEPTH, POOL_GROUPS, POOL_GC, POOL_GC), POOL_GC ** -0.5),
        'pool_scale': 1.0 + nrm(ks[17], (DEPTH, POOL_W), 0.05),
        'sgu_norm_g': 1.0 + nrm(ks[18], (DEPTH, SGU_W), 0.05),
        'sgu_w': nrm(ks[19], (DEPTH, SGU_GROUPS, SGU_CHUNK, SGU_CHUNK), SGU_CHUNK ** -0.5),
        'sgu_b': 1.0 + nrm(ks[20], (DEPTH, SGU_GROUPS, SGU_CHUNK), 0.01),
        'w_branch': nrm(ks[21], (DEPTH, N_BRANCH, BRANCH_W, D), BRANCH_W ** -0.5),
        'w_gate': nrm(ks[22], (DEPTH, D, N_BRANCH * D), D ** -0.5),
        'b_gate': nrm(ks[23], (DEPTH, N_BRANCH * D), 0.01),
        'w_out': nrm(ks[24], (DEPTH, D, D), D ** -0.5),
        'norm2_g': 1.0 + nrm(ks[25], (DEPTH, D), 0.05),
        'w_mlp_in': nrm(ks[26], (DEPTH, D, D_FF), D ** -0.5),
        'w_mlp_out': nrm(ks[27], (DEPTH, D_FF, D), D_FF ** -0.5),
        'final_norm_g': 1.0 + nrm(ks[28], (D,), 0.05),
    }


def reference(x_prompt, x_sample, cache_k, cache_v, c, c_ctx, w_mod, b_mod, norm1_g, w_in,
              q_norm_g, k_norm_g, conv_w, conv_b, conv_ln_g, conv_ln_b, pool_w, pool_scale,
              sgu_norm_g, sgu_w, sgu_b, w_branch, w_gate, b_gate, w_out, norm2_g,
              w_mlp_in, w_mlp_out, final_norm_g):
    rope = axial_rope(x_sample.shape[1])
    xp = x_prompt
    xs = x_sample
    new_k = []
    new_v = []
    for l in range(DEPTH):
        lp = {
            'norm1_g': norm1_g[l], 'w_in': w_in[l], 'q_norm_g': q_norm_g[l], 'k_norm_g': k_norm_g[l],
            'conv_w': conv_w[l], 'conv_b': conv_b[l], 'conv_ln_g': conv_ln_g[l], 'conv_ln_b': conv_ln_b[l],
            'pool_w': pool_w[l], 'pool_scale': pool_scale[l], 'sgu_norm_g': sgu_norm_g[l],
            'sgu_w': sgu_w[l], 'sgu_b': sgu_b[l], 'w_branch': w_branch[l], 'w_gate': w_gate[l],
            'b_gate': b_gate[l], 'w_out': w_out[l], 'norm2_g': norm2_g[l],
            'w_mlp_in': w_mlp_in[l], 'w_mlp_out': w_mlp_out[l],
        }
        mod_ctx = (jax.nn.silu(c_ctx)[None, :] @ w_mod[l] + b_mod[l])[:, None, :]
        mod_lat = (jax.nn.silu(c) @ w_mod[l] + b_mod[l])[:, None, :]
        xp, kp, vp = layer(xp, mod_ctx, lp, None, None, None)
        new_k.append(kp)
        new_v.append(vp)
        xs, _, _ = layer(xs, mod_lat, lp, rope, cache_k[:, l], cache_v[:, l])
    y_prompt = rmsnorm(xp, final_norm_g)
    y_sample = rmsnorm(xs, final_norm_g)
    new_cache_k = jnp.stack(new_k, axis=1)
    new_cache_v = jnp.stack(new_v, axis=1)
    return (y_prompt, y_sample, new_cache_k, new_cache_v)
```

```python
import functools

import jax
import jax.numpy as jnp
from jax import lax
from jax.experimental import pallas as pl
from jax.experimental.pallas import tpu as pltpu

D_MODEL = 1024
DEPTH = 2
GRID_W = 64
N_HEADS = 8
N_KV_HEADS = 2
HEAD_DIM = 64
Q_W = N_HEADS * HEAD_DIM
KV_W = N_KV_HEADS * HEAD_DIM
AXIS_DIM = HEAD_DIM // 2
ROPE_THETA = 10000.0
CONV_W = 512
CONV_K = 31
POOL_W = 512
POOL_GROUPS = 4
POOL_GC = POOL_W // POOL_GROUPS
POOL_WINDOWS = (2, 4, 8, 16)
SGU_W = 512
SGU_GROUPS = 4
SGU_GC = SGU_W // SGU_GROUPS
SGU_CHUNK = 128
BRANCH_W = 512
N_BRANCH = 4
D_FF = 4 * D_MODEL
IN_W = Q_W + 2 * KV_W + 2 * CONV_W + POOL_W + 2 * SGU_W
EPS = 1e-6

O_Q = 0
O_KV = Q_W
O_A = Q_W + 2 * KV_W
O_P = O_A + 2 * CONV_W
O_S = O_P + POOL_W

LANES = 128
HALO = 16
TOK_TILE = 512
Q_TILE = 256
KEY_BLOCK = 512
MOD_ROWS = 16
MOD_TILE = 1536
VMEM_LIMIT = 56 * 1024 * 1024

HEAD_ORDER = (0, 4, 1, 5, 2, 6, 3, 7)

_F32 = jnp.float32
_BF16 = jnp.bfloat16


def _dot(a, b):
    return jnp.dot(a, b, preferred_element_type=_F32)


def _cparams(sem):
    return pltpu.CompilerParams(dimension_semantics=sem, vmem_limit_bytes=VMEM_LIMIT)


def _mod_kernel(c_ref, w_ref, b_ref, o_ref):
    c = c_ref[...]
    cs = c * jax.nn.sigmoid(c)
    o_ref[0] = _dot(cs.astype(_BF16), w_ref[0].astype(_BF16)) + b_ref[0]


def _modulation(c_all, w_mod, b_mod):
    n_col = 6 * D_MODEL
    return pl.pallas_call(
        _mod_kernel,
        grid=(DEPTH, n_col // MOD_TILE),
        in_specs=[
            pl.BlockSpec((MOD_ROWS, D_MODEL), lambda l, j: (0, 0)),
            pl.BlockSpec((1, D_MODEL, MOD_TILE), lambda l, j: (l, 0, j)),
            pl.BlockSpec((1, 1, MOD_TILE), lambda l, j: (l, 0, j)),
        ],
        out_specs=pl.BlockSpec((1, MOD_ROWS, MOD_TILE), lambda l, j: (l, 0, j)),
        out_shape=jax.ShapeDtypeStruct((DEPTH, MOD_ROWS, n_col), _F32),
        compiler_params=_cparams(("parallel", "parallel")),
        name="modulation",
    )(c_all, w_mod, b_mod.reshape(DEPTH, 1, n_col))


def _rope(x, cos_ref, se_ref, so_ref):
    cos, se, so = cos_ref[...], se_ref[...], so_ref[...]
    outs = []
    for c in range(x.shape[1] // LANES):
        xc = x[:, c * LANES:(c + 1) * LANES]
        nxt = pltpu.roll(xc, LANES - 1, 1)
        prv = pltpu.roll(xc, 1, 1)
        outs.append(xc * cos + nxt * se + prv * so)
    return outs[0] if len(outs) == 1 else jnp.concatenate(outs, axis=1)


def _pre_kernel(*refs, rope, emit_kv):
    x_ref, mod_ref, ng_ref, win_ref, qg_ref, kg_ref, indq_ref, indk_ref, sg_ref = refs[:9]
    refs = refs[9:]
    if rope:
        cos_ref, se_ref, so_ref = refs[:3]
        refs = refs[3:]
    q_ref, k_ref, vt_ref, ap_ref, u_ref, vn_ref, h_ref = refs[:7]
    if emit_kv:
        kf_ref, vf_ref = refs[7:9]

    x = x_ref[0]
    xn = x * lax.rsqrt(jnp.mean(x * x, axis=-1, keepdims=True) + EPS) * ng_ref[...]
    h = xn * (1.0 + mod_ref[0, 1:2, :]) + mod_ref[0, 0:1, :]
    hb = h.astype(_BF16)
    h_ref[0] = hb

    zq = _dot(hb, win_ref[:, O_Q:O_Q + Q_W])
    msq = _dot((zq * zq).astype(_BF16), indq_ref[...])
    qn = zq * lax.rsqrt(msq + EPS) * qg_ref[...]
    if rope:
        qn = _rope(qn, cos_ref, se_ref, so_ref)
    q_ref[0] = (qn * (HEAD_DIM ** -0.5)).astype(_BF16)

    zkv = _dot(hb, win_ref[:, O_KV:O_KV + 2 * KV_W])
    zk = zkv[:, :KV_W]
    zv = zkv[:, KV_W:]
    msk = _dot((zk * zk).astype(_BF16), indk_ref[...])
    kn = zk * lax.rsqrt(msk + EPS) * kg_ref[...]
    if emit_kv:
        kf_ref[0] = kn
        vf_ref[0] = zv
    if rope:
        kn = _rope(kn, cos_ref, se_ref, so_ref)
    k_ref[0] = kn.astype(_BF16)
    vt_ref[0] = zv.T.astype(_BF16)

    za = _dot(hb, win_ref[:, O_A:O_A + 2 * CONV_W])
    ap_ref[0, :, 0:CONV_W] = za[:, :CONV_W] * jax.nn.sigmoid(za[:, CONV_W:])
    ap_ref[0, :, CONV_W:CONV_W + POOL_W] = _dot(hb, win_ref[:, O_P:O_P + POOL_W])

    zs = jax.nn.gelu(_dot(hb, win_ref[:, O_S:O_S + 2 * SGU_W]))
    u_ref[0] = zs[:, :SGU_W].astype(_BF16)
    v = zs[:, SGU_W:]
    vn = v * lax.rsqrt(jnp.mean(v * v, axis=-1, keepdims=True) + EPS) * sg_ref[...]
    vn_ref[0] = vn.astype(_BF16)


def _pre(x, mod, lw, rope_tabs, emit_kv):
    bsz, n, _ = x.shape
    t = TOK_TILE
    rope = rope_tabs is not None
    row = lambda b, i: (b, i, 0)
    const2 = lambda b, i: (0, 0)
    in_specs = [
        pl.BlockSpec((1, t, D_MODEL), row),
        pl.BlockSpec((1, 6, D_MODEL), lambda b, i: (b, 0, 0)),
        pl.BlockSpec((1, D_MODEL), const2),
        pl.BlockSpec((D_MODEL, IN_W), const2),
        pl.BlockSpec((1, Q_W), const2),
        pl.BlockSpec((1, KV_W), const2),
        pl.BlockSpec((Q_W, Q_W), const2),
        pl.BlockSpec((KV_W, KV_W), const2),
        pl.BlockSpec((1, SGU_W), const2),
    ]
    args = [x, mod, lw['norm1_g'], lw['w_in'], lw['q_g'], lw['k_g'], lw['ind_q'], lw['ind_k'], lw['sgu_norm_g']]
    if rope:
        in_specs += [pl.BlockSpec((t, LANES), lambda b, i: (i, 0))] * 3
        args += list(rope_tabs)
    out_specs = [
        pl.BlockSpec((1, t, Q_W), row),
        pl.BlockSpec((1, t, KV_W), row),
        pl.BlockSpec((1, KV_W, t), lambda b, i: (b, 0, i)),
        pl.BlockSpec((1, t, CONV_W + POOL_W), row),
        pl.BlockSpec((1, t, SGU_W), row),
        pl.BlockSpec((1, t, SGU_W), row),
        pl.BlockSpec((1, t, D_MODEL), row),
    ]
    out_shape = [
        jax.ShapeDtypeStruct((bsz, n, Q_W), _BF16),
        jax.ShapeDtypeStruct((bsz, n, KV_W), _BF16),
        jax.ShapeDtypeStruct((bsz, KV_W, n), _BF16),
        jax.ShapeDtypeStruct((bsz, n, CONV_W + POOL_W), _F32),
        jax.ShapeDtypeStruct((bsz, n, SGU_W), _BF16),
        jax.ShapeDtypeStruct((bsz, n, SGU_W), _BF16),
        jax.ShapeDtypeStruct((bsz, n, D_MODEL), _BF16),
    ]
    if emit_kv:
        out_specs += [pl.BlockSpec((1, t, KV_W), row)] * 2
        out_shape += [jax.ShapeDtypeStruct((bsz, n, KV_W), _F32)] * 2
    return pl.pallas_call(
        functools.partial(_pre_kernel, rope=rope, emit_kv=emit_kv),
        grid=(bsz, n // t),
        in_specs=in_specs,
        out_specs=out_specs,
        out_shape=out_shape,
        compiler_params=_cparams(("parallel", "parallel")),
        name="pre_lat" if rope else "pre_ctx",
    )(*args)


def _attn_kernel(*refs, n_keys, key_block, cached):
    q_ref = refs[0]
    if cached:
        kc_ref, vtc_ref, k_ref, vt_ref, o_ref = refs[1:]
    else:
        k_ref, vt_ref, o_ref = refs[1:]
    tq = q_ref.shape[1]
    hi_half = lax.broadcasted_iota(jnp.int32, (1, LANES), 1) >= HEAD_DIM
    nt = (((1,), (1,)), ((), ()))

    def step(carry, kblk, vtblk, qh):
        m, l, acc = carry
        s = lax.dot_general(kblk, qh, nt, preferred_element_type=_F32)
        m_new = jnp.maximum(m, jnp.max(s, axis=0, keepdims=True))
        alpha = jnp.exp(m - m_new)
        p = jnp.exp(s - m_new)
        l = alpha * l + jnp.sum(p, axis=0, keepdims=True)
        acc = alpha * acc + _dot(vtblk, p.astype(_BF16))
        return m_new, l, acc

    for c in range(Q_W // LANES):
        qc = q_ref[0, :, c * LANES:(c + 1) * LANES]
        halves = []
        for j in range(N_KV_HEADS):
            qh = jnp.where(hi_half if j == 1 else jnp.logical_not(hi_half), qc, jnp.zeros_like(qc))
            rows = pl.ds(j * HEAD_DIM, HEAD_DIM)
            carry = (jnp.full((1, tq), -1e30, _F32), jnp.zeros((1, tq), _F32), jnp.zeros((HEAD_DIM, tq), _F32))
            if cached:
                carry = step(carry, kc_ref[0], vtc_ref[0, rows, :], qh)

            def body(b, carry):
                off = pl.multiple_of(b * key_block, key_block)
                return step(carry, k_ref[0, pl.ds(off, key_block), :], vt_ref[0, rows, pl.ds(off, key_block)], qh)

            _, l, acc = lax.fori_loop(0, n_keys // key_block, body, carry)
            halves.append(acc / l)
        o_ref[0, :, c * LANES:(c + 1) * LANES] = jnp.concatenate(halves, axis=0).T.astype(_BF16)


def _attention(q, k, vt, n_keys, cache=None):
    bsz, n, _ = q.shape
    tq = Q_TILE
    key_block = min(KEY_BLOCK, n_keys)
    per_seq = n_keys // tq
    qmap = lambda b, i: (b, i, 0)
    kmap = lambda b, i: (b, i // per_seq, 0)
    vmap_ = lambda b, i: (b, 0, i // per_seq)
    in_specs = [pl.BlockSpec((1, tq, Q_W), qmap)]
    args = [q]
    if cache is not None:
        ck, cvt = cache
        past = ck.shape[1]
        in_specs += [pl.BlockSpec((1, past, KV_W), lambda b, i: (b, 0, 0)),
                     pl.BlockSpec((1, KV_W, past), lambda b, i: (b, 0, 0))]
        args += [ck, cvt]
    in_specs += [pl.BlockSpec((1, n_keys, KV_W), kmap), pl.BlockSpec((1, KV_W, n_keys), vmap_)]
    args += [k, vt]
    return pl.pallas_call(
        functools.partial(_attn_kernel, n_keys=n_keys, key_block=key_block, cached=cache is not None),
        grid=(bsz, n // tq),
        in_specs=in_specs,
        out_specs=pl.BlockSpec((1, tq, Q_W), qmap),
        out_shape=jax.ShapeDtypeStruct((bsz, n, Q_W), _BF16),
        compiler_params=_cparams(("parallel", "arbitrary")),
        name="attn_lat" if cache is not None else "attn_ctx",
    )(*args)


def _mix_kernel(ap_ref, prev_ref, next_ref, u_ref, vn_ref, cw_ref, cb_ref, lg_ref, lb_ref, pw_ref, ps_ref,
                sw_ref, sb_ref, conv_ref, pool_ref, sgu_ref, buf_ref, *, seq_len):
    t = ap_ref.shape[1]
    i = pl.program_id(1)
    last = pl.num_programs(1) - 1
    buf_ref[0:HALO, :] = jnp.where(i > 0, prev_ref[0], 0.0)
    buf_ref[HALO:HALO + t, :] = ap_ref[0]
    buf_ref[HALO + t:HALO + t + HALO, :] = jnp.where(i < last, next_ref[0], 0.0)

    acc = jnp.broadcast_to(cb_ref[...], (t, CONV_W))
    for k in range(CONV_K):
        off = HALO - CONV_K // 2 + k
        acc = acc + buf_ref[off:off + t, 0:CONV_W] * cw_ref[k:k + 1, :]
    mu = jnp.mean(acc, axis=-1, keepdims=True)
    cen = acc - mu
    var = jnp.mean(cen * cen, axis=-1, keepdims=True)
    y = cen * lax.rsqrt(var + EPS) * lg_ref[...] + lb_ref[...]
    conv_ref[0] = (y * jax.nn.sigmoid(y)).astype(_BF16)

    pos = i * t + lax.broadcasted_iota(jnp.int32, (t, 1), 0)
    for g, w in enumerate(POOL_WINDOWS):
        cols = slice(CONV_W + g * POOL_GC, CONV_W + (g + 1) * POOL_GC)
        s = buf_ref[HALO - w // 2:HALO - w // 2 + t, cols]
        for d in range(1, w):
            s = s + buf_ref[HALO - w // 2 + d:HALO - w // 2 + d + t, cols]
        lo = jnp.maximum(pos - w // 2, 0)
        hi = jnp.minimum(pos - w // 2 + w, seq_len)
        pooled = s / (hi - lo).astype(_F32) - buf_ref[HALO:HALO + t, cols]
        yg = _dot(pooled.astype(_BF16), pw_ref[g])
        pool_ref[0, :, g * POOL_GC:(g + 1) * POOL_GC] = (yg * ps_ref[:, g * POOL_GC:(g + 1) * POOL_GC]).astype(_BF16)

    n_chunk = t // SGU_CHUNK
    for g in range(SGU_GROUPS):
        cols = slice(g * SGU_GC, (g + 1) * SGU_GC)
        rhs = jnp.concatenate([vn_ref[0, c * SGU_CHUNK:(c + 1) * SGU_CHUNK, cols] for c in range(n_chunk)], axis=1)
        sv = _dot(sw_ref[g], rhs)
        for c in range(n_chunk):
            rows = slice(c * SGU_CHUNK, (c + 1) * SGU_CHUNK)
            gate = sv[:, c * SGU_GC:(c + 1) * SGU_GC] + sb_ref[:, cols]
            sgu_ref[0, rows, cols] = (u_ref[0, rows, cols].astype(_F32) * gate).astype(_BF16)


def _mix(ap, u, vn, lw, t):
    bsz, n, _ = ap.shape
    hb = t // HALO
    n_hb = n // HALO
    row = lambda b, i: (b, i, 0)
    const2 = lambda b, i: (0, 0)
    const3 = lambda b, i: (0, 0, 0)
    w_all = CONV_W + POOL_W
    in_specs = [
        pl.BlockSpec((1, t, w_all), row),
        pl.BlockSpec((1, HALO, w_all), lambda b, i: (b, jnp.maximum(i * hb - 1, 0), 0)),
        pl.BlockSpec((1, HALO, w_all), lambda b, i: (b, jnp.minimum((i + 1) * hb, n_hb - 1), 0)),
        pl.BlockSpec((1, t, SGU_W), row),
        pl.BlockSpec((1, t, SGU_W), row),
        pl.BlockSpec((CONV_K, CONV_W), const2),
        pl.BlockSpec((1, CONV_W), const2),
        pl.BlockSpec((1, CONV_W), const2),
        pl.BlockSpec((1, CONV_W), const2),
        pl.BlockSpec((POOL_GROUPS, POOL_GC, POOL_GC), const3),
        pl.BlockSpec((1, POOL_W), const2),
        pl.BlockSpec((SGU_GROUPS, SGU_CHUNK, SGU_CHUNK), const3),
        pl.BlockSpec((SGU_CHUNK, SGU_W), const2),
    ]
    o_spec = pl.BlockSpec((1, t, BRANCH_W), row)
    o_shape = jax.ShapeDtypeStruct((bsz, n, BRANCH_W), _BF16)
    return pl.pallas_call(
        functools.partial(_mix_kernel, seq_len=n),
        grid=(bsz, n // t),
        in_specs=in_specs,
        out_specs=[o_spec] * 3,
        out_shape=[o_shape] * 3,
        scratch_shapes=[pltpu.VMEM((t + 2 * HALO, w_all), _F32)],
        compiler_params=_cparams(("parallel", "parallel")),
        name="mix",
    )(ap, ap, ap, u, vn, lw['conv_w'], lw['conv_b'], lw['conv_ln_g'], lw['conv_ln_b'], lw['pool_w'],
      lw['pool_scale'], lw['sgu_w'], lw['sgu_bias'])


def _merge_kernel(x_ref, h_ref, b0_ref, b1_ref, b2_ref, b3_ref, mod_ref, wg_ref, bg_ref, wb_ref, wo_ref, o_ref):
    hb = h_ref[0]
    merged = None
    for idx, br_ref in enumerate((b0_ref, b1_ref, b2_ref, b3_ref)):
        cols = slice(idx * D_MODEL, (idx + 1) * D_MODEL)
        gate = jax.nn.sigmoid(_dot(hb, wg_ref[:, cols]) + bg_ref[:, cols])
        term = gate * _dot(br_ref[0], wb_ref[idx])
        merged = term if merged is None else merged + term
    m = _dot(merged.astype(_BF16), wo_ref[...])
    o_ref[0] = x_ref[0] + mod_ref[0, 2:3, :] * m


def _merge(x, h, branches, mod, lw):
    bsz, n, _ = x.shape
    t = TOK_TILE
    row = lambda b, i: (b, i, 0)
    const2 = lambda b, i: (0, 0)
    in_specs = [
        pl.BlockSpec((1, t, D_MODEL), row),
        pl.BlockSpec((1, t, D_MODEL), row),
    ] + [pl.BlockSpec((1, t, BRANCH_W), row)] * N_BRANCH + [
        pl.BlockSpec((1, 6, D_MODEL), lambda b, i: (b, 0, 0)),
        pl.BlockSpec((D_MODEL, N_BRANCH * D_MODEL), const2),
        pl.BlockSpec((1, N_BRANCH * D_MODEL), const2),
        pl.BlockSpec((N_BRANCH, BRANCH_W, D_MODEL), lambda b, i: (0, 0, 0)),
        pl.BlockSpec((D_MODEL, D_MODEL), const2),
    ]
    return pl.pallas_call(
        _merge_kernel,
        grid=(bsz, n // t),
        in_specs=in_specs,
        out_specs=pl.BlockSpec((1, t, D_MODEL), row),
        out_shape=jax.ShapeDtypeStruct((bsz, n, D_MODEL), _F32),
        compiler_params=_cparams(("parallel", "parallel")),
        name="merge",
    )(x, h, *branches, mod, lw['w_gate'], lw['b_gate'], lw['w_branch'], lw['w_out'])


def _mlp_kernel(x_ref, mod_ref, ng_ref, w1_ref, w2_ref, fg_ref, o_ref, hid_ref, *, final_norm):
    x = x_ref[0]
    xn = x * lax.rsqrt(jnp.mean(x * x, axis=-1, keepdims=True) + EPS) * ng_ref[...]
    hb = (xn * (1.0 + mod_ref[0, 4:5, :]) + mod_ref[0, 3:4, :]).astype(_BF16)
    for c in range(D_FF // D_MODEL):
        cols = slice(c * D_MODEL, (c + 1) * D_MODEL)
        a = jnp.maximum(_dot(hb, w1_ref[:, cols]), 0.0)
        hid_ref[:, cols] = (a * a).astype(_BF16)
    y = x + mod_ref[0, 5:6, :] * _dot(hid_ref[...], w2_ref[...])
    if final_norm:
        y = y * lax.rsqrt(jnp.mean(y * y, axis=-1, keepdims=True) + EPS) * fg_ref[...]
    o_ref[0] = y


def _mlp(x, mod, lw, final_g, final_norm):
    bsz, n, _ = x.shape
    t = TOK_TILE
    row = lambda b, i: (b, i, 0)
    const2 = lambda b, i: (0, 0)
    return pl.pallas_call(
        functools.partial(_mlp_kernel, final_norm=final_norm),
        grid=(bsz, n // t),
        in_specs=[
            pl.BlockSpec((1, t, D_MODEL), row),
            pl.BlockSpec((1, 6, D_MODEL), lambda b, i: (b, 0, 0)),
            pl.BlockSpec((1, D_MODEL), const2),
            pl.BlockSpec((D_MODEL, D_FF), const2),
            pl.BlockSpec((D_FF, D_MODEL), const2),
            pl.BlockSpec((1, D_MODEL), const2),
        ],
        out_specs=pl.BlockSpec((1, t, D_MODEL), row),
        out_shape=jax.ShapeDtypeStruct((bsz, n, D_MODEL), _F32),
        scratch_shapes=[pltpu.VMEM((t, D_FF), _BF16)],
        compiler_params=_cparams(("parallel", "parallel")),
        name="mlp",
    )(x, mod, lw['norm2_g'], lw['w_mlp_in'], lw['w_mlp_out'], final_g)


def _rope_tables(n):
    rows = n // GRID_W
    row = jnp.repeat(jnp.arange(rows, dtype=_F32), GRID_W)
    col = jnp.tile(jnp.arange(GRID_W, dtype=_F32), rows)
    inv = ROPE_THETA ** (-jnp.arange(0, AXIS_DIM, 2, dtype=_F32) / AXIS_DIM)
    ang = jnp.concatenate([row[:, None] * inv, col[:, None] * inv], axis=-1)
    cos = jnp.repeat(jnp.cos(ang), 2, axis=-1)
    sin = jnp.repeat(jnp.sin(ang), 2, axis=-1)
    even = (jnp.arange(HEAD_DIM) % 2 == 0)[None, :]
    se = jnp.where(even, -sin, 0.0)
    so = jnp.where(even, 0.0, sin)
    rep = LANES // HEAD_DIM
    return tuple(jnp.tile(tab, (1, rep)) for tab in (cos, se, so))


def _head_indicator(width):
    head = jnp.arange(width) // HEAD_DIM
    return ((head[:, None] == head[None, :]).astype(_F32) / HEAD_DIM).astype(_BF16)


def _layer_weights(l, w_in, norm1_g, q_norm_g, k_norm_g, conv_w, conv_b, conv_ln_g, conv_ln_b, pool_w,
                   pool_scale, sgu_norm_g, sgu_w, sgu_b, w_branch, w_gate, b_gate, w_out, norm2_g,
                   w_mlp_in, w_mlp_out):
    order = jnp.array(HEAD_ORDER)
    q_cols = (order[:, None] * HEAD_DIM + jnp.arange(HEAD_DIM)[None, :]).reshape(-1)
    win = w_in[l]
    win = jnp.concatenate([win[:, q_cols], win[:, Q_W:]], axis=1).astype(_BF16)
    wb = w_branch[l]
    wb = jnp.concatenate([wb[0][q_cols][None], wb[1:]], axis=0).astype(_BF16)
    row = lambda v: v.reshape(1, -1)
    return {
        'norm1_g': row(norm1_g[l]),
        'w_in': win,
        'q_g': row(jnp.tile(q_norm_g[l], N_HEADS)),
        'k_g': row(jnp.tile(k_norm_g[l], N_KV_HEADS)),
        'ind_q': _head_indicator(Q_W),
        'ind_k': _head_indicator(KV_W),
        'sgu_norm_g': row(sgu_norm_g[l]),
        'conv_w': conv_w[l],
        'conv_b': row(conv_b[l]),
        'conv_ln_g': row(conv_ln_g[l]),
        'conv_ln_b': row(conv_ln_b[l]),
        'pool_w': pool_w[l].astype(_BF16),
        'pool_scale': row(pool_scale[l]),
        'sgu_w': sgu_w[l].astype(_BF16),
        'sgu_bias': jnp.repeat(sgu_b[l].T, SGU_GC, axis=1),
        'w_gate': w_gate[l].astype(_BF16),
        'b_gate': row(b_gate[l]),
        'w_branch': wb,
        'w_out': w_out[l].astype(_BF16),
        'norm2_g': row(norm2_g[l]),
        'w_mlp_in': w_mlp_in[l].astype(_BF16),
        'w_mlp_out': w_mlp_out[l].astype(_BF16),
    }


def kernel(x_prompt, x_sample, cache_k, cache_v, c, c_ctx, w_mod, b_mod, norm1_g, w_in, q_norm_g, k_norm_g, conv_w, conv_b, conv_ln_g, conv_ln_b, pool_w, pool_scale, sgu_norm_g, sgu_w, sgu_b, w_branch, w_gate, b_gate, w_out, norm2_g, w_mlp_in, w_mlp_out, final_norm_g):
    batch, seq, _ = x_prompt.shape
    dec_batch, dec_seq, _ = x_sample.shape
    past = cache_k.shape[2]
    n_ctx = batch * seq

    c_all = jnp.concatenate([c, c_ctx[None, :], jnp.zeros((MOD_ROWS - dec_batch - 1, D_MODEL), _F32)], axis=0)
    mod = _modulation(c_all, w_mod, b_mod).reshape(DEPTH, MOD_ROWS, 6, D_MODEL)

    rope_tabs = _rope_tables(dec_seq)
    final_g = final_norm_g.reshape(1, D_MODEL)

    xp = x_prompt.reshape(1, n_ctx, D_MODEL)
    xs = x_sample
    new_k, new_v = [], []
    for l in range(DEPTH):
        lw = _layer_weights(l, w_in, norm1_g, q_norm_g, k_norm_g, conv_w, conv_b, conv_ln_g, conv_ln_b, pool_w,
                            pool_scale, sgu_norm_g, sgu_w, sgu_b, w_branch, w_gate, b_gate, w_out, norm2_g,
                            w_mlp_in, w_mlp_out)
        mod_lat = mod[l, :dec_batch]
        mod_ctx = mod[l, dec_batch:dec_batch + 1]
        last = l == DEPTH - 1

        q, k, vt, ap, u, vn, h, kf, vf = _pre(xp, mod_ctx, lw, None, True)
        new_k.append(kf.reshape(batch, seq, N_KV_HEADS, HEAD_DIM))
        new_v.append(vf.reshape(batch, seq, N_KV_HEADS, HEAD_DIM))
        attn = _attention(q, k, vt, seq)
        conv, pool, sgu = _mix(ap.reshape(batch, seq, -1), u.reshape(batch, seq, -1), vn.reshape(batch, seq, -1),
                               lw, seq)
        branches = (attn, conv.reshape(1, n_ctx, -1), pool.reshape(1, n_ctx, -1), sgu.reshape(1, n_ctx, -1))
        xp = _merge(xp, h, branches, mod_ctx, lw)
        xp = _mlp(xp, mod_ctx, lw, final_g, last)

        q, k, vt, ap, u, vn, h = _pre(xs, mod_lat, lw, rope_tabs, False)
        ck = cache_k[:, l].reshape(dec_batch, past, KV_W).astype(_BF16)
        cvt = jnp.swapaxes(cache_v[:, l].reshape(dec_batch, past, KV_W), 1, 2).astype(_BF16)
        attn = _attention(q, k, vt, dec_seq, cache=(ck, cvt))
        conv, pool, sgu = _mix(ap, u, vn, lw, TOK_TILE)
        xs = _merge(xs, h, (attn, conv, pool, sgu), mod_lat, lw)
        xs = _mlp(xs, mod_lat, lw, final_g, last)

    y_prompt = xp.reshape(batch, seq, D_MODEL)
    return (y_prompt, xs, jnp.stack(new_k, axis=1), jnp.stack(new_v, axis=1))
```

```python
import functools

import jax
import jax.numpy as jnp
from jax import lax
from jax.experimental import pallas as pl
from jax.experimental.pallas import tpu as pltpu

D_MODEL = 1024
DEPTH = 2
GRID_W = 64
N_HEADS = 8
N_KV_HEADS = 2
HEAD_DIM = 64
Q_W = N_HEADS * HEAD_DIM
KV_W = N_KV_HEADS * HEAD_DIM
GROUP = N_HEADS // N_KV_HEADS
AXIS_DIM = HEAD_DIM // 2
ROPE_THETA = 10000.0
CONV_W = 512
CONV_K = 31
POOL_W = 512
POOL_GROUPS = 4
POOL_GC = POOL_W // POOL_GROUPS
POOL_WINDOWS = (2, 4, 8, 16)
SGU_W = 512
SGU_GROUPS = 4
SGU_GC = SGU_W // SGU_GROUPS
SGU_CHUNK = 128
BRANCH_W = 512
N_BRANCH = 4
D_FF = 4 * D_MODEL
IN_W = Q_W + 2 * KV_W + 2 * CONV_W + POOL_W + 2 * SGU_W
EPS = 1e-6

O_Q = 0
O_KV = Q_W
O_A = Q_W + 2 * KV_W
O_P = O_A + 2 * CONV_W
O_S = O_P + POOL_W

LANES = 128
HALO = 16
TOK_TILE = 512
Q_TILE = 256
KEY_BLOCK = 512
ONES_ROWS = 16
VT_HEAD_ROWS = HEAD_DIM + ONES_ROWS
VT_ROWS = N_KV_HEADS * VT_HEAD_ROWS
MOD_ROWS = 16
MOD_TILE = 1536
VMEM_LIMIT = 56 * 1024 * 1024
Q_SCALE = HEAD_DIM ** -0.5 * 1.4426950408889634

HEAD_ORDER = (0, 4, 1, 5, 2, 6, 3, 7)

_F32 = jnp.float32
_BF16 = jnp.bfloat16


def _dot(a, b):
    return jnp.dot(a, b, preferred_element_type=_F32)


def _cparams(sem):
    return pltpu.CompilerParams(dimension_semantics=sem, vmem_limit_bytes=VMEM_LIMIT)


def _mod_kernel(c_ref, w_ref, b_ref, o_ref):
    c = c_ref[...]
    cs = c * jax.nn.sigmoid(c)
    o_ref[0] = _dot(cs.astype(_BF16), w_ref[0].astype(_BF16)) + b_ref[0]


def _modulation(c_all, w_mod, b_mod):
    n_col = 6 * D_MODEL
    return pl.pallas_call(
        _mod_kernel,
        grid=(DEPTH, n_col // MOD_TILE),
        in_specs=[
            pl.BlockSpec((MOD_ROWS, D_MODEL), lambda l, j: (0, 0)),
            pl.BlockSpec((1, D_MODEL, MOD_TILE), lambda l, j: (l, 0, j)),
            pl.BlockSpec((1, 1, MOD_TILE), lambda l, j: (l, 0, j)),
        ],
        out_specs=pl.BlockSpec((1, MOD_ROWS, MOD_TILE), lambda l, j: (l, 0, j)),
        out_shape=jax.ShapeDtypeStruct((DEPTH, MOD_ROWS, n_col), _F32),
        compiler_params=_cparams(("parallel", "parallel")),
        name="modulation",
    )(c_all, w_mod, b_mod.reshape(DEPTH, 1, n_col))


def _rope(x, cos_ref, se_ref, so_ref):
    cos, se, so = cos_ref[...], se_ref[...], so_ref[...]
    outs = []
    for c in range(x.shape[1] // LANES):
        xc = x[:, c * LANES:(c + 1) * LANES]
        nxt = pltpu.roll(xc, LANES - 1, 1)
        prv = pltpu.roll(xc, 1, 1)
        outs.append(xc * cos + nxt * se + prv * so)
    return outs[0] if len(outs) == 1 else jnp.concatenate(outs, axis=1)


def _pre_kernel(*refs, rope, emit_kv):
    x_ref, mod_ref, ng_ref, win_ref, qg_ref, kg_ref, indq_ref, indk_ref, sg_ref = refs[:9]
    refs = refs[9:]
    if rope:
        cos_ref, se_ref, so_ref = refs[:3]
        refs = refs[3:]
    q_ref, k_ref, vt_ref, ap_ref, u_ref, vn_ref, h_ref = refs[:7]
    if emit_kv:
        kf_ref, vf_ref = refs[7:9]

    x = x_ref[0]
    xn = x * lax.rsqrt(jnp.mean(x * x, axis=-1, keepdims=True) + EPS) * ng_ref[...]
    h = xn * (1.0 + mod_ref[0, 1:2, :]) + mod_ref[0, 0:1, :]
    hb = h.astype(_BF16)
    h_ref[0] = hb

    zq = _dot(hb, win_ref[:, O_Q:O_Q + Q_W])
    msq = _dot((zq * zq).astype(_BF16), indq_ref[...])
    qn = zq * lax.rsqrt(msq + EPS) * qg_ref[...]
    if rope:
        qn = _rope(qn, cos_ref, se_ref, so_ref)
    q_ref[0] = (qn * Q_SCALE).astype(_BF16)

    zkv = _dot(hb, win_ref[:, O_KV:O_KV + 2 * KV_W])
    zk = zkv[:, :KV_W]
    zv = zkv[:, KV_W:]
    msk = _dot((zk * zk).astype(_BF16), indk_ref[...])
    kn = zk * lax.rsqrt(msk + EPS) * kg_ref[...]
    if emit_kv:
        kf_ref[0] = kn
        vf_ref[0] = zv
    if rope:
        kn = _rope(kn, cos_ref, se_ref, so_ref)
    k_ref[0] = kn.astype(_BF16)
    vt = zv.T.astype(_BF16)
    for j in range(N_KV_HEADS):
        vt_ref[0, j * VT_HEAD_ROWS:j * VT_HEAD_ROWS + HEAD_DIM, :] = vt[j * HEAD_DIM:(j + 1) * HEAD_DIM, :]
        vt_ref[0, j * VT_HEAD_ROWS + HEAD_DIM:(j + 1) * VT_HEAD_ROWS, :] = jnp.ones((ONES_ROWS, vt.shape[1]), _BF16)

    za = _dot(hb, win_ref[:, O_A:O_A + 2 * CONV_W])
    ap_ref[0, :, 0:CONV_W] = za[:, :CONV_W] * jax.nn.sigmoid(za[:, CONV_W:])
    ap_ref[0, :, CONV_W:CONV_W + POOL_W] = _dot(hb, win_ref[:, O_P:O_P + POOL_W])

    zs = jax.nn.gelu(_dot(hb, win_ref[:, O_S:O_S + 2 * SGU_W]))
    u_ref[0] = zs[:, :SGU_W].astype(_BF16)
    v = zs[:, SGU_W:]
    vn = v * lax.rsqrt(jnp.mean(v * v, axis=-1, keepdims=True) + EPS) * sg_ref[...]
    vn_ref[0] = vn.astype(_BF16)


def _pre(x, mod, lw, rope_tabs, emit_kv):
    bsz, n, _ = x.shape
    t = TOK_TILE
    rope = rope_tabs is not None
    row = lambda b, i: (b, i, 0)
    const2 = lambda b, i: (0, 0)
    in_specs = [
        pl.BlockSpec((1, t, D_MODEL), row),
        pl.BlockSpec((1, 6, D_MODEL), lambda b, i: (b, 0, 0)),
        pl.BlockSpec((1, D_MODEL), const2),
        pl.BlockSpec((D_MODEL, IN_W), const2),
        pl.BlockSpec((1, Q_W), const2),
        pl.BlockSpec((1, KV_W), const2),
        pl.BlockSpec((Q_W, Q_W), const2),
        pl.BlockSpec((KV_W, KV_W), const2),
        pl.BlockSpec((1, SGU_W), const2),
    ]
    args = [x, mod, lw['norm1_g'], lw['w_in'], lw['q_g'], lw['k_g'], lw['ind_q'], lw['ind_k'], lw['sgu_norm_g']]
    if rope:
        in_specs += [pl.BlockSpec((t, LANES), lambda b, i: (i, 0))] * 3
        args += list(rope_tabs)
    out_specs = [
        pl.BlockSpec((1, t, Q_W), row),
        pl.BlockSpec((1, t, KV_W), row),
        pl.BlockSpec((1, VT_ROWS, t), lambda b, i: (b, 0, i)),
        pl.BlockSpec((1, t, CONV_W + POOL_W), row),
        pl.BlockSpec((1, t, SGU_W), row),
        pl.BlockSpec((1, t, SGU_W), row),
        pl.BlockSpec((1, t, D_MODEL), row),
    ]
    out_shape = [
        jax.ShapeDtypeStruct((bsz, n, Q_W), _BF16),
        jax.ShapeDtypeStruct((bsz, n, KV_W), _BF16),
        jax.ShapeDtypeStruct((bsz, VT_ROWS, n), _BF16),
        jax.ShapeDtypeStruct((bsz, n, CONV_W + POOL_W), _F32),
        jax.ShapeDtypeStruct((bsz, n, SGU_W), _BF16),
        jax.ShapeDtypeStruct((bsz, n, SGU_W), _BF16),
        jax.ShapeDtypeStruct((bsz, n, D_MODEL), _BF16),
    ]
    if emit_kv:
        out_specs += [pl.BlockSpec((1, t, KV_W), row)] * 2
        out_shape += [jax.ShapeDtypeStruct((bsz, n, KV_W), _F32)] * 2
    return pl.pallas_call(
        functools.partial(_pre_kernel, rope=rope, emit_kv=emit_kv),
        grid=(bsz, n // t),
        in_specs=in_specs,
        out_specs=out_specs,
        out_shape=out_shape,
        compiler_params=_cparams(("parallel", "parallel")),
        name="pre_lat" if rope else "pre_ctx",
    )(*args)


def _attn_kernel(q_ref, k_ref, vt_ref, o_ref, qh_ref, m_ref, alpha_ref, acc_ref, s_ref, *, n_keys, key_block):
    tq = q_ref.shape[1]
    hi_half = lax.broadcasted_iota(jnp.int32, (1, LANES), 1) >= HEAD_DIM
    nt = (((1,), (1,)), ((), ()))
    for g in range(GROUP):
        qc = q_ref[0, :, g * LANES:(g + 1) * LANES]
        for j in range(N_KV_HEADS):
            qh_ref[j, g * tq:(g + 1) * tq, :] = jnp.where(hi_half if j == 1 else jnp.logical_not(hi_half), qc,
                                                          jnp.zeros_like(qc))
    m_ref[...] = jnp.full(m_ref.shape, -1e30, _F32)
    acc_ref[...] = jnp.zeros(acc_ref.shape, _F32)

    def scores(off, j):
        s = lax.dot_general(k_ref[0, pl.ds(off, key_block), :], qh_ref[j], nt, preferred_element_type=_F32)
        m_old = m_ref[j]
        m_new = jnp.maximum(m_old, jnp.max(s, axis=0, keepdims=True))
        m_ref[j] = m_new
        alpha_ref[j] = jnp.exp2(m_old - m_new)
        s_ref[j] = s

    def values(off, j):
        p = jnp.exp2(s_ref[j] - m_ref[j]).astype(_BF16)
        vblk = vt_ref[0, j * VT_HEAD_ROWS:(j + 1) * VT_HEAD_ROWS, pl.ds(off, key_block)]
        acc_ref[j] = alpha_ref[j] * acc_ref[j] + _dot(vblk, p)

    n_blocks = n_keys // key_block
    scores(0, 0)

    def body(b, carry):
        off = pl.multiple_of(b * key_block, key_block)
        scores(off, 1)
        values(off, 0)
        scores(off + key_block, 0)
        values(off, 1)
        return carry

    lax.fori_loop(0, n_blocks - 1, body, 0)
    last = (n_blocks - 1) * key_block
    scores(last, 1)
    values(last, 0)
    values(last, 1)

    for g in range(GROUP):
        halves = []
        for j in range(N_KV_HEADS):
            acc = acc_ref[j, :, g * tq:(g + 1) * tq]
            halves.append(acc[0:HEAD_DIM, :] / acc[HEAD_DIM:HEAD_DIM + 1, :])
        o_ref[0, :, g * LANES:(g + 1) * LANES] = jnp.concatenate(halves, axis=0).T.astype(_BF16)


def _attention(q, k, vt, q_run, n_keys):
    bsz, n, _ = q.shape
    tq = Q_TILE
    key_block = min(KEY_BLOCK, n_keys)
    per_run = q_run // tq
    qmap = lambda b, i: (b, i, 0)
    return pl.pallas_call(
        functools.partial(_attn_kernel, n_keys=n_keys, key_block=key_block),
        grid=(bsz, n // tq),
        in_specs=[
            pl.BlockSpec((1, tq, Q_W), qmap),
            pl.BlockSpec((1, n_keys, KV_W), lambda b, i: (b, i // per_run, 0)),
            pl.BlockSpec((1, VT_ROWS, n_keys), lambda b, i: (b, 0, i // per_run)),
        ],
        out_specs=pl.BlockSpec((1, tq, Q_W), qmap),
        out_shape=jax.ShapeDtypeStruct((bsz, n, Q_W), _BF16),
        scratch_shapes=[
            pltpu.VMEM((N_KV_HEADS, GROUP * tq, LANES), _BF16),
            pltpu.VMEM((N_KV_HEADS, 1, GROUP * tq), _F32),
            pltpu.VMEM((N_KV_HEADS, 1, GROUP * tq), _F32),
            pltpu.VMEM((N_KV_HEADS, VT_HEAD_ROWS, GROUP * tq), _F32),
            pltpu.VMEM((N_KV_HEADS, key_block, GROUP * tq), _F32),
        ],
        compiler_params=_cparams(("parallel", "arbitrary")),
        name="attn_lat" if n_keys > q_run else "attn_ctx",
    )(q, k, vt)


def _mix_kernel(ap_ref, prev_ref, next_ref, u_ref, vn_ref, cw_ref, cb_ref, lg_ref, lb_ref, pw_ref, ps_ref,
                sw_ref, sb_ref, conv_ref, pool_ref, sgu_ref, buf_ref, *, seq_len):
    t = ap_ref.shape[1]
    i = pl.program_id(1)
    last = pl.num_programs(1) - 1
    buf_ref[0:HALO, :] = jnp.where(i > 0, prev_ref[0], 0.0)
    buf_ref[HALO:HALO + t, :] = ap_ref[0]
    buf_ref[HALO + t:HALO + t + HALO, :] = jnp.where(i < last, next_ref[0], 0.0)

    acc = jnp.broadcast_to(cb_ref[...], (t, CONV_W))
    for k in range(CONV_K):
        off = HALO - CONV_K // 2 + k
        acc = acc + buf_ref[off:off + t, 0:CONV_W] * cw_ref[k:k + 1, :]
    mu = jnp.mean(acc, axis=-1, keepdims=True)
    cen = acc - mu
    var = jnp.mean(cen * cen, axis=-1, keepdims=True)
    y = cen * lax.rsqrt(var + EPS) * lg_ref[...] + lb_ref[...]
    conv_ref[0] = (y * jax.nn.sigmoid(y)).astype(_BF16)

    pos = i * t + lax.broadcasted_iota(jnp.int32, (t, 1), 0)
    for g, w in enumerate(POOL_WINDOWS):
        cols = slice(CONV_W + g * POOL_GC, CONV_W + (g + 1) * POOL_GC)
        s = buf_ref[HALO - w // 2:HALO - w // 2 + t, cols]
        for d in range(1, w):
            s = s + buf_ref[HALO - w // 2 + d:HALO - w // 2 + d + t, cols]
        lo = jnp.maximum(pos - w // 2, 0)
        hi = jnp.minimum(pos - w // 2 + w, seq_len)
        pooled = s / (hi - lo).astype(_F32) - buf_ref[HALO:HALO + t, cols]
        yg = _dot(pooled.astype(_BF16), pw_ref[g])
        pool_ref[0, :, g * POOL_GC:(g + 1) * POOL_GC] = (yg * ps_ref[:, g * POOL_GC:(g + 1) * POOL_GC]).astype(_BF16)

    n_chunk = t // SGU_CHUNK
    for g in range(SGU_GROUPS):
        cols = slice(g * SGU_GC, (g + 1) * SGU_GC)
        rhs = jnp.concatenate([vn_ref[0, c * SGU_CHUNK:(c + 1) * SGU_CHUNK, cols] for c in range(n_chunk)], axis=1)
        sv = _dot(sw_ref[g], rhs)
        for c in range(n_chunk):
            rows = slice(c * SGU_CHUNK, (c + 1) * SGU_CHUNK)
            gate = sv[:, c * SGU_GC:(c + 1) * SGU_GC] + sb_ref[:, cols]
            sgu_ref[0, rows, cols] = (u_ref[0, rows, cols].astype(_F32) * gate).astype(_BF16)


def _mix(ap, u, vn, lw, t):
    bsz, n, _ = ap.shape
    hb = t // HALO
    n_hb = n // HALO
    row = lambda b, i: (b, i, 0)
    const2 = lambda b, i: (0, 0)
    const3 = lambda b, i: (0, 0, 0)
    w_all = CONV_W + POOL_W
    in_specs = [
        pl.BlockSpec((1, t, w_all), row),
        pl.BlockSpec((1, HALO, w_all), lambda b, i: (b, jnp.maximum(i * hb - 1, 0), 0)),
        pl.BlockSpec((1, HALO, w_all), lambda b, i: (b, jnp.minimum((i + 1) * hb, n_hb - 1), 0)),
        pl.BlockSpec((1, t, SGU_W), row),
        pl.BlockSpec((1, t, SGU_W), row),
        pl.BlockSpec((CONV_K, CONV_W), const2),
        pl.BlockSpec((1, CONV_W), const2),
        pl.BlockSpec((1, CONV_W), const2),
        pl.BlockSpec((1, CONV_W), const2),
        pl.BlockSpec((POOL_GROUPS, POOL_GC, POOL_GC), const3),
        pl.BlockSpec((1, POOL_W), const2),
        pl.BlockSpec((SGU_GROUPS, SGU_CHUNK, SGU_CHUNK), const3),
        pl.BlockSpec((SGU_CHUNK, SGU_W), const2),
    ]
    o_spec = pl.BlockSpec((1, t, BRANCH_W), row)
    o_shape = jax.ShapeDtypeStruct((bsz, n, BRANCH_W), _BF16)
    return pl.pallas_call(
        functools.partial(_mix_kernel, seq_len=n),
        grid=(bsz, n // t),
        in_specs=in_specs,
        out_specs=[o_spec] * 3,
        out_shape=[o_shape] * 3,
        scratch_shapes=[pltpu.VMEM((t + 2 * HALO, w_all), _F32)],
        compiler_params=_cparams(("parallel", "parallel")),
        name="mix",
    )(ap, ap, ap, u, vn, lw['conv_w'], lw['conv_b'], lw['conv_ln_g'], lw['conv_ln_b'], lw['pool_w'],
      lw['pool_scale'], lw['sgu_w'], lw['sgu_bias'])


def _merge_kernel(x_ref, h_ref, b0_ref, b1_ref, b2_ref, b3_ref, mod_ref, wg_ref, bg_ref, wb_ref, wo_ref, o_ref):
    hb = h_ref[0]
    merged = None
    for idx, br_ref in enumerate((b0_ref, b1_ref, b2_ref, b3_ref)):
        cols = slice(idx * D_MODEL, (idx + 1) * D_MODEL)
        gate = jax.nn.sigmoid(_dot(hb, wg_ref[:, cols]) + bg_ref[:, cols])
        term = gate * _dot(br_ref[0], wb_ref[idx])
        merged = term if merged is None else merged + term
    m = _dot(merged.astype(_BF16), wo_ref[...])
    o_ref[0] = x_ref[0] + mod_ref[0, 2:3, :] * m


def _merge(x, h, branches, mod, lw):
    bsz, n, _ = x.shape
    t = TOK_TILE
    row = lambda b, i: (b, i, 0)
    const2 = lambda b, i: (0, 0)
    in_specs = [
        pl.BlockSpec((1, t, D_MODEL), row),
        pl.BlockSpec((1, t, D_MODEL), row),
    ] + [pl.BlockSpec((1, t, BRANCH_W), row)] * N_BRANCH + [
        pl.BlockSpec((1, 6, D_MODEL), lambda b, i: (b, 0, 0)),
        pl.BlockSpec((D_MODEL, N_BRANCH * D_MODEL), const2),
        pl.BlockSpec((1, N_BRANCH * D_MODEL), const2),
        pl.BlockSpec((N_BRANCH, BRANCH_W, D_MODEL), lambda b, i: (0, 0, 0)),
        pl.BlockSpec((D_MODEL, D_MODEL), const2),
    ]
    return pl.pallas_call(
        _merge_kernel,
        grid=(bsz, n // t),
        in_specs=in_specs,
        out_specs=pl.BlockSpec((1, t, D_MODEL), row),
        out_shape=jax.ShapeDtypeStruct((bsz, n, D_MODEL), _F32),
        compiler_params=_cparams(("parallel", "parallel")),
        name="merge",
    )(x, h, *branches, mod, lw['w_gate'], lw['b_gate'], lw['w_branch'], lw['w_out'])


def _mlp_kernel(x_ref, mod_ref, ng_ref, w1_ref, w2_ref, fg_ref, o_ref, hid_ref, *, final_norm):
    x = x_ref[0]
    xn = x * lax.rsqrt(jnp.mean(x * x, axis=-1, keepdims=True) + EPS) * ng_ref[...]
    hb = (xn * (1.0 + mod_ref[0, 4:5, :]) + mod_ref[0, 3:4, :]).astype(_BF16)
    for c in range(D_FF // D_MODEL):
        cols = slice(c * D_MODEL, (c + 1) * D_MODEL)
        a = jnp.maximum(_dot(hb, w1_ref[:, cols]), 0.0)
        hid_ref[:, cols] = (a * a).astype(_BF16)
    y = x + mod_ref[0, 5:6, :] * _dot(hid_ref[...], w2_ref[...])
    if final_norm:
        y = y * lax.rsqrt(jnp.mean(y * y, axis=-1, keepdims=True) + EPS) * fg_ref[...]
    o_ref[0] = y


def _mlp(x, mod, lw, final_g, final_norm):
    bsz, n, _ = x.shape
    t = TOK_TILE
    row = lambda b, i: (b, i, 0)
    const2 = lambda b, i: (0, 0)
    return pl.pallas_call(
        functools.partial(_mlp_kernel, final_norm=final_norm),
        grid=(bsz, n // t),
        in_specs=[
            pl.BlockSpec((1, t, D_MODEL), row),
            pl.BlockSpec((1, 6, D_MODEL), lambda b, i: (b, 0, 0)),
            pl.BlockSpec((1, D_MODEL), const2),
            pl.BlockSpec((D_MODEL, D_FF), const2),
            pl.BlockSpec((D_FF, D_MODEL), const2),
            pl.BlockSpec((1, D_MODEL), const2),
        ],
        out_specs=pl.BlockSpec((1, t, D_MODEL), row),
        out_shape=jax.ShapeDtypeStruct((bsz, n, D_MODEL), _F32),
        scratch_shapes=[pltpu.VMEM((t, D_FF), _BF16)],
        compiler_params=_cparams(("parallel", "parallel")),
        name="mlp",
    )(x, mod, lw['norm2_g'], lw['w_mlp_in'], lw['w_mlp_out'], final_g)


def _rope_tables(n):
    rows = n // GRID_W
    row = jnp.repeat(jnp.arange(rows, dtype=_F32), GRID_W)
    col = jnp.tile(jnp.arange(GRID_W, dtype=_F32), rows)
    inv = ROPE_THETA ** (-jnp.arange(0, AXIS_DIM, 2, dtype=_F32) / AXIS_DIM)
    ang = jnp.concatenate([row[:, None] * inv, col[:, None] * inv], axis=-1)
    cos = jnp.repeat(jnp.cos(ang), 2, axis=-1)
    sin = jnp.repeat(jnp.sin(ang), 2, axis=-1)
    even = (jnp.arange(HEAD_DIM) % 2 == 0)[None, :]
    se = jnp.where(even, -sin, 0.0)
    so = jnp.where(even, 0.0, sin)
    rep = LANES // HEAD_DIM
    return tuple(jnp.tile(tab, (1, rep)) for tab in (cos, se, so))


def _head_indicator(width):
    head = jnp.arange(width) // HEAD_DIM
    return ((head[:, None] == head[None, :]).astype(_F32) / HEAD_DIM).astype(_BF16)


def _layer_weights(l, w_in, norm1_g, q_norm_g, k_norm_g, conv_w, conv_b, conv_ln_g, conv_ln_b, pool_w,
                   pool_scale, sgu_norm_g, sgu_w, sgu_b, w_branch, w_gate, b_gate, w_out, norm2_g,
                   w_mlp_in, w_mlp_out):
    order = jnp.array(HEAD_ORDER)
    q_cols = (order[:, None] * HEAD_DIM + jnp.arange(HEAD_DIM)[None, :]).reshape(-1)
    win = w_in[l]
    win = jnp.concatenate([win[:, q_cols], win[:, Q_W:]], axis=1).astype(_BF16)
    wb = w_branch[l]
    wb = jnp.concatenate([wb[0][q_cols][None], wb[1:]], axis=0).astype(_BF16)
    row = lambda v: v.reshape(1, -1)
    return {
        'norm1_g': row(norm1_g[l]),
        'w_in': win,
        'q_g': row(jnp.tile(q_norm_g[l], N_HEADS)),
        'k_g': row(jnp.tile(k_norm_g[l], N_KV_HEADS)),
        'ind_q': _head_indicator(Q_W),
        'ind_k': _head_indicator(KV_W),
        'sgu_norm_g': row(sgu_norm_g[l]),
        'conv_w': conv_w[l],
        'conv_b': row(conv_b[l]),
        'conv_ln_g': row(conv_ln_g[l]),
        'conv_ln_b': row(conv_ln_b[l]),
        'pool_w': pool_w[l].astype(_BF16),
        'pool_scale': row(pool_scale[l]),
        'sgu_w': sgu_w[l].astype(_BF16),
        'sgu_bias': jnp.repeat(sgu_b[l].T, SGU_GC, axis=1),
        'w_gate': w_gate[l].astype(_BF16),
        'b_gate': row(b_gate[l]),
        'w_branch': wb,
        'w_out': w_out[l].astype(_BF16),
        'norm2_g': row(norm2_g[l]),
        'w_mlp_in': w_mlp_in[l].astype(_BF16),
        'w_mlp_out': w_mlp_out[l].astype(_BF16),
    }


def kernel(x_prompt, x_sample, cache_k, cache_v, c, c_ctx, w_mod, b_mod, norm1_g, w_in, q_norm_g, k_norm_g, conv_w, conv_b, conv_ln_g, conv_ln_b, pool_w, pool_scale, sgu_norm_g, sgu_w, sgu_b, w_branch, w_gate, b_gate, w_out, norm2_g, w_mlp_in, w_mlp_out, final_norm_g):
    batch, seq, _ = x_prompt.shape
    dec_batch, dec_seq, _ = x_sample.shape
    past = cache_k.shape[2]
    n_ctx = batch * seq

    c_all = jnp.concatenate([c, c_ctx[None, :], jnp.zeros((MOD_ROWS - dec_batch - 1, D_MODEL), _F32)], axis=0)
    mod = _modulation(c_all, w_mod, b_mod).reshape(DEPTH, MOD_ROWS, 6, D_MODEL)

    rope_tabs = _rope_tables(dec_seq)
    final_g = final_norm_g.reshape(1, D_MODEL)

    xp = x_prompt.reshape(1, n_ctx, D_MODEL)
    xs = x_sample
    new_k, new_v = [], []
    for l in range(DEPTH):
        lw = _layer_weights(l, w_in, norm1_g, q_norm_g, k_norm_g, conv_w, conv_b, conv_ln_g, conv_ln_b, pool_w,
                            pool_scale, sgu_norm_g, sgu_w, sgu_b, w_branch, w_gate, b_gate, w_out, norm2_g,
                            w_mlp_in, w_mlp_out)
        mod_lat = mod[l, :dec_batch]
        mod_ctx = mod[l, dec_batch:dec_batch + 1]
        last = l == DEPTH - 1

        q, k, vt, ap, u, vn, h, kf, vf = _pre(xp, mod_ctx, lw, None, True)
        new_k.append(kf.reshape(batch, seq, N_KV_HEADS, HEAD_DIM))
        new_v.append(vf.reshape(batch, seq, N_KV_HEADS, HEAD_DIM))
        attn = _attention(q, k, vt, seq, seq)
        conv, pool, sgu = _mix(ap.reshape(batch, seq, -1), u.reshape(batch, seq, -1), vn.reshape(batch, seq, -1),
                               lw, seq)
        branches = (attn, conv.reshape(1, n_ctx, -1), pool.reshape(1, n_ctx, -1), sgu.reshape(1, n_ctx, -1))
        xp = _merge(xp, h, branches, mod_ctx, lw)
        xp = _mlp(xp, mod_ctx, lw, final_g, last)

        q, k, vt, ap, u, vn, h = _pre(xs, mod_lat, lw, rope_tabs, False)
        ck = cache_k[:, l].reshape(dec_batch, past, KV_W).astype(_BF16)
        cvt = jnp.transpose(cache_v[:, l], (0, 2, 3, 1)).astype(_BF16)
        cvt = jnp.concatenate([cvt, jnp.ones((dec_batch, N_KV_HEADS, ONES_ROWS, past), _BF16)], axis=2)
        k_all = jnp.concatenate([ck, k], axis=1)
        vt_all = jnp.concatenate([cvt.reshape(dec_batch, VT_ROWS, past), vt], axis=2)
        attn = _attention(q, k_all, vt_all, dec_seq, past + dec_seq)
        conv, pool, sgu = _mix(ap, u, vn, lw, TOK_TILE)
        xs = _merge(xs, h, (attn, conv, pool, sgu), mod_lat, lw)
        xs = _mlp(xs, mod_lat, lw, final_g, last)

    y_prompt = xp.reshape(batch, seq, D_MODEL)
    return (y_prompt, xs, jnp.stack(new_k, axis=1), jnp.stack(new_v, axis=1))
```

```python
import functools

import jax
import jax.numpy as jnp
from jax import lax
from jax.experimental import pallas as pl
from jax.experimental.pallas import tpu as pltpu

D_MODEL = 1024
DEPTH = 2
GRID_W = 64
N_HEADS = 8
N_KV_HEADS = 2
HEAD_DIM = 64
Q_W = N_HEADS * HEAD_DIM
KV_W = N_KV_HEADS * HEAD_DIM
GROUP = N_HEADS // N_KV_HEADS
AXIS_DIM = HEAD_DIM // 2
ROPE_THETA = 10000.0
CONV_W = 512
CONV_K = 31
POOL_W = 512
POOL_GROUPS = 4
POOL_GC = POOL_W // POOL_GROUPS
POOL_WINDOWS = (2, 4, 8, 16)
SGU_W = 512
SGU_GROUPS = 4
SGU_GC = SGU_W // SGU_GROUPS
SGU_CHUNK = 128
BRANCH_W = 512
N_BRANCH = 4
D_FF = 4 * D_MODEL
IN_W = Q_W + 2 * KV_W + 2 * CONV_W + POOL_W + 2 * SGU_W
EPS = 1e-6

O_Q = 0
O_KV = Q_W
O_A = Q_W + 2 * KV_W
O_P = O_A + 2 * CONV_W
O_S = O_P + POOL_W

LANES = 128
SUBLANES = 8
CONV_ROWS = 32
HALO = 16
TOK_TILE = 512
Q_TILE = 256
KEY_BLOCK = 768
ONES_ROWS = 16
VT_HEAD_ROWS = HEAD_DIM + ONES_ROWS
VT_ROWS = N_KV_HEADS * VT_HEAD_ROWS
MOD_ROWS = 16
MOD_TILE = 1536
VMEM_LIMIT = 56 * 1024 * 1024
Q_SCALE = HEAD_DIM ** -0.5 * 1.4426950408889634

HEAD_ORDER = (0, 4, 1, 5, 2, 6, 3, 7)

_F32 = jnp.float32
_BF16 = jnp.bfloat16


def _dot(a, b):
    return jnp.dot(a, b, preferred_element_type=_F32)


def _cparams(sem, flags=None):
    return pltpu.CompilerParams(dimension_semantics=sem, vmem_limit_bytes=VMEM_LIMIT, flags=flags)


def _mod_kernel(c_ref, w_ref, b_ref, o_ref):
    c = c_ref[...]
    cs = c * jax.nn.sigmoid(c)
    o_ref[0] = _dot(cs.astype(_BF16), w_ref[0].astype(_BF16)) + b_ref[0]


def _modulation(c_all, w_mod, b_mod):
    n_col = 6 * D_MODEL
    return pl.pallas_call(
        _mod_kernel,
        grid=(DEPTH, n_col // MOD_TILE),
        in_specs=[
            pl.BlockSpec((MOD_ROWS, D_MODEL), lambda l, j: (0, 0)),
            pl.BlockSpec((1, D_MODEL, MOD_TILE), lambda l, j: (l, 0, j)),
            pl.BlockSpec((1, 1, MOD_TILE), lambda l, j: (l, 0, j)),
        ],
        out_specs=pl.BlockSpec((1, MOD_ROWS, MOD_TILE), lambda l, j: (l, 0, j)),
        out_shape=jax.ShapeDtypeStruct((DEPTH, MOD_ROWS, n_col), _F32),
        compiler_params=_cparams(("parallel", "parallel")),
        name="modulation",
    )(c_all, w_mod, b_mod.reshape(DEPTH, 1, n_col))


def _rope(x, cos_ref, se_ref, so_ref):
    cos, se, so = cos_ref[...], se_ref[...], so_ref[...]
    outs = []
    for c in range(x.shape[1] // LANES):
        xc = x[:, c * LANES:(c + 1) * LANES]
        nxt = pltpu.roll(xc, LANES - 1, 1)
        prv = pltpu.roll(xc, 1, 1)
        outs.append(xc * cos + nxt * se + prv * so)
    return outs[0] if len(outs) == 1 else jnp.concatenate(outs, axis=1)


def _pre_kernel(*refs, rope, emit_kv):
    x_ref, mod_ref, ng_ref, win_ref, qg_ref, kg_ref, indq_ref, indk_ref, sg_ref = refs[:9]
    refs = refs[9:]
    if rope:
        cos_ref, se_ref, so_ref = refs[:3]
        refs = refs[3:]
    q_ref, k_ref, vt_ref, ap_ref, u_ref, vn_ref, h_ref = refs[:7]
    if emit_kv:
        kf_ref, vf_ref = refs[7:9]

    x = x_ref[0]
    xn = x * lax.rsqrt(jnp.mean(x * x, axis=-1, keepdims=True) + EPS) * ng_ref[...]
    h = xn * (1.0 + mod_ref[0, 1:2, :]) + mod_ref[0, 0:1, :]
    hb = h.astype(_BF16)
    h_ref[0] = hb

    zq = _dot(hb, win_ref[:, O_Q:O_Q + Q_W])
    msq = _dot((zq * zq).astype(_BF16), indq_ref[...])
    qn = zq * lax.rsqrt(msq + EPS) * qg_ref[...]
    if rope:
        qn = _rope(qn, cos_ref, se_ref, so_ref)
    q_ref[0] = (qn * Q_SCALE).astype(_BF16)

    zkv = _dot(hb, win_ref[:, O_KV:O_KV + 2 * KV_W])
    zk = zkv[:, :KV_W]
    zv = zkv[:, KV_W:]
    msk = _dot((zk * zk).astype(_BF16), indk_ref[...])
    kn = zk * lax.rsqrt(msk + EPS) * kg_ref[...]
    if emit_kv:
        kf_ref[0] = kn
        vf_ref[0] = zv
    if rope:
        kn = _rope(kn, cos_ref, se_ref, so_ref)
    k_ref[0] = kn.astype(_BF16)
    vt = zv.T.astype(_BF16)
    for j in range(N_KV_HEADS):
        vt_ref[0, j * VT_HEAD_ROWS:j * VT_HEAD_ROWS + HEAD_DIM, :] = vt[j * HEAD_DIM:(j + 1) * HEAD_DIM, :]
        vt_ref[0, j * VT_HEAD_ROWS + HEAD_DIM:(j + 1) * VT_HEAD_ROWS, :] = jnp.ones((ONES_ROWS, vt.shape[1]), _BF16)

    za = _dot(hb, win_ref[:, O_A:O_A + 2 * CONV_W])
    ap_ref[0, :, 0:CONV_W] = za[:, :CONV_W] * jax.nn.sigmoid(za[:, CONV_W:])
    ap_ref[0, :, CONV_W:CONV_W + POOL_W] = _dot(hb, win_ref[:, O_P:O_P + POOL_W])

    zs = jax.nn.gelu(_dot(hb, win_ref[:, O_S:O_S + 2 * SGU_W]))
    u_ref[0] = zs[:, :SGU_W].astype(_BF16)
    v = zs[:, SGU_W:]
    vn = v * lax.rsqrt(jnp.mean(v * v, axis=-1, keepdims=True) + EPS) * sg_ref[...]
    vn_ref[0] = vn.astype(_BF16)


def _pre(x, mod, lw, rope_tabs, emit_kv):
    bsz, n, _ = x.shape
    t = TOK_TILE
    rope = rope_tabs is not None
    row = lambda b, i: (b, i, 0)
    const2 = lambda b, i: (0, 0)
    in_specs = [
        pl.BlockSpec((1, t, D_MODEL), row),
        pl.BlockSpec((1, 6, D_MODEL), lambda b, i: (b, 0, 0)),
        pl.BlockSpec((1, D_MODEL), const2),
        pl.BlockSpec((D_MODEL, IN_W), const2),
        pl.BlockSpec((1, Q_W), const2),
        pl.BlockSpec((1, KV_W), const2),
        pl.BlockSpec((Q_W, Q_W), const2),
        pl.BlockSpec((KV_W, KV_W), const2),
        pl.BlockSpec((1, SGU_W), const2),
    ]
    args = [x, mod, lw['norm1_g'], lw['w_in'], lw['q_g'], lw['k_g'], lw['ind_q'], lw['ind_k'], lw['sgu_norm_g']]
    if rope:
        in_specs += [pl.BlockSpec((t, LANES), lambda b, i: (i, 0))] * 3
        args += list(rope_tabs)
    out_specs = [
        pl.BlockSpec((1, t, Q_W), row),
        pl.BlockSpec((1, t, KV_W), row),
        pl.BlockSpec((1, VT_ROWS, t), lambda b, i: (b, 0, i)),
        pl.BlockSpec((1, t, CONV_W + POOL_W), row),
        pl.BlockSpec((1, t, SGU_W), row),
        pl.BlockSpec((1, t, SGU_W), row),
        pl.BlockSpec((1, t, D_MODEL), row),
    ]
    out_shape = [
        jax.ShapeDtypeStruct((bsz, n, Q_W), _BF16),
        jax.ShapeDtypeStruct((bsz, n, KV_W), _BF16),
        jax.ShapeDtypeStruct((bsz, VT_ROWS, n), _BF16),
        jax.ShapeDtypeStruct((bsz, n, CONV_W + POOL_W), _F32),
        jax.ShapeDtypeStruct((bsz, n, SGU_W), _BF16),
        jax.ShapeDtypeStruct((bsz, n, SGU_W), _BF16),
        jax.ShapeDtypeStruct((bsz, n, D_MODEL), _BF16),
    ]
    if emit_kv:
        out_specs += [pl.BlockSpec((1, t, KV_W), row)] * 2
        out_shape += [jax.ShapeDtypeStruct((bsz, n, KV_W), _F32)] * 2
    return pl.pallas_call(
        functools.partial(_pre_kernel, rope=rope, emit_kv=emit_kv),
        grid=(bsz, n // t),
        in_specs=in_specs,
        out_specs=out_specs,
        out_shape=out_shape,
        compiler_params=_cparams(("parallel", "parallel")),
        name="pre_lat" if rope else "pre_ctx",
    )(*args)


def _attn_kernel(q_ref, k_ref, vt_ref, o_ref, qh_ref, m_ref, alpha_ref, acc_ref, s_ref, *, n_keys, key_block):
    tq = q_ref.shape[1]
    hi_half = lax.broadcasted_iota(jnp.int32, (1, LANES), 1) >= HEAD_DIM
    nt = (((1,), (1,)), ((), ()))
    for g in range(GROUP):
        qc = q_ref[0, :, g * LANES:(g + 1) * LANES]
        for j in range(N_KV_HEADS):
            qh_ref[j, g * tq:(g + 1) * tq, :] = jnp.where(hi_half if j == 1 else jnp.logical_not(hi_half), qc,
                                                          jnp.zeros_like(qc))
    m_ref[...] = jnp.full(m_ref.shape, -1e30, _F32)
    acc_ref[...] = jnp.zeros(acc_ref.shape, _F32)

    def scores(off, j):
        s = lax.dot_general(k_ref[0, pl.ds(off, key_block), :], qh_ref[j], nt, preferred_element_type=_F32)
        m_old = m_ref[j]
        m_new = jnp.maximum(m_old, jnp.max(s, axis=0, keepdims=True))
        m_ref[j] = m_new
        alpha_ref[j] = jnp.exp2(m_old - m_new)
        s_ref[j] = s

    def values(off, j):
        p = jnp.exp2(s_ref[j] - m_ref[j]).astype(_BF16)
        vblk = vt_ref[0, j * VT_HEAD_ROWS:(j + 1) * VT_HEAD_ROWS, pl.ds(off, key_block)]
        acc_ref[j] = alpha_ref[j] * acc_ref[j] + _dot(vblk, p)

    n_blocks = n_keys // key_block
    unroll = 2 if n_blocks % 2 == 0 else 1
    scores(0, 0)

    def body(b, carry):
        for r in range(unroll):
            off = pl.multiple_of((b * unroll + r) * key_block, key_block)
            scores(off, 1)
            values(off, 0)
            scores(off + key_block, 0)
            values(off, 1)
        return carry

    lax.fori_loop(0, n_blocks // unroll - 1, body, 0)
    for r in range(unroll):
        off = (n_blocks - unroll + r) * key_block
        scores(off, 1)
        values(off, 0)
        if r < unroll - 1:
            scores(off + key_block, 0)
        values(off, 1)

    for g in range(GROUP):
        halves = []
        for j in range(N_KV_HEADS):
            acc = acc_ref[j, :, g * tq:(g + 1) * tq]
            halves.append(acc[0:HEAD_DIM, :] / acc[HEAD_DIM:HEAD_DIM + 1, :])
        o_ref[0, :, g * LANES:(g + 1) * LANES] = jnp.concatenate(halves, axis=0).T.astype(_BF16)


def _attention(q, k, vt, q_run, n_keys):
    bsz, n, _ = q.shape
    tq = Q_TILE
    key_block = min(KEY_BLOCK, n_keys)
    per_run = q_run // tq
    qmap = lambda b, i: (b, i, 0)
    return pl.pallas_call(
        functools.partial(_attn_kernel, n_keys=n_keys, key_block=key_block),
        grid=(bsz, n // tq),
        in_specs=[
            pl.BlockSpec((1, tq, Q_W), qmap),
            pl.BlockSpec((1, n_keys, KV_W), lambda b, i: (b, i // per_run, 0)),
            pl.BlockSpec((1, VT_ROWS, n_keys), lambda b, i: (b, 0, i // per_run)),
        ],
        out_specs=pl.BlockSpec((1, tq, Q_W), qmap),
        out_shape=jax.ShapeDtypeStruct((bsz, n, Q_W), _BF16),
        scratch_shapes=[
            pltpu.VMEM((N_KV_HEADS, GROUP * tq, LANES), _BF16),
            pltpu.VMEM((N_KV_HEADS, 1, GROUP * tq), _F32),
            pltpu.VMEM((N_KV_HEADS, 1, GROUP * tq), _F32),
            pltpu.VMEM((N_KV_HEADS, VT_HEAD_ROWS, GROUP * tq), _F32),
            pltpu.VMEM((N_KV_HEADS, key_block, GROUP * tq), _F32),
        ],
        compiler_params=_cparams(("parallel", "arbitrary")),
        name="attn_lat" if n_keys > q_run else "attn_ctx",
    )(q, k, vt)


def _mix_kernel(ap_ref, prev_ref, next_ref, u_ref, vn_ref, cw_ref, cb_ref, lg_ref, lb_ref, pw_ref, ps_ref,
                sw_ref, sb_ref, conv_ref, pool_ref, sgu_ref, buf_ref, xs_ref, cacc_ref, tmp_ref, *, seq_len):
    t = ap_ref.shape[1]
    i = pl.program_id(1)
    last = pl.num_programs(1) - 1
    buf_ref[0:HALO, :] = jnp.where(i > 0, prev_ref[0], 0.0)
    buf_ref[HALO:HALO + t, :] = ap_ref[0]
    buf_ref[HALO + t:HALO + t + HALO, :] = jnp.where(i < last, next_ref[0], 0.0)
    buf_ref[t + 2 * HALO:t + 2 * HALO + SUBLANES, :] = jnp.zeros((SUBLANES, buf_ref.shape[1]), _F32)

    n_sh = t + 2 * HALO - SUBLANES
    for r in range(1, SUBLANES):
        xs_ref[r - 1] = buf_ref[r:r + n_sh, 0:CONV_W]

    def conv_rows(ci, carry):
        base = pl.multiple_of(ci * CONV_ROWS, CONV_ROWS)
        acc = jnp.broadcast_to(cb_ref[...], (CONV_ROWS, CONV_W))
        for k in range(CONV_K):
            q8, r = divmod(HALO - CONV_K // 2 + k, SUBLANES)
            rows = pl.ds(base + q8 * SUBLANES, CONV_ROWS)
            x = buf_ref[rows, 0:CONV_W] if r == 0 else xs_ref[r - 1, rows, :]
            acc = acc + x * cw_ref[k:k + 1, :]
        cacc_ref[pl.ds(base, CONV_ROWS), :] = acc
        return carry

    lax.fori_loop(0, t // CONV_ROWS, conv_rows, 0)
    acc = cacc_ref[...]
    mu = jnp.mean(acc, axis=-1, keepdims=True)
    cen = acc - mu
    var = jnp.mean(cen * cen, axis=-1, keepdims=True)
    y = cen * lax.rsqrt(var + EPS) * lg_ref[...] + lb_ref[...]
    conv_ref[0] = (y * jax.nn.sigmoid(y)).astype(_BF16)

    pos = i * t + lax.broadcasted_iota(jnp.int32, (t, 1), 0)
    for g, w in enumerate(POOL_WINDOWS):
        cols = slice(CONV_W + g * POOL_GC, CONV_W + (g + 1) * POOL_GC)
        first = HALO - w // 2
        if w == 2:
            s = buf_ref[first:first + t, cols] + buf_ref[first + 1:first + 1 + t, cols]
        else:
            n = t + 2 * HALO
            tmp_ref[0, 0:n, :] = buf_ref[0:n, cols] + buf_ref[1:n + 1, cols]
            span, stage = 2, 0
            while span * 2 < w:
                n -= SUBLANES
                tmp_ref[stage + 1, 0:n, :] = tmp_ref[stage, 0:n, :] + tmp_ref[stage, span:span + n, :]
                span, stage = span * 2, stage + 1
            s = tmp_ref[stage, first:first + t, :] + tmp_ref[stage, first + span:first + span + t, :]
        lo = jnp.maximum(pos - w // 2, 0)
        hi = jnp.minimum(pos - w // 2 + w, seq_len)
        pooled = s / (hi - lo).astype(_F32) - buf_ref[HALO:HALO + t, cols]
        yg = _dot(pooled.astype(_BF16), pw_ref[g])
        pool_ref[0, :, g * POOL_GC:(g + 1) * POOL_GC] = (yg * ps_ref[:, g * POOL_GC:(g + 1) * POOL_GC]).astype(_BF16)

    n_chunk = t // SGU_CHUNK
    for g in range(SGU_GROUPS):
        cols = slice(g * SGU_GC, (g + 1) * SGU_GC)
        rhs = jnp.concatenate([vn_ref[0, c * SGU_CHUNK:(c + 1) * SGU_CHUNK, cols] for c in range(n_chunk)], axis=1)
        sv = _dot(sw_ref[g], rhs)
        for c in range(n_chunk):
            rows = slice(c * SGU_CHUNK, (c + 1) * SGU_CHUNK)
            gate = sv[:, c * SGU_GC:(c + 1) * SGU_GC] + sb_ref[:, cols]
            sgu_ref[0, rows, cols] = (u_ref[0, rows, cols].astype(_F32) * gate).astype(_BF16)


def _mix(ap, u, vn, lw, t):
    bsz, n, _ = ap.shape
    hb = t // HALO
    n_hb = n // HALO
    row = lambda b, i: (b, i, 0)
    const2 = lambda b, i: (0, 0)
    const3 = lambda b, i: (0, 0, 0)
    w_all = CONV_W + POOL_W
    in_specs = [
        pl.BlockSpec((1, t, w_all), row),
        pl.BlockSpec((1, HALO, w_all), lambda b, i: (b, jnp.maximum(i * hb - 1, 0), 0)),
        pl.BlockSpec((1, HALO, w_all), lambda b, i: (b, jnp.minimum((i + 1) * hb, n_hb - 1), 0)),
        pl.BlockSpec((1, t, SGU_W), row),
        pl.BlockSpec((1, t, SGU_W), row),
        pl.BlockSpec((CONV_K, CONV_W), const2),
        pl.BlockSpec((1, CONV_W), const2),
        pl.BlockSpec((1, CONV_W), const2),
        pl.BlockSpec((1, CONV_W), const2),
        pl.BlockSpec((POOL_GROUPS, POOL_GC, POOL_GC), const3),
        pl.BlockSpec((1, POOL_W), const2),
        pl.BlockSpec((SGU_GROUPS, SGU_CHUNK, SGU_CHUNK), const3),
        pl.BlockSpec((SGU_CHUNK, SGU_W), const2),
    ]
    o_spec = pl.BlockSpec((1, t, BRANCH_W), row)
    o_shape = jax.ShapeDtypeStruct((bsz, n, BRANCH_W), _BF16)
    return pl.pallas_call(
        functools.partial(_mix_kernel, seq_len=n),
        grid=(bsz, n // t),
        in_specs=in_specs,
        out_specs=[o_spec] * 3,
        out_shape=[o_shape] * 3,
        scratch_shapes=[
            pltpu.VMEM((t + 2 * HALO + SUBLANES, w_all), _F32),
            pltpu.VMEM((SUBLANES - 1, t + 2 * HALO - SUBLANES, CONV_W), _F32),
            pltpu.VMEM((t, CONV_W), _F32),
            pltpu.VMEM((3, t + 2 * HALO, POOL_GC), _F32),
        ],
        compiler_params=_cparams(("parallel", "parallel")),
        name="mix",
    )(ap, ap, ap, u, vn, lw['conv_w'], lw['conv_b'], lw['conv_ln_g'], lw['conv_ln_b'], lw['pool_w'],
      lw['pool_scale'], lw['sgu_w'], lw['sgu_bias'])


def _merge_kernel(x_ref, h_ref, b0_ref, b1_ref, b2_ref, b3_ref, mod_ref, wg_ref, bg_ref, wb_ref, wo_ref, o_ref):
    hb = h_ref[0]
    merged = None
    for idx, br_ref in enumerate((b0_ref, b1_ref, b2_ref, b3_ref)):
        cols = slice(idx * D_MODEL, (idx + 1) * D_MODEL)
        gate = jax.nn.sigmoid(_dot(hb, wg_ref[:, cols]) + bg_ref[:, cols])
        term = gate * _dot(br_ref[0], wb_ref[idx])
        merged = term if merged is None else merged + term
    m = _dot(merged.astype(_BF16), wo_ref[...])
    o_ref[0] = x_ref[0] + mod_ref[0, 2:3, :] * m


def _merge(x, h, branches, mod, lw):
    bsz, n, _ = x.shape
    t = TOK_TILE
    row = lambda b, i: (b, i, 0)
    const2 = lambda b, i: (0, 0)
    in_specs = [
        pl.BlockSpec((1, t, D_MODEL), row),
        pl.BlockSpec((1, t, D_MODEL), row),
    ] + [pl.BlockSpec((1, t, BRANCH_W), row)] * N_BRANCH + [
        pl.BlockSpec((1, 6, D_MODEL), lambda b, i: (b, 0, 0)),
        pl.BlockSpec((D_MODEL, N_BRANCH * D_MODEL), const2),
        pl.BlockSpec((1, N_BRANCH * D_MODEL), const2),
        pl.BlockSpec((N_BRANCH, BRANCH_W, D_MODEL), lambda b, i: (0, 0, 0)),
        pl.BlockSpec((D_MODEL, D_MODEL), const2),
    ]
    return pl.pallas_call(
        _merge_kernel,
        grid=(bsz, n // t),
        in_specs=in_specs,
        out_specs=pl.BlockSpec((1, t, D_MODEL), row),
        out_shape=jax.ShapeDtypeStruct((bsz, n, D_MODEL), _F32),
        compiler_params=_cparams(("parallel", "parallel")),
        name="merge",
    )(x, h, *branches, mod, lw['w_gate'], lw['b_gate'], lw['w_branch'], lw['w_out'])


def _mlp_kernel(x_ref, mod_ref, ng_ref, w1_ref, w2_ref, fg_ref, o_ref, hid_ref, *, final_norm):
    x = x_ref[0]
    xn = x * lax.rsqrt(jnp.mean(x * x, axis=-1, keepdims=True) + EPS) * ng_ref[...]
    hb = (xn * (1.0 + mod_ref[0, 4:5, :]) + mod_ref[0, 3:4, :]).astype(_BF16)
    for c in range(D_FF // D_MODEL):
        cols = slice(c * D_MODEL, (c + 1) * D_MODEL)
        a = jnp.maximum(_dot(hb, w1_ref[:, cols]), 0.0)
        hid_ref[:, cols] = (a * a).astype(_BF16)
    y = x + mod_ref[0, 5:6, :] * _dot(hid_ref[...], w2_ref[...])
    if final_norm:
        y = y * lax.rsqrt(jnp.mean(y * y, axis=-1, keepdims=True) + EPS) * fg_ref[...]
    o_ref[0] = y


def _mlp(x, mod, lw, final_g, final_norm):
    bsz, n, _ = x.shape
    t = TOK_TILE
    row = lambda b, i: (b, i, 0)
    const2 = lambda b, i: (0, 0)
    return pl.pallas_call(
        functools.partial(_mlp_kernel, final_norm=final_norm),
        grid=(bsz, n // t),
        in_specs=[
            pl.BlockSpec((1, t, D_MODEL), row),
            pl.BlockSpec((1, 6, D_MODEL), lambda b, i: (b, 0, 0)),
            pl.BlockSpec((1, D_MODEL), const2),
            pl.BlockSpec((D_MODEL, D_FF), const2),
            pl.BlockSpec((D_FF, D_MODEL), const2),
            pl.BlockSpec((1, D_MODEL), const2),
        ],
        out_specs=pl.BlockSpec((1, t, D_MODEL), row),
        out_shape=jax.ShapeDtypeStruct((bsz, n, D_MODEL), _F32),
        scratch_shapes=[pltpu.VMEM((t, D_FF), _BF16)],
        compiler_params=_cparams(("parallel", "parallel")),
        name="mlp",
    )(x, mod, lw['norm2_g'], lw['w_mlp_in'], lw['w_mlp_out'], final_g)


def _rope_tables(n):
    rows = n // GRID_W
    row = jnp.repeat(jnp.arange(rows, dtype=_F32), GRID_W)
    col = jnp.tile(jnp.arange(GRID_W, dtype=_F32), rows)
    inv = ROPE_THETA ** (-jnp.arange(0, AXIS_DIM, 2, dtype=_F32) / AXIS_DIM)
    ang = jnp.concatenate([row[:, None] * inv, col[:, None] * inv], axis=-1)
    cos = jnp.repeat(jnp.cos(ang), 2, axis=-1)
    sin = jnp.repeat(jnp.sin(ang), 2, axis=-1)
    even = (jnp.arange(HEAD_DIM) % 2 == 0)[None, :]
    se = jnp.where(even, -sin, 0.0)
    so = jnp.where(even, 0.0, sin)
    rep = LANES // HEAD_DIM
    return tuple(jnp.tile(tab, (1, rep)) for tab in (cos, se, so))


def _head_indicator(width):
    head = jnp.arange(width) // HEAD_DIM
    return ((head[:, None] == head[None, :]).astype(_F32) / HEAD_DIM).astype(_BF16)


def _layer_weights(l, w_in, norm1_g, q_norm_g, k_norm_g, conv_w, conv_b, conv_ln_g, conv_ln_b, pool_w,
                   pool_scale, sgu_norm_g, sgu_w, sgu_b, w_branch, w_gate, b_gate, w_out, norm2_g,
                   w_mlp_in, w_mlp_out):
    order = jnp.array(HEAD_ORDER)
    q_cols = (order[:, None] * HEAD_DIM + jnp.arange(HEAD_DIM)[None, :]).reshape(-1)
    win = w_in[l]
    win = jnp.concatenate([win[:, q_cols], win[:, Q_W:]], axis=1).astype(_BF16)
    wb = w_branch[l]
    wb = jnp.concatenate([wb[0][q_cols][None], wb[1:]], axis=0).astype(_BF16)
    row = lambda v: v.reshape(1, -1)
    return {
        'norm1_g': row(norm1_g[l]),
        'w_in': win,
        'q_g': row(jnp.tile(q_norm_g[l], N_HEADS)),
        'k_g': row(jnp.tile(k_norm_g[l], N_KV_HEADS)),
        'ind_q': _head_indicator(Q_W),
        'ind_k': _head_indicator(KV_W),
        'sgu_norm_g': row(sgu_norm_g[l]),
        'conv_w': conv_w[l],
        'conv_b': row(conv_b[l]),
        'conv_ln_g': row(conv_ln_g[l]),
        'conv_ln_b': row(conv_ln_b[l]),
        'pool_w': pool_w[l].astype(_BF16),
        'pool_scale': row(pool_scale[l]),
        'sgu_w': sgu_w[l].astype(_BF16),
        'sgu_bias': jnp.repeat(sgu_b[l].T, SGU_GC, axis=1),
        'w_gate': w_gate[l].astype(_BF16),
        'b_gate': row(b_gate[l]),
        'w_branch': wb,
        'w_out': w_out[l].astype(_BF16),
        'norm2_g': row(norm2_g[l]),
        'w_mlp_in': w_mlp_in[l].astype(_BF16),
        'w_mlp_out': w_mlp_out[l].astype(_BF16),
    }


def kernel(x_prompt, x_sample, cache_k, cache_v, c, c_ctx, w_mod, b_mod, norm1_g, w_in, q_norm_g, k_norm_g, conv_w, conv_b, conv_ln_g, conv_ln_b, pool_w, pool_scale, sgu_norm_g, sgu_w, sgu_b, w_branch, w_gate, b_gate, w_out, norm2_g, w_mlp_in, w_mlp_out, final_norm_g):
    batch, seq, _ = x_prompt.shape
    dec_batch, dec_seq, _ = x_sample.shape
    past = cache_k.shape[2]
    n_ctx = batch * seq

    c_all = jnp.concatenate([c, c_ctx[None, :], jnp.zeros((MOD_ROWS - dec_batch - 1, D_MODEL), _F32)], axis=0)
    mod = _modulation(c_all, w_mod, b_mod).reshape(DEPTH, MOD_ROWS, 6, D_MODEL)

    rope_tabs = _rope_tables(dec_seq)
    final_g = final_norm_g.reshape(1, D_MODEL)

    xp = x_prompt.reshape(1, n_ctx, D_MODEL)
    xs = x_sample
    new_k, new_v = [], []
    for l in range(DEPTH):
        lw = _layer_weights(l, w_in, norm1_g, q_norm_g, k_norm_g, conv_w, conv_b, conv_ln_g, conv_ln_b, pool_w,
                            pool_scale, sgu_norm_g, sgu_w, sgu_b, w_branch, w_gate, b_gate, w_out, norm2_g,
                            w_mlp_in, w_mlp_out)
        mod_lat = mod[l, :dec_batch]
        mod_ctx = mod[l, dec_batch:dec_batch + 1]
        last = l == DEPTH - 1

        q, k, vt, ap, u, vn, h, kf, vf = _pre(xp, mod_ctx, lw, None, True)
        new_k.append(kf.reshape(batch, seq, N_KV_HEADS, HEAD_DIM))
        new_v.append(vf.reshape(batch, seq, N_KV_HEADS, HEAD_DIM))
        attn = _attention(q, k, vt, seq, seq)
        conv, pool, sgu = _mix(ap.reshape(batch, seq, -1), u.reshape(batch, seq, -1), vn.reshape(batch, seq, -1),
                               lw, seq)
        branches = (attn, conv.reshape(1, n_ctx, -1), pool.reshape(1, n_ctx, -1), sgu.reshape(1, n_ctx, -1))
        xp = _merge(xp, h, branches, mod_ctx, lw)
        xp = _mlp(xp, mod_ctx, lw, final_g, last)

        q, k, vt, ap, u, vn, h = _pre(xs, mod_lat, lw, rope_tabs, False)
        ck = cache_k[:, l].reshape(dec_batch, past, KV_W).astype(_BF16)
        cvt = jnp.transpose(cache_v[:, l], (0, 2, 3, 1)).astype(_BF16)
        cvt = jnp.concatenate([cvt, jnp.ones((dec_batch, N_KV_HEADS, ONES_ROWS, past), _BF16)], axis=2)
        k_all = jnp.concatenate([ck, k], axis=1)
        vt_all = jnp.concatenate([cvt.reshape(dec_batch, VT_ROWS, past), vt], axis=2)
        attn = _attention(q, k_all, vt_all, dec_seq, past + dec_seq)
        conv, pool, sgu = _mix(ap, u, vn, lw, TOK_TILE)
        xs = _merge(xs, h, (attn, conv, pool, sgu), mod_lat, lw)
        xs = _mlp(xs, mod_lat, lw, final_g, last)

    y_prompt = xp.reshape(batch, seq, D_MODEL)
    return (y_prompt, xs, jnp.stack(new_k, axis=1), jnp.stack(new_v, axis=1))
```

```python
import functools

import jax
import jax.numpy as jnp
from jax import lax
from jax.experimental import pallas as pl
from jax.experimental.pallas import tpu as pltpu

D_MODEL = 1024
DEPTH = 2
GRID_W = 64
N_HEADS = 8
N_KV_HEADS = 2
HEAD_DIM = 64
Q_W = N_HEADS * HEAD_DIM
KV_W = N_KV_HEADS * HEAD_DIM
GROUP = N_HEADS // N_KV_HEADS
AXIS_DIM = HEAD_DIM // 2
ROPE_THETA = 10000.0
CONV_W = 512
CONV_K = 31
POOL_W = 512
POOL_GROUPS = 4
POOL_GC = POOL_W // POOL_GROUPS
POOL_WINDOWS = (2, 4, 8, 16)
SGU_W = 512
SGU_GROUPS = 4
SGU_GC = SGU_W // SGU_GROUPS
SGU_CHUNK = 128
BRANCH_W = 512
N_BRANCH = 4
D_FF = 4 * D_MODEL
IN_W = Q_W + 2 * KV_W + 2 * CONV_W + POOL_W + 2 * SGU_W
EPS = 1e-6

O_Q = 0
O_KV = Q_W
O_A = Q_W + 2 * KV_W
O_P = O_A + 2 * CONV_W
O_S = O_P + POOL_W

LANES = 128
SUBLANES = 8
CONV_ROWS = 64
HALO = 16
TOK_TILE = 512
Q_TILE = 256
KEY_BLOCK = 768
ONES_ROWS = 16
VT_HEAD_ROWS = HEAD_DIM + ONES_ROWS
VT_ROWS = N_KV_HEADS * VT_HEAD_ROWS
MOD_ROWS = 16
MOD_TILE = 1536
VMEM_LIMIT = 56 * 1024 * 1024
Q_SCALE = HEAD_DIM ** -0.5 * 1.4426950408889634
BOUND_MARGIN = 1.02
MAX_SCORE_BOUND = 40.0

HEAD_ORDER = (0, 4, 1, 5, 2, 6, 3, 7)

_F32 = jnp.float32
_BF16 = jnp.bfloat16


def _dot(a, b):
    return jnp.dot(a, b, preferred_element_type=_F32)


def _cparams(sem, flags=None):
    return pltpu.CompilerParams(dimension_semantics=sem, vmem_limit_bytes=VMEM_LIMIT, flags=flags)


def _mod_kernel(c_ref, w_ref, b_ref, o_ref):
    c = c_ref[...]
    cs = c * jax.nn.sigmoid(c)
    o_ref[0] = _dot(cs.astype(_BF16), w_ref[0].astype(_BF16)) + b_ref[0]


def _modulation(c_all, w_mod, b_mod):
    n_col = 6 * D_MODEL
    return pl.pallas_call(
        _mod_kernel,
        grid=(DEPTH, n_col // MOD_TILE),
        in_specs=[
            pl.BlockSpec((MOD_ROWS, D_MODEL), lambda l, j: (0, 0)),
            pl.BlockSpec((1, D_MODEL, MOD_TILE), lambda l, j: (l, 0, j)),
            pl.BlockSpec((1, 1, MOD_TILE), lambda l, j: (l, 0, j)),
        ],
        out_specs=pl.BlockSpec((1, MOD_ROWS, MOD_TILE), lambda l, j: (l, 0, j)),
        out_shape=jax.ShapeDtypeStruct((DEPTH, MOD_ROWS, n_col), _F32),
        compiler_params=_cparams(("parallel", "parallel")),
        name="modulation",
    )(c_all, w_mod, b_mod.reshape(DEPTH, 1, n_col))


def _rope(x, cos_ref, se_ref, so_ref):
    cos, se, so = cos_ref[...], se_ref[...], so_ref[...]
    outs = []
    for c in range(x.shape[1] // LANES):
        xc = x[:, c * LANES:(c + 1) * LANES]
        nxt = pltpu.roll(xc, LANES - 1, 1)
        prv = pltpu.roll(xc, 1, 1)
        outs.append(xc * cos + nxt * se + prv * so)
    return outs[0] if len(outs) == 1 else jnp.concatenate(outs, axis=1)


def _pre_kernel(*refs, rope, emit_kv):
    x_ref, mod_ref, ng_ref, win_ref, qg_ref, kg_ref, indq_ref, indk_ref, sg_ref = refs[:9]
    refs = refs[9:]
    if rope:
        cos_ref, se_ref, so_ref = refs[:3]
        refs = refs[3:]
    q_ref, k_ref, vt_ref, ap_ref, u_ref, vn_ref, h_ref = refs[:7]
    if emit_kv:
        kf_ref, vf_ref = refs[7:9]

    x = x_ref[0]
    xn = x * lax.rsqrt(jnp.mean(x * x, axis=-1, keepdims=True) + EPS) * ng_ref[...]
    h = xn * (1.0 + mod_ref[0, 1:2, :]) + mod_ref[0, 0:1, :]
    hb = h.astype(_BF16)
    h_ref[0] = hb

    zq = _dot(hb, win_ref[:, O_Q:O_Q + Q_W])
    msq = _dot((zq * zq).astype(_BF16), indq_ref[...])
    qn = zq * lax.rsqrt(msq + EPS) * qg_ref[...]
    if rope:
        qn = _rope(qn, cos_ref, se_ref, so_ref)
    q_ref[0] = (qn * Q_SCALE).astype(_BF16)

    zkv = _dot(hb, win_ref[:, O_KV:O_KV + 2 * KV_W])
    zk = zkv[:, :KV_W]
    zv = zkv[:, KV_W:]
    msk = _dot((zk * zk).astype(_BF16), indk_ref[...])
    kn = zk * lax.rsqrt(msk + EPS) * kg_ref[...]
    if emit_kv:
        kf_ref[0] = kn
        vf_ref[0] = zv
    if rope:
        kn = _rope(kn, cos_ref, se_ref, so_ref)
    k_ref[0] = kn.astype(_BF16)
    vt = zv.T.astype(_BF16)
    for j in range(N_KV_HEADS):
        vt_ref[0, j * VT_HEAD_ROWS:j * VT_HEAD_ROWS + HEAD_DIM, :] = vt[j * HEAD_DIM:(j + 1) * HEAD_DIM, :]
        vt_ref[0, j * VT_HEAD_ROWS + HEAD_DIM:(j + 1) * VT_HEAD_ROWS, :] = jnp.ones((ONES_ROWS, vt.shape[1]), _BF16)

    za = _dot(hb, win_ref[:, O_A:O_A + 2 * CONV_W])
    ap_ref[0, :, 0:CONV_W] = za[:, :CONV_W] * jax.nn.sigmoid(za[:, CONV_W:])
    ap_ref[0, :, CONV_W:CONV_W + POOL_W] = _dot(hb, win_ref[:, O_P:O_P + POOL_W])

    zs = jax.nn.gelu(_dot(hb, win_ref[:, O_S:O_S + 2 * SGU_W]))
    u_ref[0] = zs[:, :SGU_W].astype(_BF16)
    v = zs[:, SGU_W:]
    vn = v * lax.rsqrt(jnp.mean(v * v, axis=-1, keepdims=True) + EPS) * sg_ref[...]
    vn_ref[0] = vn.astype(_BF16)


def _pre(x, mod, lw, rope_tabs, emit_kv):
    bsz, n, _ = x.shape
    t = TOK_TILE
    rope = rope_tabs is not None
    row = lambda b, i: (b, i, 0)
    const2 = lambda b, i: (0, 0)
    in_specs = [
        pl.BlockSpec((1, t, D_MODEL), row),
        pl.BlockSpec((1, 6, D_MODEL), lambda b, i: (b, 0, 0)),
        pl.BlockSpec((1, D_MODEL), const2),
        pl.BlockSpec((D_MODEL, IN_W), const2),
        pl.BlockSpec((1, Q_W), const2),
        pl.BlockSpec((1, KV_W), const2),
        pl.BlockSpec((Q_W, Q_W), const2),
        pl.BlockSpec((KV_W, KV_W), const2),
        pl.BlockSpec((1, SGU_W), const2),
    ]
    args = [x, mod, lw['norm1_g'], lw['w_in'], lw['q_g'], lw['k_g'], lw['ind_q'], lw['ind_k'], lw['sgu_norm_g']]
    if rope:
        in_specs += [pl.BlockSpec((t, LANES), lambda b, i: (i, 0))] * 3
        args += list(rope_tabs)
    out_specs = [
        pl.BlockSpec((1, t, Q_W), row),
        pl.BlockSpec((1, t, KV_W), row),
        pl.BlockSpec((1, VT_ROWS, t), lambda b, i: (b, 0, i)),
        pl.BlockSpec((1, t, CONV_W + POOL_W), row),
        pl.BlockSpec((1, t, SGU_W), row),
        pl.BlockSpec((1, t, SGU_W), row),
        pl.BlockSpec((1, t, D_MODEL), row),
    ]
    out_shape = [
        jax.ShapeDtypeStruct((bsz, n, Q_W), _BF16),
        jax.ShapeDtypeStruct((bsz, n, KV_W), _BF16),
        jax.ShapeDtypeStruct((bsz, VT_ROWS, n), _BF16),
        jax.ShapeDtypeStruct((bsz, n, CONV_W + POOL_W), _F32),
        jax.ShapeDtypeStruct((bsz, n, SGU_W), _BF16),
        jax.ShapeDtypeStruct((bsz, n, SGU_W), _BF16),
        jax.ShapeDtypeStruct((bsz, n, D_MODEL), _BF16),
    ]
    if emit_kv:
        out_specs += [pl.BlockSpec((1, t, KV_W), row)] * 2
        out_shape += [jax.ShapeDtypeStruct((bsz, n, KV_W), _F32)] * 2
    return pl.pallas_call(
        functools.partial(_pre_kernel, rope=rope, emit_kv=emit_kv),
        grid=(bsz, n // t),
        in_specs=in_specs,
        out_specs=out_specs,
        out_shape=out_shape,
        compiler_params=_cparams(("parallel", "parallel")),
        name="pre_lat" if rope else "pre_ctx",
    )(*args)


def _split_heads(q_ref, qh_ref):
    tq = q_ref.shape[1]
    hi_half = lax.broadcasted_iota(jnp.int32, (1, LANES), 1) >= HEAD_DIM
    for g in range(GROUP):
        qc = q_ref[0, :, g * LANES:(g + 1) * LANES]
        for j in range(N_KV_HEADS):
            qh_ref[j, g * tq:(g + 1) * tq, :] = jnp.where(hi_half if j == 1 else jnp.logical_not(hi_half), qc,
                                                          jnp.zeros_like(qc))


def _write_heads(acc_ref, o_ref):
    tq = o_ref.shape[1]
    for g in range(GROUP):
        halves = []
        for j in range(N_KV_HEADS):
            acc = acc_ref[j, :, g * tq:(g + 1) * tq]
            halves.append(acc[0:HEAD_DIM, :] / acc[HEAD_DIM:HEAD_DIM + 1, :])
        o_ref[0, :, g * LANES:(g + 1) * LANES] = jnp.concatenate(halves, axis=0).T.astype(_BF16)


_NT = (((1,), (1,)), ((), ()))


def _attn_kernel(q_ref, k_ref, vt_ref, o_ref, qh_ref, m_ref, alpha_ref, acc_ref, s_ref, *, n_keys, key_block):
    _split_heads(q_ref, qh_ref)
    m_ref[...] = jnp.full(m_ref.shape, -1e30, _F32)
    acc_ref[...] = jnp.zeros(acc_ref.shape, _F32)

    def scores(off, j):
        s = lax.dot_general(k_ref[0, pl.ds(off, key_block), :], qh_ref[j], _NT, preferred_element_type=_F32)
        m_old = m_ref[j]
        m_new = jnp.maximum(m_old, jnp.max(s, axis=0, keepdims=True))
        m_ref[j] = m_new
        alpha_ref[j] = jnp.exp2(m_old - m_new)
        s_ref[j] = s

    def values(off, j):
        p = jnp.exp2(s_ref[j] - m_ref[j]).astype(_BF16)
        vblk = vt_ref[0, j * VT_HEAD_ROWS:(j + 1) * VT_HEAD_ROWS, pl.ds(off, key_block)]
        acc_ref[j] = alpha_ref[j] * acc_ref[j] + _dot(vblk, p)

    n_blocks = n_keys // key_block
    unroll = 2 if n_blocks % 2 == 0 else 1
    scores(0, 0)

    def body(b, carry):
        for r in range(unroll):
            off = pl.multiple_of((b * unroll + r) * key_block, key_block)
            scores(off, 1)
            values(off, 0)
            scores(off + key_block, 0)
            values(off, 1)
        return carry

    lax.fori_loop(0, n_blocks // unroll - 1, body, 0)
    for r in range(unroll):
        off = (n_blocks - unroll + r) * key_block
        scores(off, 1)
        values(off, 0)
        if r < unroll - 1:
            scores(off + key_block, 0)
        values(off, 1)
    _write_heads(acc_ref, o_ref)


def _attn_bounded_kernel(bound_ref, q_ref, k_ref, vt_ref, o_ref, qh_ref, acc_ref, *, n_keys, key_block):
    _split_heads(q_ref, qh_ref)
    acc_ref[...] = jnp.zeros(acc_ref.shape, _F32)
    shift = bound_ref[0, 0]

    def unit(off, j):
        s = lax.dot_general(k_ref[0, pl.ds(off, key_block), :], qh_ref[j], _NT, preferred_element_type=_F32)
        p = jnp.exp2(s - shift).astype(_BF16)
        vblk = vt_ref[0, j * VT_HEAD_ROWS:(j + 1) * VT_HEAD_ROWS, pl.ds(off, key_block)]
        acc_ref[j] = acc_ref[j] + _dot(vblk, p)

    n_blocks = n_keys // key_block
    unroll = 2 if n_blocks % 2 == 0 else 1

    def body(b, carry):
        for r in range(unroll):
            off = pl.multiple_of((b * unroll + r) * key_block, key_block)
            for j in range(N_KV_HEADS):
                unit(off, j)
        return carry

    lax.fori_loop(0, n_blocks // unroll, body, 0)
    _write_heads(acc_ref, o_ref)


def _attention(q, k, vt, q_run, n_keys, score_bound=None):
    bsz, n, _ = q.shape
    tq = Q_TILE
    key_block = min(KEY_BLOCK, n_keys)
    per_run = q_run // tq
    qmap = lambda b, i: (b, i, 0)
    in_specs = [
        pl.BlockSpec((1, tq, Q_W), qmap),
        pl.BlockSpec((1, n_keys, KV_W), lambda b, i: (b, i // per_run, 0)),
        pl.BlockSpec((1, VT_ROWS, n_keys), lambda b, i: (b, 0, i // per_run)),
    ]
    out_spec = pl.BlockSpec((1, tq, Q_W), qmap)
    out_shape = jax.ShapeDtypeStruct((bsz, n, Q_W), _BF16)
    qh_scratch = pltpu.VMEM((N_KV_HEADS, GROUP * tq, LANES), _BF16)
    acc_scratch = pltpu.VMEM((N_KV_HEADS, VT_HEAD_ROWS, GROUP * tq), _F32)
    row_scratch = pltpu.VMEM((N_KV_HEADS, 1, GROUP * tq), _F32)

    def online(q, k, vt):
        return pl.pallas_call(
            functools.partial(_attn_kernel, n_keys=n_keys, key_block=key_block),
            grid=(bsz, n // tq),
            in_specs=in_specs,
            out_specs=out_spec,
            out_shape=out_shape,
            scratch_shapes=[qh_scratch, row_scratch, row_scratch, acc_scratch,
                            pltpu.VMEM((N_KV_HEADS, key_block, GROUP * tq), _F32)],
            compiler_params=_cparams(("parallel", "arbitrary")),
            name="attn_online",
        )(q, k, vt)

    if score_bound is None:
        return online(q, k, vt)

    def bounded(q, k, vt):
        return pl.pallas_call(
            functools.partial(_attn_bounded_kernel, n_keys=n_keys, key_block=key_block),
            grid=(bsz, n // tq),
            in_specs=[pl.BlockSpec(memory_space=pltpu.SMEM)] + in_specs,
            out_specs=out_spec,
            out_shape=out_shape,
            scratch_shapes=[qh_scratch, acc_scratch],
            compiler_params=_cparams(("parallel", "arbitrary")),
            name="attn_bounded",
        )(score_bound.reshape(1, 1), q, k, vt)

    return lax.cond(score_bound < MAX_SCORE_BOUND, bounded, online, q, k, vt)


def _mix_kernel(ap_ref, prev_ref, next_ref, u_ref, vn_ref, cw_ref, cb_ref, lg_ref, lb_ref, pw_ref, ps_ref,
                sw_ref, sb_ref, conv_ref, pool_ref, sgu_ref, buf_ref, xs_ref, cacc_ref, tmp_ref, *, seq_len):
    t = ap_ref.shape[1]
    i = pl.program_id(1)
    last = pl.num_programs(1) - 1
    buf_ref[0:HALO, :] = jnp.where(i > 0, prev_ref[0], 0.0)
    buf_ref[HALO:HALO + t, :] = ap_ref[0]
    buf_ref[HALO + t:HALO + t + HALO, :] = jnp.where(i < last, next_ref[0], 0.0)
    buf_ref[t + 2 * HALO:t + 2 * HALO + SUBLANES, :] = jnp.zeros((SUBLANES, buf_ref.shape[1]), _F32)

    n_sh = t + 2 * HALO - SUBLANES
    for r in range(1, SUBLANES):
        xs_ref[r - 1] = buf_ref[r:r + n_sh, 0:CONV_W]

    def conv_rows(ci, carry):
        base = pl.multiple_of(ci * CONV_ROWS, CONV_ROWS)
        acc = jnp.broadcast_to(cb_ref[...], (CONV_ROWS, CONV_W))
        for k in range(CONV_K):
            q8, r = divmod(HALO - CONV_K // 2 + k, SUBLANES)
            rows = pl.ds(base + q8 * SUBLANES, CONV_ROWS)
            x = buf_ref[rows, 0:CONV_W] if r == 0 else xs_ref[r - 1, rows, :]
            acc = acc + x * cw_ref[k:k + 1, :]
        cacc_ref[pl.ds(base, CONV_ROWS), :] = acc
        return carry

    lax.fori_loop(0, t // CONV_ROWS, conv_rows, 0)
    acc = cacc_ref[...]
    mu = jnp.mean(acc, axis=-1, keepdims=True)
    cen = acc - mu
    var = jnp.mean(cen * cen, axis=-1, keepdims=True)
    y = cen * lax.rsqrt(var + EPS) * lg_ref[...] + lb_ref[...]
    conv_ref[0] = (y * jax.nn.sigmoid(y)).astype(_BF16)

    pos = i * t + lax.broadcasted_iota(jnp.int32, (t, 1), 0)
    for g, w in enumerate(POOL_WINDOWS):
        cols = slice(CONV_W + g * POOL_GC, CONV_W + (g + 1) * POOL_GC)
        first = HALO - w // 2
        if w == 2:
            s = buf_ref[first:first + t, cols] + buf_ref[first + 1:first + 1 + t, cols]
        else:
            n = t + 2 * HALO
            tmp_ref[0, 0:n, :] = buf_ref[0:n, cols] + buf_ref[1:n + 1, cols]
            span, stage = 2, 0
            while span * 2 < w:
                n -= SUBLANES
                tmp_ref[stage + 1, 0:n, :] = tmp_ref[stage, 0:n, :] + tmp_ref[stage, span:span + n, :]
                span, stage = span * 2, stage + 1
            s = tmp_ref[stage, first:first + t, :] + tmp_ref[stage, first + span:first + span + t, :]
        lo = jnp.maximum(pos - w // 2, 0)
        hi = jnp.minimum(pos - w // 2 + w, seq_len)
        pooled = s / (hi - lo).astype(_F32) - buf_ref[HALO:HALO + t, cols]
        yg = _dot(pooled.astype(_BF16), pw_ref[g])
        pool_ref[0, :, g * POOL_GC:(g + 1) * POOL_GC] = (yg * ps_ref[:, g * POOL_GC:(g + 1) * POOL_GC]).astype(_BF16)

    n_chunk = t // SGU_CHUNK
    for g in range(SGU_GROUPS):
        cols = slice(g * SGU_GC, (g + 1) * SGU_GC)
        rhs = jnp.concatenate([vn_ref[0, c * SGU_CHUNK:(c + 1) * SGU_CHUNK, cols] for c in range(n_chunk)], axis=1)
        sv = _dot(sw_ref[g], rhs)
        for c in range(n_chunk):
            rows = slice(c * SGU_CHUNK, (c + 1) * SGU_CHUNK)
            gate = sv[:, c * SGU_GC:(c + 1) * SGU_GC] + sb_ref[:, cols]
            sgu_ref[0, rows, cols] = (u_ref[0, rows, cols].astype(_F32) * gate).astype(_BF16)


def _mix(ap, u, vn, lw, t):
    bsz, n, _ = ap.shape
    hb = t // HALO
    n_hb = n // HALO
    row = lambda b, i: (b, i, 0)
    const2 = lambda b, i: (0, 0)
    const3 = lambda b, i: (0, 0, 0)
    w_all = CONV_W + POOL_W
    in_specs = [
        pl.BlockSpec((1, t, w_all), row),
        pl.BlockSpec((1, HALO, w_all), lambda b, i: (b, jnp.maximum(i * hb - 1, 0), 0)),
        pl.BlockSpec((1, HALO, w_all), lambda b, i: (b, jnp.minimum((i + 1) * hb, n_hb - 1), 0)),
        pl.BlockSpec((1, t, SGU_W), row),
        pl.BlockSpec((1, t, SGU_W), row),
        pl.BlockSpec((CONV_K, CONV_W), const2),
        pl.BlockSpec((1, CONV_W), const2),
        pl.BlockSpec((1, CONV_W), const2),
        pl.BlockSpec((1, CONV_W), const2),
        pl.BlockSpec((POOL_GROUPS, POOL_GC, POOL_GC), const3),
        pl.BlockSpec((1, POOL_W), const2),
        pl.BlockSpec((SGU_GROUPS, SGU_CHUNK, SGU_CHUNK), const3),
        pl.BlockSpec((SGU_CHUNK, SGU_W), const2),
    ]
    o_spec = pl.BlockSpec((1, t, BRANCH_W), row)
    o_shape = jax.ShapeDtypeStruct((bsz, n, BRANCH_W), _BF16)
    return pl.pallas_call(
        functools.partial(_mix_kernel, seq_len=n),
        grid=(bsz, n // t),
        in_specs=in_specs,
        out_specs=[o_spec] * 3,
        out_shape=[o_shape] * 3,
        scratch_shapes=[
            pltpu.VMEM((t + 2 * HALO + SUBLANES, w_all), _F32),
            pltpu.VMEM((SUBLANES - 1, t + 2 * HALO - SUBLANES, CONV_W), _F32),
            pltpu.VMEM((t, CONV_W), _F32),
            pltpu.VMEM((3, t + 2 * HALO, POOL_GC), _F32),
        ],
        compiler_params=_cparams(("parallel", "parallel")),
        name="mix",
    )(ap, ap, ap, u, vn, lw['conv_w'], lw['conv_b'], lw['conv_ln_g'], lw['conv_ln_b'], lw['pool_w'],
      lw['pool_scale'], lw['sgu_w'], lw['sgu_bias'])


def _merge_kernel(x_ref, h_ref, b0_ref, b1_ref, b2_ref, b3_ref, mod_ref, wg_ref, bg_ref, wb_ref, wo_ref, o_ref):
    hb = h_ref[0]
    merged = None
    for idx, br_ref in enumerate((b0_ref, b1_ref, b2_ref, b3_ref)):
        cols = slice(idx * D_MODEL, (idx + 1) * D_MODEL)
        gate = jax.nn.sigmoid(_dot(hb, wg_ref[:, cols]) + bg_ref[:, cols])
        term = gate * _dot(br_ref[0], wb_ref[idx])
        merged = term if merged is None else merged + term
    m = _dot(merged.astype(_BF16), wo_ref[...])
    o_ref[0] = x_ref[0] + mod_ref[0, 2:3, :] * m


def _merge(x, h, branches, mod, lw):
    bsz, n, _ = x.shape
    t = TOK_TILE
    row = lambda b, i: (b, i, 0)
    const2 = lambda b, i: (0, 0)
    in_specs = [
        pl.BlockSpec((1, t, D_MODEL), row),
        pl.BlockSpec((1, t, D_MODEL), row),
    ] + [pl.BlockSpec((1, t, BRANCH_W), row)] * N_BRANCH + [
        pl.BlockSpec((1, 6, D_MODEL), lambda b, i: (b, 0, 0)),
        pl.BlockSpec((D_MODEL, N_BRANCH * D_MODEL), const2),
        pl.BlockSpec((1, N_BRANCH * D_MODEL), const2),
        pl.BlockSpec((N_BRANCH, BRANCH_W, D_MODEL), lambda b, i: (0, 0, 0)),
        pl.BlockSpec((D_MODEL, D_MODEL), const2),
    ]
    return pl.pallas_call(
        _merge_kernel,
        grid=(bsz, n // t),
        in_specs=in_specs,
        out_specs=pl.BlockSpec((1, t, D_MODEL), row),
        out_shape=jax.ShapeDtypeStruct((bsz, n, D_MODEL), _F32),
        compiler_params=_cparams(("parallel", "parallel")),
        name="merge",
    )(x, h, *branches, mod, lw['w_gate'], lw['b_gate'], lw['w_branch'], lw['w_out'])


def _mlp_kernel(x_ref, mod_ref, ng_ref, w1_ref, w2_ref, fg_ref, o_ref, hid_ref, *, final_norm):
    x = x_ref[0]
    xn = x * lax.rsqrt(jnp.mean(x * x, axis=-1, keepdims=True) + EPS) * ng_ref[...]
    hb = (xn * (1.0 + mod_ref[0, 4:5, :]) + mod_ref[0, 3:4, :]).astype(_BF16)
    for c in range(D_FF // D_MODEL):
        cols = slice(c * D_MODEL, (c + 1) * D_MODEL)
        a = jnp.maximum(_dot(hb, w1_ref[:, cols]), 0.0)
        hid_ref[:, cols] = (a * a).astype(_BF16)
    y = x + mod_ref[0, 5:6, :] * _dot(hid_ref[...], w2_ref[...])
    if final_norm:
        y = y * lax.rsqrt(jnp.mean(y * y, axis=-1, keepdims=True) + EPS) * fg_ref[...]
    o_ref[0] = y


def _mlp(x, mod, lw, final_g, final_norm):
    bsz, n, _ = x.shape
    t = TOK_TILE
    row = lambda b, i: (b, i, 0)
    const2 = lambda b, i: (0, 0)
    return pl.pallas_call(
        functools.partial(_mlp_kernel, final_norm=final_norm),
        grid=(bsz, n // t),
        in_specs=[
            pl.BlockSpec((1, t, D_MODEL), row),
            pl.BlockSpec((1, 6, D_MODEL), lambda b, i: (b, 0, 0)),
            pl.BlockSpec((1, D_MODEL), const2),
            pl.BlockSpec((D_MODEL, D_FF), const2),
            pl.BlockSpec((D_FF, D_MODEL), const2),
            pl.BlockSpec((1, D_MODEL), const2),
        ],
        out_specs=pl.BlockSpec((1, t, D_MODEL), row),
        out_shape=jax.ShapeDtypeStruct((bsz, n, D_MODEL), _F32),
        scratch_shapes=[pltpu.VMEM((t, D_FF), _BF16)],
        compiler_params=_cparams(("parallel", "parallel")),
        name="mlp",
    )(x, mod, lw['norm2_g'], lw['w_mlp_in'], lw['w_mlp_out'], final_g)


def _rope_tables(n):
    rows = n // GRID_W
    row = jnp.repeat(jnp.arange(rows, dtype=_F32), GRID_W)
    col = jnp.tile(jnp.arange(GRID_W, dtype=_F32), rows)
    inv = ROPE_THETA ** (-jnp.arange(0, AXIS_DIM, 2, dtype=_F32) / AXIS_DIM)
    ang = jnp.concatenate([row[:, None] * inv, col[:, None] * inv], axis=-1)
    cos = jnp.repeat(jnp.cos(ang), 2, axis=-1)
    sin = jnp.repeat(jnp.sin(ang), 2, axis=-1)
    even = (jnp.arange(HEAD_DIM) % 2 == 0)[None, :]
    se = jnp.where(even, -sin, 0.0)
    so = jnp.where(even, 0.0, sin)
    rep = LANES // HEAD_DIM
    return tuple(jnp.tile(tab, (1, rep)) for tab in (cos, se, so))


def _head_indicator(width):
    head = jnp.arange(width) // HEAD_DIM
    return ((head[:, None] == head[None, :]).astype(_F32) / HEAD_DIM).astype(_BF16)


def _layer_weights(l, w_in, norm1_g, q_norm_g, k_norm_g, conv_w, conv_b, conv_ln_g, conv_ln_b, pool_w,
                   pool_scale, sgu_norm_g, sgu_w, sgu_b, w_branch, w_gate, b_gate, w_out, norm2_g,
                   w_mlp_in, w_mlp_out):
    order = jnp.array(HEAD_ORDER)
    q_cols = (order[:, None] * HEAD_DIM + jnp.arange(HEAD_DIM)[None, :]).reshape(-1)
    win = w_in[l]
    win = jnp.concatenate([win[:, q_cols], win[:, Q_W:]], axis=1).astype(_BF16)
    wb = w_branch[l]
    wb = jnp.concatenate([wb[0][q_cols][None], wb[1:]], axis=0).astype(_BF16)
    row = lambda v: v.reshape(1, -1)
    return {
        'norm1_g': row(norm1_g[l]),
        'w_in': win,
        'q_g': row(jnp.tile(q_norm_g[l], N_HEADS)),
        'k_g': row(jnp.tile(k_norm_g[l], N_KV_HEADS)),
        'ind_q': _head_indicator(Q_W),
        'ind_k': _head_indicator(KV_W),
        'sgu_norm_g': row(sgu_norm_g[l]),
        'conv_w': conv_w[l],
        'conv_b': row(conv_b[l]),
        'conv_ln_g': row(conv_ln_g[l]),
        'conv_ln_b': row(conv_ln_b[l]),
        'pool_w': pool_w[l].astype(_BF16),
        'pool_scale': row(pool_scale[l]),
        'sgu_w': sgu_w[l].astype(_BF16),
        'sgu_bias': jnp.repeat(sgu_b[l].T, SGU_GC, axis=1),
        'w_gate': w_gate[l].astype(_BF16),
        'b_gate': row(b_gate[l]),
        'w_branch': wb,
        'w_out': w_out[l].astype(_BF16),
        'norm2_g': row(norm2_g[l]),
        'w_mlp_in': w_mlp_in[l].astype(_BF16),
        'w_mlp_out': w_mlp_out[l].astype(_BF16),
    }


def kernel(x_prompt, x_sample, cache_k, cache_v, c, c_ctx, w_mod, b_mod, norm1_g, w_in, q_norm_g, k_norm_g, conv_w, conv_b, conv_ln_g, conv_ln_b, pool_w, pool_scale, sgu_norm_g, sgu_w, sgu_b, w_branch, w_gate, b_gate, w_out, norm2_g, w_mlp_in, w_mlp_out, final_norm_g):
    batch, seq, _ = x_prompt.shape
    dec_batch, dec_seq, _ = x_sample.shape
    past = cache_k.shape[2]
    n_ctx = batch * seq

    c_all = jnp.concatenate([c, c_ctx[None, :], jnp.zeros((MOD_ROWS - dec_batch - 1, D_MODEL), _F32)], axis=0)
    mod = _modulation(c_all, w_mod, b_mod).reshape(DEPTH, MOD_ROWS, 6, D_MODEL)

    rope_tabs = _rope_tables(dec_seq)
    final_g = final_norm_g.reshape(1, D_MODEL)

    xp = x_prompt.reshape(1, n_ctx, D_MODEL)
    xs = x_sample
    new_k, new_v = [], []
    for l in range(DEPTH):
        lw = _layer_weights(l, w_in, norm1_g, q_norm_g, k_norm_g, conv_w, conv_b, conv_ln_g, conv_ln_b, pool_w,
                            pool_scale, sgu_norm_g, sgu_w, sgu_b, w_branch, w_gate, b_gate, w_out, norm2_g,
                            w_mlp_in, w_mlp_out)
        mod_lat = mod[l, :dec_batch]
        mod_ctx = mod[l, dec_batch:dec_batch + 1]
        last = l == DEPTH - 1

        q, k, vt, ap, u, vn, h, kf, vf = _pre(xp, mod_ctx, lw, None, True)
        new_k.append(kf.reshape(batch, seq, N_KV_HEADS, HEAD_DIM))
        new_v.append(vf.reshape(batch, seq, N_KV_HEADS, HEAD_DIM))
        attn = _attention(q, k, vt, seq, seq)
        conv, pool, sgu = _mix(ap.reshape(batch, seq, -1), u.reshape(batch, seq, -1), vn.reshape(batch, seq, -1),
                               lw, seq)
        branches = (attn, conv.reshape(1, n_ctx, -1), pool.reshape(1, n_ctx, -1), sgu.reshape(1, n_ctx, -1))
        xp = _merge(xp, h, branches, mod_ctx, lw)
        xp = _mlp(xp, mod_ctx, lw, final_g, last)

        q, k, vt, ap, u, vn, h = _pre(xs, mod_lat, lw, rope_tabs, False)
        ck = cache_k[:, l].reshape(dec_batch, past, KV_W).astype(_BF16)
        cvt = jnp.transpose(cache_v[:, l], (0, 2, 3, 1)).astype(_BF16)
        cvt = jnp.concatenate([cvt, jnp.ones((dec_batch, N_KV_HEADS, ONES_ROWS, past), _BF16)], axis=2)
        k_all = jnp.concatenate([ck, k], axis=1)
        vt_all = jnp.concatenate([cvt.reshape(dec_batch, VT_ROWS, past), vt], axis=2)
        q_norm = HEAD_DIM ** 0.5 * jnp.max(jnp.abs(q_norm_g[l])) * Q_SCALE
        k_norm = HEAD_DIM ** 0.5 * jnp.max(jnp.abs(k_norm_g[l]))
        ck_sq = jnp.square(ck.astype(_F32)).reshape(dec_batch, past, N_KV_HEADS, HEAD_DIM)
        k_norm = jnp.maximum(k_norm, jnp.sqrt(jnp.max(jnp.sum(ck_sq, axis=-1))))
        attn = _attention(q, k_all, vt_all, dec_seq, past + dec_seq, score_bound=BOUND_MARGIN * q_norm * k_norm)
        conv, pool, sgu = _mix(ap, u, vn, lw, TOK_TILE)
        xs = _merge(xs, h, (attn, conv, pool, sgu), mod_lat, lw)
        xs = _mlp(xs, mod_lat, lw, final_g, last)

    y_prompt = xp.reshape(batch, seq, D_MODEL)
    return (y_prompt, xs, jnp.stack(new_k, axis=1), jnp.stack(new_v, axis=1))
```

```python
import functools

import jax
import jax.numpy as jnp
from jax import lax
from jax.experimental import pallas as pl
from jax.experimental.pallas import tpu as pltpu

D_MODEL = 1024
DEPTH = 2
GRID_W = 64
N_HEADS = 8
N_KV_HEADS = 2
HEAD_DIM = 64
Q_W = N_HEADS * HEAD_DIM
KV_W = N_KV_HEADS * HEAD_DIM
GROUP = N_HEADS // N_KV_HEADS
AXIS_DIM = HEAD_DIM // 2
ROPE_THETA = 10000.0
CONV_W = 512
CONV_K = 31
POOL_W = 512
POOL_GROUPS = 4
POOL_GC = POOL_W // POOL_GROUPS
POOL_WINDOWS = (2, 4, 8, 16)
SGU_W = 512
SGU_GROUPS = 4
SGU_GC = SGU_W // SGU_GROUPS
SGU_CHUNK = 128
BRANCH_W = 512
N_BRANCH = 4
D_FF = 4 * D_MODEL
IN_W = Q_W + 2 * KV_W + 2 * CONV_W + POOL_W + 2 * SGU_W
EPS = 1e-6

O_Q = 0
O_KV = Q_W
O_A = Q_W + 2 * KV_W
O_P = O_A + 2 * CONV_W
O_S = O_P + POOL_W

LANES = 128
SUBLANES = 8
CONV_ROWS = 64
CONV_LANES = 512
HALO = 16
TOK_TILE = 512
Q_TILE = 512
KEY_BLOCK = 768
ONES_ROWS = 16
VT_HEAD_ROWS = HEAD_DIM + ONES_ROWS
VT_ROWS = N_KV_HEADS * VT_HEAD_ROWS
MOD_ROWS = 16
MOD_TILE = 1536
VMEM_LIMIT = 56 * 1024 * 1024
Q_SCALE = HEAD_DIM ** -0.5 * 1.4426950408889634
BOUND_MARGIN = 1.02
MAX_SCORE_BOUND = 40.0

HEAD_ORDER = (0, 4, 1, 5, 2, 6, 3, 7)

_F32 = jnp.float32
_BF16 = jnp.bfloat16


def _dot(a, b):
    return jnp.dot(a, b, preferred_element_type=_F32)


def _cparams(sem, flags=None):
    return pltpu.CompilerParams(dimension_semantics=sem, vmem_limit_bytes=VMEM_LIMIT, flags=flags)


def _mod_kernel(c_ref, w_ref, b_ref, o_ref):
    c = c_ref[...]
    cs = c * jax.nn.sigmoid(c)
    o_ref[0] = _dot(cs.astype(_BF16), w_ref[0].astype(_BF16)) + b_ref[0]


def _modulation(c_all, w_mod, b_mod):
    n_col = 6 * D_MODEL
    return pl.pallas_call(
        _mod_kernel,
        grid=(DEPTH, n_col // MOD_TILE),
        in_specs=[
            pl.BlockSpec((MOD_ROWS, D_MODEL), lambda l, j: (0, 0)),
            pl.BlockSpec((1, D_MODEL, MOD_TILE), lambda l, j: (l, 0, j)),
            pl.BlockSpec((1, 1, MOD_TILE), lambda l, j: (l, 0, j)),
        ],
        out_specs=pl.BlockSpec((1, MOD_ROWS, MOD_TILE), lambda l, j: (l, 0, j)),
        out_shape=jax.ShapeDtypeStruct((DEPTH, MOD_ROWS, n_col), _F32),
        compiler_params=_cparams(("parallel", "parallel")),
        name="modulation",
    )(c_all, w_mod, b_mod.reshape(DEPTH, 1, n_col))


def _rope(x, cos_ref, se_ref, so_ref):
    cos, se, so = cos_ref[...], se_ref[...], so_ref[...]
    outs = []
    for c in range(x.shape[1] // LANES):
        xc = x[:, c * LANES:(c + 1) * LANES]
        nxt = pltpu.roll(xc, LANES - 1, 1)
        prv = pltpu.roll(xc, 1, 1)
        outs.append(xc * cos + nxt * se + prv * so)
    return outs[0] if len(outs) == 1 else jnp.concatenate(outs, axis=1)


def _pre_kernel(*refs, rope, emit_kv):
    x_ref, mod_ref, ng_ref, win_ref, qg_ref, kg_ref, indq_ref, indk_ref, sg_ref = refs[:9]
    refs = refs[9:]
    if rope:
        cos_ref, se_ref, so_ref = refs[:3]
        refs = refs[3:]
    q_ref, k_ref, vt_ref, ap_ref, u_ref, vn_ref, h_ref = refs[:7]
    if emit_kv:
        kf_ref, vf_ref = refs[7:9]

    x = x_ref[0]
    xn = x * lax.rsqrt(jnp.mean(x * x, axis=-1, keepdims=True) + EPS) * ng_ref[...]
    h = xn * (1.0 + mod_ref[0, 1:2, :]) + mod_ref[0, 0:1, :]
    hb = h.astype(_BF16)
    h_ref[0] = hb

    zq = _dot(hb, win_ref[:, O_Q:O_Q + Q_W])
    msq = _dot((zq * zq).astype(_BF16), indq_ref[...])
    qn = zq * lax.rsqrt(msq + EPS) * qg_ref[...]
    if rope:
        qn = _rope(qn, cos_ref, se_ref, so_ref)
    q_ref[0] = (qn * Q_SCALE).astype(_BF16)

    zkv = _dot(hb, win_ref[:, O_KV:O_KV + 2 * KV_W])
    zk = zkv[:, :KV_W]
    zv = zkv[:, KV_W:]
    msk = _dot((zk * zk).astype(_BF16), indk_ref[...])
    kn = zk * lax.rsqrt(msk + EPS) * kg_ref[...]
    if emit_kv:
        kf_ref[0] = kn
        vf_ref[0] = zv
    if rope:
        kn = _rope(kn, cos_ref, se_ref, so_ref)
    k_ref[0] = kn.astype(_BF16)
    vt = zv.T.astype(_BF16)
    for j in range(N_KV_HEADS):
        vt_ref[0, j * VT_HEAD_ROWS:j * VT_HEAD_ROWS + HEAD_DIM, :] = vt[j * HEAD_DIM:(j + 1) * HEAD_DIM, :]
        vt_ref[0, j * VT_HEAD_ROWS + HEAD_DIM:(j + 1) * VT_HEAD_ROWS, :] = jnp.ones((ONES_ROWS, vt.shape[1]), _BF16)

    za = _dot(hb, win_ref[:, O_A:O_A + 2 * CONV_W])
    ap_ref[0, :, 0:CONV_W] = za[:, :CONV_W] * jax.nn.sigmoid(za[:, CONV_W:])
    ap_ref[0, :, CONV_W:CONV_W + POOL_W] = _dot(hb, win_ref[:, O_P:O_P + POOL_W])

    zs = jax.nn.gelu(_dot(hb, win_ref[:, O_S:O_S + 2 * SGU_W]))
    u_ref[0] = zs[:, :SGU_W].astype(_BF16)
    v = zs[:, SGU_W:]
    vn = v * lax.rsqrt(jnp.mean(v * v, axis=-1, keepdims=True) + EPS) * sg_ref[...]
    vn_ref[0] = vn.astype(_BF16)


def _pre(x, mod, lw, rope_tabs, emit_kv):
    bsz, n, _ = x.shape
    t = TOK_TILE
    rope = rope_tabs is not None
    row = lambda b, i: (b, i, 0)
    const2 = lambda b, i: (0, 0)
    in_specs = [
        pl.BlockSpec((1, t, D_MODEL), row),
        pl.BlockSpec((1, 6, D_MODEL), lambda b, i: (b, 0, 0)),
        pl.BlockSpec((1, D_MODEL), const2),
        pl.BlockSpec((D_MODEL, IN_W), const2),
        pl.BlockSpec((1, Q_W), const2),
        pl.BlockSpec((1, KV_W), const2),
        pl.BlockSpec((Q_W, Q_W), const2),
        pl.BlockSpec((KV_W, KV_W), const2),
        pl.BlockSpec((1, SGU_W), const2),
    ]
    args = [x, mod, lw['norm1_g'], lw['w_in'], lw['q_g'], lw['k_g'], lw['ind_q'], lw['ind_k'], lw['sgu_norm_g']]
    if rope:
        in_specs += [pl.BlockSpec((t, LANES), lambda b, i: (i, 0))] * 3
        args += list(rope_tabs)
    out_specs = [
        pl.BlockSpec((1, t, Q_W), row),
        pl.BlockSpec((1, t, KV_W), row),
        pl.BlockSpec((1, VT_ROWS, t), lambda b, i: (b, 0, i)),
        pl.BlockSpec((1, t, CONV_W + POOL_W), row),
        pl.BlockSpec((1, t, SGU_W), row),
        pl.BlockSpec((1, t, SGU_W), row),
        pl.BlockSpec((1, t, D_MODEL), row),
    ]
    out_shape = [
        jax.ShapeDtypeStruct((bsz, n, Q_W), _BF16),
        jax.ShapeDtypeStruct((bsz, n, KV_W), _BF16),
        jax.ShapeDtypeStruct((bsz, VT_ROWS, n), _BF16),
        jax.ShapeDtypeStruct((bsz, n, CONV_W + POOL_W), _F32),
        jax.ShapeDtypeStruct((bsz, n, SGU_W), _BF16),
        jax.ShapeDtypeStruct((bsz, n, SGU_W), _BF16),
        jax.ShapeDtypeStruct((bsz, n, D_MODEL), _BF16),
    ]
    if emit_kv:
        out_specs += [pl.BlockSpec((1, t, KV_W), row)] * 2
        out_shape += [jax.ShapeDtypeStruct((bsz, n, KV_W), _F32)] * 2
    return pl.pallas_call(
        functools.partial(_pre_kernel, rope=rope, emit_kv=emit_kv),
        grid=(bsz, n // t),
        in_specs=in_specs,
        out_specs=out_specs,
        out_shape=out_shape,
        compiler_params=_cparams(("parallel", "parallel")),
        name="pre_lat" if rope else "pre_ctx",
    )(*args)


def _split_heads(q_ref, qh_ref):
    tq = q_ref.shape[1]
    hi_half = lax.broadcasted_iota(jnp.int32, (1, LANES), 1) >= HEAD_DIM
    for g in range(GROUP):
        qc = q_ref[0, :, g * LANES:(g + 1) * LANES]
        for j in range(N_KV_HEADS):
            qh_ref[j, g * tq:(g + 1) * tq, :] = jnp.where(hi_half if j == 1 else jnp.logical_not(hi_half), qc,
                                                          jnp.zeros_like(qc))


def _write_heads(acc_ref, o_ref):
    tq = o_ref.shape[1]
    for g in range(GROUP):
        halves = []
        for j in range(N_KV_HEADS):
            acc = acc_ref[j, :, g * tq:(g + 1) * tq]
            halves.append(acc[0:HEAD_DIM, :] / acc[HEAD_DIM:HEAD_DIM + 1, :])
        o_ref[0, :, g * LANES:(g + 1) * LANES] = jnp.concatenate(halves, axis=0).T.astype(_BF16)


_NT = (((1,), (1,)), ((), ()))


def _attn_kernel(q_ref, k_ref, vt_ref, o_ref, qh_ref, m_ref, alpha_ref, acc_ref, s_ref, *, n_keys, key_block):
    _split_heads(q_ref, qh_ref)
    m_ref[...] = jnp.full(m_ref.shape, -1e30, _F32)
    acc_ref[...] = jnp.zeros(acc_ref.shape, _F32)

    def scores(off, j):
        s = lax.dot_general(k_ref[0, pl.ds(off, key_block), :], qh_ref[j], _NT, preferred_element_type=_F32)
        m_old = m_ref[j]
        m_new = jnp.maximum(m_old, jnp.max(s, axis=0, keepdims=True))
        m_ref[j] = m_new
        alpha_ref[j] = jnp.exp2(m_old - m_new)
        s_ref[j] = s

    def values(off, j):
        p = jnp.exp2(s_ref[j] - m_ref[j]).astype(_BF16)
        vblk = vt_ref[0, j * VT_HEAD_ROWS:(j + 1) * VT_HEAD_ROWS, pl.ds(off, key_block)]
        acc_ref[j] = alpha_ref[j] * acc_ref[j] + _dot(vblk, p)

    n_blocks = n_keys // key_block
    unroll = 2 if n_blocks % 2 == 0 else 1
    scores(0, 0)

    def body(b, carry):
        for r in range(unroll):
            off = pl.multiple_of((b * unroll + r) * key_block, key_block)
            scores(off, 1)
            values(off, 0)
            scores(off + key_block, 0)
            values(off, 1)
        return carry

    lax.fori_loop(0, n_blocks // unroll - 1, body, 0)
    for r in range(unroll):
        off = (n_blocks - unroll + r) * key_block
        scores(off, 1)
        values(off, 0)
        if r < unroll - 1:
            scores(off + key_block, 0)
        values(off, 1)
    _write_heads(acc_ref, o_ref)


def _attn_bounded_kernel(bound_ref, q_ref, k_ref, vt_ref, o_ref, qh_ref, acc_ref, *, n_keys, key_block):
    _split_heads(q_ref, qh_ref)
    acc_ref[...] = jnp.zeros(acc_ref.shape, _F32)
    shift = bound_ref[0, 0]

    def unit(off, j):
        s = lax.dot_general(k_ref[0, pl.ds(off, key_block), :], qh_ref[j], _NT, preferred_element_type=_F32)
        p = jnp.exp2(s - shift).astype(_BF16)
        vblk = vt_ref[0, j * VT_HEAD_ROWS:(j + 1) * VT_HEAD_ROWS, pl.ds(off, key_block)]
        acc_ref[j] = acc_ref[j] + _dot(vblk, p)

    n_blocks = n_keys // key_block
    unroll = 2 if n_blocks % 2 == 0 else 1

    def body(b, carry):
        for r in range(unroll):
            off = pl.multiple_of((b * unroll + r) * key_block, key_block)
            for j in range(N_KV_HEADS):
                unit(off, j)
        return carry

    lax.fori_loop(0, n_blocks // unroll, body, 0)
    _write_heads(acc_ref, o_ref)


def _attention(q, k, vt, q_run, n_keys, score_bound=None):
    bsz, n, _ = q.shape
    tq = min(Q_TILE, q_run)
    key_block = min(KEY_BLOCK, n_keys)
    per_run = q_run // tq
    qmap = lambda b, i: (b, i, 0)
    in_specs = [
        pl.BlockSpec((1, tq, Q_W), qmap),
        pl.BlockSpec((1, n_keys, KV_W), lambda b, i: (b, i // per_run, 0)),
        pl.BlockSpec((1, VT_ROWS, n_keys), lambda b, i: (b, 0, i // per_run)),
    ]
    out_spec = pl.BlockSpec((1, tq, Q_W), qmap)
    out_shape = jax.ShapeDtypeStruct((bsz, n, Q_W), _BF16)
    qh_scratch = pltpu.VMEM((N_KV_HEADS, GROUP * tq, LANES), _BF16)
    acc_scratch = pltpu.VMEM((N_KV_HEADS, VT_HEAD_ROWS, GROUP * tq), _F32)
    row_scratch = pltpu.VMEM((N_KV_HEADS, 1, GROUP * tq), _F32)

    def online(q, k, vt):
        return pl.pallas_call(
            functools.partial(_attn_kernel, n_keys=n_keys, key_block=key_block),
            grid=(bsz, n // tq),
            in_specs=in_specs,
            out_specs=out_spec,
            out_shape=out_shape,
            scratch_shapes=[qh_scratch, row_scratch, row_scratch, acc_scratch,
                            pltpu.VMEM((N_KV_HEADS, key_block, GROUP * tq), _F32)],
            compiler_params=_cparams(("parallel", "arbitrary")),
            name="attn_online",
        )(q, k, vt)

    if score_bound is None:
        return online(q, k, vt)

    def bounded(q, k, vt):
        return pl.pallas_call(
            functools.partial(_attn_bounded_kernel, n_keys=n_keys, key_block=key_block),
            grid=(bsz, n // tq),
            in_specs=[pl.BlockSpec(memory_space=pltpu.SMEM)] + in_specs,
            out_specs=out_spec,
            out_shape=out_shape,
            scratch_shapes=[qh_scratch, acc_scratch],
            compiler_params=_cparams(("parallel", "arbitrary")),
            name="attn_bounded",
        )(score_bound.reshape(1, 1), q, k, vt)

    return lax.cond(score_bound < MAX_SCORE_BOUND, bounded, online, q, k, vt)


def _mix_kernel(ap_ref, prev_ref, next_ref, u_ref, vn_ref, cw_ref, cb_ref, lg_ref, lb_ref, pw_ref, ps_ref,
                sw_ref, sb_ref, conv_ref, pool_ref, sgu_ref, buf_ref, xs_ref, cacc_ref, tmp_ref, *, seq_len):
    t = ap_ref.shape[1]
    i = pl.program_id(1)
    last = pl.num_programs(1) - 1
    buf_ref[0:HALO, :] = jnp.where(i > 0, prev_ref[0], 0.0)
    buf_ref[HALO:HALO + t, :] = ap_ref[0]
    buf_ref[HALO + t:HALO + t + HALO, :] = jnp.where(i < last, next_ref[0], 0.0)
    buf_ref[t + 2 * HALO:t + 2 * HALO + SUBLANES, :] = jnp.zeros((SUBLANES, buf_ref.shape[1]), _F32)

    n_sh = t + 2 * HALO - SUBLANES
    for r in range(1, SUBLANES):
        xs_ref[r - 1] = buf_ref[r:r + n_sh, 0:CONV_W]

    def conv_rows(ci, carry):
        base = pl.multiple_of(ci * CONV_ROWS, CONV_ROWS)
        for c0 in range(0, CONV_W, CONV_LANES):
            cols = slice(c0, c0 + CONV_LANES)
            acc = jnp.broadcast_to(cb_ref[:, cols], (CONV_ROWS, CONV_LANES))
            for k in range(CONV_K):
                q8, r = divmod(HALO - CONV_K // 2 + k, SUBLANES)
                rows = pl.ds(base + q8 * SUBLANES, CONV_ROWS)
                x = buf_ref[rows, cols] if r == 0 else xs_ref[r - 1, rows, cols]
                acc = acc + x * cw_ref[k:k + 1, cols]
            cacc_ref[pl.ds(base, CONV_ROWS), cols] = acc
        return carry

    lax.fori_loop(0, t // CONV_ROWS, conv_rows, 0)
    acc = cacc_ref[...]
    mu = jnp.mean(acc, axis=-1, keepdims=True)
    cen = acc - mu
    var = jnp.mean(cen * cen, axis=-1, keepdims=True)
    y = cen * lax.rsqrt(var + EPS) * lg_ref[...] + lb_ref[...]
    conv_ref[0] = (y * jax.nn.sigmoid(y)).astype(_BF16)

    pos = i * t + lax.broadcasted_iota(jnp.int32, (t, 1), 0)
    for g, w in enumerate(POOL_WINDOWS):
        cols = slice(CONV_W + g * POOL_GC, CONV_W + (g + 1) * POOL_GC)
        first = HALO - w // 2
        if w == 2:
            s = buf_ref[first:first + t, cols] + buf_ref[first + 1:first + 1 + t, cols]
        else:
            n = t + 2 * HALO
            tmp_ref[0, 0:n, :] = buf_ref[0:n, cols] + buf_ref[1:n + 1, cols]
            span, stage = 2, 0
            while span * 2 < w:
                n -= SUBLANES
                tmp_ref[stage + 1, 0:n, :] = tmp_ref[stage, 0:n, :] + tmp_ref[stage, span:span + n, :]
                span, stage = span * 2, stage + 1
            s = tmp_ref[stage, first:first + t, :] + tmp_ref[stage, first + span:first + span + t, :]
        lo = jnp.maximum(pos - w // 2, 0)
        hi = jnp.minimum(pos - w // 2 + w, seq_len)
        pooled = s / (hi - lo).astype(_F32) - buf_ref[HALO:HALO + t, cols]
        yg = _dot(pooled.astype(_BF16), pw_ref[g])
        pool_ref[0, :, g * POOL_GC:(g + 1) * POOL_GC] = (yg * ps_ref[:, g * POOL_GC:(g + 1) * POOL_GC]).astype(_BF16)

    n_chunk = t // SGU_CHUNK
    for g in range(SGU_GROUPS):
        cols = slice(g * SGU_GC, (g + 1) * SGU_GC)
        rhs = jnp.concatenate([vn_ref[0, c * SGU_CHUNK:(c + 1) * SGU_CHUNK, cols] for c in range(n_chunk)], axis=1)
        sv = _dot(sw_ref[g], rhs)
        for c in range(n_chunk):
            rows = slice(c * SGU_CHUNK, (c + 1) * SGU_CHUNK)
            gate = sv[:, c * SGU_GC:(c + 1) * SGU_GC] + sb_ref[:, cols]
            sgu_ref[0, rows, cols] = (u_ref[0, rows, cols].astype(_F32) * gate).astype(_BF16)


def _mix(ap, u, vn, lw, t):
    bsz, n, _ = ap.shape
    hb = t // HALO
    n_hb = n // HALO
    row = lambda b, i: (b, i, 0)
    const2 = lambda b, i: (0, 0)
    const3 = lambda b, i: (0, 0, 0)
    w_all = CONV_W + POOL_W
    in_specs = [
        pl.BlockSpec((1, t, w_all), row),
        pl.BlockSpec((1, HALO, w_all), lambda b, i: (b, jnp.maximum(i * hb - 1, 0), 0)),
        pl.BlockSpec((1, HALO, w_all), lambda b, i: (b, jnp.minimum((i + 1) * hb, n_hb - 1), 0)),
        pl.BlockSpec((1, t, SGU_W), row),
        pl.BlockSpec((1, t, SGU_W), row),
        pl.BlockSpec((CONV_K, CONV_W), const2),
        pl.BlockSpec((1, CONV_W), const2),
        pl.BlockSpec((1, CONV_W), const2),
        pl.BlockSpec((1, CONV_W), const2),
        pl.BlockSpec((POOL_GROUPS, POOL_GC, POOL_GC), const3),
        pl.BlockSpec((1, POOL_W), const2),
        pl.BlockSpec((SGU_GROUPS, SGU_CHUNK, SGU_CHUNK), const3),
        pl.BlockSpec((SGU_CHUNK, SGU_W), const2),
    ]
    o_spec = pl.BlockSpec((1, t, BRANCH_W), row)
    o_shape = jax.ShapeDtypeStruct((bsz, n, BRANCH_W), _BF16)
    return pl.pallas_call(
        functools.partial(_mix_kernel, seq_len=n),
        grid=(bsz, n // t),
        in_specs=in_specs,
        out_specs=[o_spec] * 3,
        out_shape=[o_shape] * 3,
        scratch_shapes=[
            pltpu.VMEM((t + 2 * HALO + SUBLANES, w_all), _F32),
            pltpu.VMEM((SUBLANES - 1, t + 2 * HALO - SUBLANES, CONV_W), _F32),
            pltpu.VMEM((t, CONV_W), _F32),
            pltpu.VMEM((3, t + 2 * HALO, POOL_GC), _F32),
        ],
        compiler_params=_cparams(("parallel", "parallel")),
        name="mix",
    )(ap, ap, ap, u, vn, lw['conv_w'], lw['conv_b'], lw['conv_ln_g'], lw['conv_ln_b'], lw['pool_w'],
      lw['pool_scale'], lw['sgu_w'], lw['sgu_bias'])


def _merge_kernel(x_ref, h_ref, b0_ref, b1_ref, b2_ref, b3_ref, mod_ref, wg_ref, bg_ref, wb_ref, wo_ref, o_ref):
    hb = h_ref[0]
    merged = None
    for idx, br_ref in enumerate((b0_ref, b1_ref, b2_ref, b3_ref)):
        cols = slice(idx * D_MODEL, (idx + 1) * D_MODEL)
        gate = jax.nn.sigmoid(_dot(hb, wg_ref[:, cols]) + bg_ref[:, cols])
        term = gate * _dot(br_ref[0], wb_ref[idx])
        merged = term if merged is None else merged + term
    m = _dot(merged.astype(_BF16), wo_ref[...])
    o_ref[0] = x_ref[0] + mod_ref[0, 2:3, :] * m


def _merge(x, h, branches, mod, lw):
    bsz, n, _ = x.shape
    t = TOK_TILE
    row = lambda b, i: (b, i, 0)
    const2 = lambda b, i: (0, 0)
    in_specs = [
        pl.BlockSpec((1, t, D_MODEL), row),
        pl.BlockSpec((1, t, D_MODEL), row),
    ] + [pl.BlockSpec((1, t, BRANCH_W), row)] * N_BRANCH + [
        pl.BlockSpec((1, 6, D_MODEL), lambda b, i: (b, 0, 0)),
        pl.BlockSpec((D_MODEL, N_BRANCH * D_MODEL), const2),
        pl.BlockSpec((1, N_BRANCH * D_MODEL), const2),
        pl.BlockSpec((N_BRANCH, BRANCH_W, D_MODEL), lambda b, i: (0, 0, 0)),
        pl.BlockSpec((D_MODEL, D_MODEL), const2),
    ]
    return pl.pallas_call(
        _merge_kernel,
        grid=(bsz, n // t),
        in_specs=in_specs,
        out_specs=pl.BlockSpec((1, t, D_MODEL), row),
        out_shape=jax.ShapeDtypeStruct((bsz, n, D_MODEL), _F32),
        compiler_params=_cparams(("parallel", "parallel")),
        name="merge",
    )(x, h, *branches, mod, lw['w_gate'], lw['b_gate'], lw['w_branch'], lw['w_out'])


def _mlp_kernel(x_ref, mod_ref, ng_ref, w1_ref, w2_ref, fg_ref, o_ref, hid_ref, *, final_norm):
    x = x_ref[0]
    xn = x * lax.rsqrt(jnp.mean(x * x, axis=-1, keepdims=True) + EPS) * ng_ref[...]
    hb = (xn * (1.0 + mod_ref[0, 4:5, :]) + mod_ref[0, 3:4, :]).astype(_BF16)
    for c in range(D_FF // D_MODEL):
        cols = slice(c * D_MODEL, (c + 1) * D_MODEL)
        a = jnp.maximum(_dot(hb, w1_ref[:, cols]), 0.0)
        hid_ref[:, cols] = (a * a).astype(_BF16)
    y = x + mod_ref[0, 5:6, :] * _dot(hid_ref[...], w2_ref[...])
    if final_norm:
        y = y * lax.rsqrt(jnp.mean(y * y, axis=-1, keepdims=True) + EPS) * fg_ref[...]
    o_ref[0] = y


def _mlp(x, mod, lw, final_g, final_norm):
    bsz, n, _ = x.shape
    t = TOK_TILE
    row = lambda b, i: (b, i, 0)
    const2 = lambda b, i: (0, 0)
    return pl.pallas_call(
        functools.partial(_mlp_kernel, final_norm=final_norm),
        grid=(bsz, n // t),
        in_specs=[
            pl.BlockSpec((1, t, D_MODEL), row),
            pl.BlockSpec((1, 6, D_MODEL), lambda b, i: (b, 0, 0)),
            pl.BlockSpec((1, D_MODEL), const2),
            pl.BlockSpec((D_MODEL, D_FF), const2),
            pl.BlockSpec((D_FF, D_MODEL), const2),
            pl.BlockSpec((1, D_MODEL), const2),
        ],
        out_specs=pl.BlockSpec((1, t, D_MODEL), row),
        out_shape=jax.ShapeDtypeStruct((bsz, n, D_MODEL), _F32),
        scratch_shapes=[pltpu.VMEM((t, D_FF), _BF16)],
        compiler_params=_cparams(("parallel", "parallel")),
        name="mlp",
    )(x, mod, lw['norm2_g'], lw['w_mlp_in'], lw['w_mlp_out'], final_g)


def _rope_tables(n):
    rows = n // GRID_W
    row = jnp.repeat(jnp.arange(rows, dtype=_F32), GRID_W)
    col = jnp.tile(jnp.arange(GRID_W, dtype=_F32), rows)
    inv = ROPE_THETA ** (-jnp.arange(0, AXIS_DIM, 2, dtype=_F32) / AXIS_DIM)
    ang = jnp.concatenate([row[:, None] * inv, col[:, None] * inv], axis=-1)
    cos = jnp.repeat(jnp.cos(ang), 2, axis=-1)
    sin = jnp.repeat(jnp.sin(ang), 2, axis=-1)
    even = (jnp.arange(HEAD_DIM) % 2 == 0)[None, :]
    se = jnp.where(even, -sin, 0.0)
    so = jnp.where(even, 0.0, sin)
    rep = LANES // HEAD_DIM
    return tuple(jnp.tile(tab, (1, rep)) for tab in (cos, se, so))


def _head_indicator(width):
    head = jnp.arange(width) // HEAD_DIM
    return ((head[:, None] == head[None, :]).astype(_F32) / HEAD_DIM).astype(_BF16)


def _layer_weights(l, w_in, norm1_g, q_norm_g, k_norm_g, conv_w, conv_b, conv_ln_g, conv_ln_b, pool_w,
                   pool_scale, sgu_norm_g, sgu_w, sgu_b, w_branch, w_gate, b_gate, w_out, norm2_g,
                   w_mlp_in, w_mlp_out):
    order = jnp.array(HEAD_ORDER)
    q_cols = (order[:, None] * HEAD_DIM + jnp.arange(HEAD_DIM)[None, :]).reshape(-1)
    win = w_in[l]
    win = jnp.concatenate([win[:, q_cols], win[:, Q_W:]], axis=1).astype(_BF16)
    wb = w_branch[l]
    wb = jnp.concatenate([wb[0][q_cols][None], wb[1:]], axis=0).astype(_BF16)
    row = lambda v: v.reshape(1, -1)
    return {
        'norm1_g': row(norm1_g[l]),
        'w_in': win,
        'q_g': row(jnp.tile(q_norm_g[l], N_HEADS)),
        'k_g': row(jnp.tile(k_norm_g[l], N_KV_HEADS)),
        'ind_q': _head_indicator(Q_W),
        'ind_k': _head_indicator(KV_W),
        'sgu_norm_g': row(sgu_norm_g[l]),
        'conv_w': conv_w[l],
        'conv_b': row(conv_b[l]),
        'conv_ln_g': row(conv_ln_g[l]),
        'conv_ln_b': row(conv_ln_b[l]),
        'pool_w': pool_w[l].astype(_BF16),
        'pool_scale': row(pool_scale[l]),
        'sgu_w': sgu_w[l].astype(_BF16),
        'sgu_bias': jnp.repeat(sgu_b[l].T, SGU_GC, axis=1),
        'w_gate': w_gate[l].astype(_BF16),
        'b_gate': row(b_gate[l]),
        'w_branch': wb,
        'w_out': w_out[l].astype(_BF16),
        'norm2_g': row(norm2_g[l]),
        'w_mlp_in': w_mlp_in[l].astype(_BF16),
        'w_mlp_out': w_mlp_out[l].astype(_BF16),
    }


def kernel(x_prompt, x_sample, cache_k, cache_v, c, c_ctx, w_mod, b_mod, norm1_g, w_in, q_norm_g, k_norm_g, conv_w, conv_b, conv_ln_g, conv_ln_b, pool_w, pool_scale, sgu_norm_g, sgu_w, sgu_b, w_branch, w_gate, b_gate, w_out, norm2_g, w_mlp_in, w_mlp_out, final_norm_g):
    batch, seq, _ = x_prompt.shape
    dec_batch, dec_seq, _ = x_sample.shape
    past = cache_k.shape[2]
    n_ctx = batch * seq

    c_all = jnp.concatenate([c, c_ctx[None, :], jnp.zeros((MOD_ROWS - dec_batch - 1, D_MODEL), _F32)], axis=0)
    mod = _modulation(c_all, w_mod, b_mod).reshape(DEPTH, MOD_ROWS, 6, D_MODEL)

    rope_tabs = _rope_tables(dec_seq)
    final_g = final_norm_g.reshape(1, D_MODEL)

    xp = x_prompt.reshape(1, n_ctx, D_MODEL)
    xs = x_sample
    new_k, new_v = [], []
    for l in range(DEPTH):
        lw = _layer_weights(l, w_in, norm1_g, q_norm_g, k_norm_g, conv_w, conv_b, conv_ln_g, conv_ln_b, pool_w,
                            pool_scale, sgu_norm_g, sgu_w, sgu_b, w_branch, w_gate, b_gate, w_out, norm2_g,
                            w_mlp_in, w_mlp_out)
        mod_lat = mod[l, :dec_batch]
        mod_ctx = mod[l, dec_batch:dec_batch + 1]
        last = l == DEPTH - 1

        q, k, vt, ap, u, vn, h, kf, vf = _pre(xp, mod_ctx, lw, None, True)
        new_k.append(kf.reshape(batch, seq, N_KV_HEADS, HEAD_DIM))
        new_v.append(vf.reshape(batch, seq, N_KV_HEADS, HEAD_DIM))
        attn = _attention(q, k, vt, seq, seq)
        conv, pool, sgu = _mix(ap.reshape(batch, seq, -1), u.reshape(batch, seq, -1), vn.reshape(batch, seq, -1),
                               lw, seq)
        branches = (attn, conv.reshape(1, n_ctx, -1), pool.reshape(1, n_ctx, -1), sgu.reshape(1, n_ctx, -1))
        xp = _merge(xp, h, branches, mod_ctx, lw)
        xp = _mlp(xp, mod_ctx, lw, final_g, last)

        q, k, vt, ap, u, vn, h = _pre(xs, mod_lat, lw, rope_tabs, False)
        ck = cache_k[:, l].reshape(dec_batch, past, KV_W).astype(_BF16)
        cvt = jnp.transpose(cache_v[:, l], (0, 2, 3, 1)).astype(_BF16)
        cvt = jnp.concatenate([cvt, jnp.ones((dec_batch, N_KV_HEADS, ONES_ROWS, past), _BF16)], axis=2)
        k_all = jnp.concatenate([ck, k], axis=1)
        vt_all = jnp.concatenate([cvt.reshape(dec_batch, VT_ROWS, past), vt], axis=2)
        q_norm = HEAD_DIM ** 0.5 * jnp.max(jnp.abs(q_norm_g[l])) * Q_SCALE
        k_norm = HEAD_DIM ** 0.5 * jnp.max(jnp.abs(k_norm_g[l]))
        ck_sq = jnp.square(ck.astype(_F32)).reshape(dec_batch, past, N_KV_HEADS, HEAD_DIM)
        k_norm = jnp.maximum(k_norm, jnp.sqrt(jnp.max(jnp.sum(ck_sq, axis=-1))))
        attn = _attention(q, k_all, vt_all, dec_seq, past + dec_seq, score_bound=BOUND_MARGIN * q_norm * k_norm)
        conv, pool, sgu = _mix(ap, u, vn, lw, TOK_TILE)
        xs = _merge(xs, h, (attn, conv, pool, sgu), mod_lat, lw)
        xs = _mlp(xs, mod_lat, lw, final_g, last)

    y_prompt = xp.reshape(batch, seq, D_MODEL)
    return (y_prompt, xs, jnp.stack(new_k, axis=1), jnp.stack(new_v, axis=1))
```

```python
import functools

import jax
import jax.numpy as jnp
from jax import lax
from jax.experimental import pallas as pl
from jax.experimental.pallas import tpu as pltpu

D_MODEL = 1024
DEPTH = 2
GRID_W = 64
N_HEADS = 8
N_KV_HEADS = 2
HEAD_DIM = 64
Q_W = N_HEADS * HEAD_DIM
KV_W = N_KV_HEADS * HEAD_DIM
GROUP = N_HEADS // N_KV_HEADS
AXIS_DIM = HEAD_DIM // 2
ROPE_THETA = 10000.0
CONV_W = 512
CONV_K = 31
POOL_W = 512
POOL_GROUPS = 4
POOL_GC = POOL_W // POOL_GROUPS
POOL_WINDOWS = (2, 4, 8, 16)
SGU_W = 512
SGU_GROUPS = 4
SGU_GC = SGU_W // SGU_GROUPS
SGU_CHUNK = 128
BRANCH_W = 512
N_BRANCH = 4
D_FF = 4 * D_MODEL
IN_W = Q_W + 2 * KV_W + 2 * CONV_W + POOL_W + 2 * SGU_W
EPS = 1e-6

O_Q = 0
O_KV = Q_W
O_A = Q_W + 2 * KV_W
O_P = O_A + 2 * CONV_W
O_S = O_P + POOL_W

LANES = 128
SUBLANES = 8
CONV_ROWS = 64
CONV_LANES = 512
HALO = 16
TOK_TILE = 512
Q_TILE = 1024
Q_TILE_ONLINE = 512
KEY_BLOCK = 768
KEY_BLOCK_BOUNDED = 1024
ONES_ROWS = 16
VT_HEAD_ROWS = HEAD_DIM + ONES_ROWS
VT_ROWS = N_KV_HEADS * VT_HEAD_ROWS
MOD_ROWS = 16
MOD_TILE = 1536
VMEM_LIMIT = 56 * 1024 * 1024
Q_SCALE = HEAD_DIM ** -0.5 * 1.4426950408889634
BOUND_MARGIN = 1.02
MAX_SCORE_BOUND = 40.0

_F32 = jnp.float32
_BF16 = jnp.bfloat16


def _dot(a, b):
    return jnp.dot(a, b, preferred_element_type=_F32)


def _cparams(sem, flags=None):
    return pltpu.CompilerParams(dimension_semantics=sem, vmem_limit_bytes=VMEM_LIMIT, flags=flags)


def _mod_kernel(c_ref, w_ref, b_ref, o_ref):
    c = c_ref[...]
    cs = c * jax.nn.sigmoid(c)
    o_ref[0] = _dot(cs.astype(_BF16), w_ref[0].astype(_BF16)) + b_ref[0]


def _modulation(c_all, w_mod, b_mod):
    n_col = 6 * D_MODEL
    return pl.pallas_call(
        _mod_kernel,
        grid=(DEPTH, n_col // MOD_TILE),
        in_specs=[
            pl.BlockSpec((MOD_ROWS, D_MODEL), lambda l, j: (0, 0)),
            pl.BlockSpec((1, D_MODEL, MOD_TILE), lambda l, j: (l, 0, j)),
            pl.BlockSpec((1, 1, MOD_TILE), lambda l, j: (l, 0, j)),
        ],
        out_specs=pl.BlockSpec((1, MOD_ROWS, MOD_TILE), lambda l, j: (l, 0, j)),
        out_shape=jax.ShapeDtypeStruct((DEPTH, MOD_ROWS, n_col), _F32),
        compiler_params=_cparams(("parallel", "parallel")),
        name="modulation",
    )(c_all, w_mod, b_mod.reshape(DEPTH, 1, n_col))


def _rope(x, cos_ref, se_ref, so_ref):
    cos, se, so = cos_ref[...], se_ref[...], so_ref[...]
    outs = []
    for c in range(x.shape[1] // LANES):
        xc = x[:, c * LANES:(c + 1) * LANES]
        nxt = pltpu.roll(xc, LANES - 1, 1)
        prv = pltpu.roll(xc, 1, 1)
        outs.append(xc * cos + nxt * se + prv * so)
    return outs[0] if len(outs) == 1 else jnp.concatenate(outs, axis=1)


def _pre_kernel(*refs, rope, emit_kv):
    x_ref, mod_ref, ng_ref, win_ref, qg_ref, kg_ref, indq_ref, indk_ref, sg_ref = refs[:9]
    refs = refs[9:]
    if rope:
        cos_ref, se_ref, so_ref = refs[:3]
        refs = refs[3:]
    q_ref, k_ref, vt_ref, ap_ref, u_ref, vn_ref, h_ref = refs[:7]
    if emit_kv:
        kf_ref, vf_ref = refs[7:9]

    x = x_ref[0]
    xn = x * lax.rsqrt(jnp.mean(x * x, axis=-1, keepdims=True) + EPS) * ng_ref[...]
    h = xn * (1.0 + mod_ref[0, 1:2, :]) + mod_ref[0, 0:1, :]
    hb = h.astype(_BF16)
    h_ref[0] = hb

    zq = _dot(hb, win_ref[:, O_Q:O_Q + Q_W])
    msq = _dot((zq * zq).astype(_BF16), indq_ref[...])
    qn = zq * lax.rsqrt(msq + EPS) * qg_ref[...]
    if rope:
        qn = _rope(qn, cos_ref, se_ref, so_ref)
    q_ref[0] = (qn * Q_SCALE).astype(_BF16)

    zkv = _dot(hb, win_ref[:, O_KV:O_KV + 2 * KV_W])
    zk = zkv[:, :KV_W]
    zv = zkv[:, KV_W:]
    msk = _dot((zk * zk).astype(_BF16), indk_ref[...])
    kn = zk * lax.rsqrt(msk + EPS) * kg_ref[...]
    if emit_kv:
        kf_ref[0] = kn
        vf_ref[0] = zv
    if rope:
        kn = _rope(kn, cos_ref, se_ref, so_ref)
    k_ref[0] = kn.astype(_BF16)
    vt = zv.T.astype(_BF16)
    for j in range(N_KV_HEADS):
        vt_ref[0, j * VT_HEAD_ROWS:j * VT_HEAD_ROWS + HEAD_DIM, :] = vt[j * HEAD_DIM:(j + 1) * HEAD_DIM, :]
        vt_ref[0, j * VT_HEAD_ROWS + HEAD_DIM:(j + 1) * VT_HEAD_ROWS, :] = jnp.ones((ONES_ROWS, vt.shape[1]), _BF16)

    za = _dot(hb, win_ref[:, O_A:O_A + 2 * CONV_W])
    ap_ref[0, :, 0:CONV_W] = za[:, :CONV_W] * jax.nn.sigmoid(za[:, CONV_W:])
    ap_ref[0, :, CONV_W:CONV_W + POOL_W] = _dot(hb, win_ref[:, O_P:O_P + POOL_W])

    zs = jax.nn.gelu(_dot(hb, win_ref[:, O_S:O_S + 2 * SGU_W]))
    u_ref[0] = zs[:, :SGU_W].astype(_BF16)
    v = zs[:, SGU_W:]
    vn = v * lax.rsqrt(jnp.mean(v * v, axis=-1, keepdims=True) + EPS) * sg_ref[...]
    vn_ref[0] = vn.astype(_BF16)


def _pre(x, mod, lw, rope_tabs, emit_kv):
    bsz, n, _ = x.shape
    t = TOK_TILE
    rope = rope_tabs is not None
    row = lambda b, i: (b, i, 0)
    const2 = lambda b, i: (0, 0)
    in_specs = [
        pl.BlockSpec((1, t, D_MODEL), row),
        pl.BlockSpec((1, 6, D_MODEL), lambda b, i: (b, 0, 0)),
        pl.BlockSpec((1, D_MODEL), const2),
        pl.BlockSpec((D_MODEL, IN_W), const2),
        pl.BlockSpec((1, Q_W), const2),
        pl.BlockSpec((1, KV_W), const2),
        pl.BlockSpec((Q_W, Q_W), const2),
        pl.BlockSpec((KV_W, KV_W), const2),
        pl.BlockSpec((1, SGU_W), const2),
    ]
    args = [x, mod, lw['norm1_g'], lw['w_in'], lw['q_g'], lw['k_g'], lw['ind_q'], lw['ind_k'], lw['sgu_norm_g']]
    if rope:
        in_specs += [pl.BlockSpec((t, LANES), lambda b, i: (i, 0))] * 3
        args += list(rope_tabs)
    out_specs = [
        pl.BlockSpec((1, t, Q_W), row),
        pl.BlockSpec((1, t, KV_W), row),
        pl.BlockSpec((1, VT_ROWS, t), lambda b, i: (b, 0, i)),
        pl.BlockSpec((1, t, CONV_W + POOL_W), row),
        pl.BlockSpec((1, t, SGU_W), row),
        pl.BlockSpec((1, t, SGU_W), row),
        pl.BlockSpec((1, t, D_MODEL), row),
    ]
    out_shape = [
        jax.ShapeDtypeStruct((bsz, n, Q_W), _BF16),
        jax.ShapeDtypeStruct((bsz, n, KV_W), _BF16),
        jax.ShapeDtypeStruct((bsz, VT_ROWS, n), _BF16),
        jax.ShapeDtypeStruct((bsz, n, CONV_W + POOL_W), _F32),
        jax.ShapeDtypeStruct((bsz, n, SGU_W), _BF16),
        jax.ShapeDtypeStruct((bsz, n, SGU_W), _BF16),
        jax.ShapeDtypeStruct((bsz, n, D_MODEL), _BF16),
    ]
    if emit_kv:
        out_specs += [pl.BlockSpec((1, t, KV_W), row)] * 2
        out_shape += [jax.ShapeDtypeStruct((bsz, n, KV_W), _F32)] * 2
    return pl.pallas_call(
        functools.partial(_pre_kernel, rope=rope, emit_kv=emit_kv),
        grid=(bsz, n // t),
        in_specs=in_specs,
        out_specs=out_specs,
        out_shape=out_shape,
        compiler_params=_cparams(("parallel", "parallel")),
        name="pre_lat" if rope else "pre_ctx",
    )(*args)


def _split_heads(q_ref, qh_ref):
    tq = q_ref.shape[1]
    hi_half = lax.broadcasted_iota(jnp.int32, (1, LANES), 1) >= HEAD_DIM
    for g in range(GROUP):
        qc = q_ref[0, :, g * LANES:(g + 1) * LANES]
        for j in range(N_KV_HEADS):
            qh_ref[j, g * tq:(g + 1) * tq, :] = jnp.where(hi_half if j == 1 else jnp.logical_not(hi_half), qc,
                                                          jnp.zeros_like(qc))


def _write_heads(acc_ref, o_ref):
    tq = o_ref.shape[1]
    for g in range(GROUP):
        halves = []
        for j in range(N_KV_HEADS):
            acc = acc_ref[j, :, g * tq:(g + 1) * tq]
            halves.append(acc[0:HEAD_DIM, :] / acc[HEAD_DIM:HEAD_DIM + 1, :])
        o_ref[0, :, g * LANES:(g + 1) * LANES] = jnp.concatenate(halves, axis=0).T.astype(_BF16)


_NT = (((1,), (1,)), ((), ()))


def _attn_kernel(q_ref, k_ref, vt_ref, o_ref, qh_ref, m_ref, alpha_ref, acc_ref, s_ref, *, n_keys, key_block):
    _split_heads(q_ref, qh_ref)
    m_ref[...] = jnp.full(m_ref.shape, -1e30, _F32)
    acc_ref[...] = jnp.zeros(acc_ref.shape, _F32)

    def scores(off, j):
        s = lax.dot_general(k_ref[0, pl.ds(off, key_block), :], qh_ref[j], _NT, preferred_element_type=_F32)
        m_old = m_ref[j]
        m_new = jnp.maximum(m_old, jnp.max(s, axis=0, keepdims=True))
        m_ref[j] = m_new
        alpha_ref[j] = jnp.exp2(m_old - m_new)
        s_ref[j] = s

    def values(off, j):
        p = jnp.exp2(s_ref[j] - m_ref[j]).astype(_BF16)
        vblk = vt_ref[0, j * VT_HEAD_ROWS:(j + 1) * VT_HEAD_ROWS, pl.ds(off, key_block)]
        acc_ref[j] = alpha_ref[j] * acc_ref[j] + _dot(vblk, p)

    n_blocks = n_keys // key_block
    unroll = 2 if n_blocks % 2 == 0 else 1
    scores(0, 0)

    def body(b, carry):
        for r in range(unroll):
            off = pl.multiple_of((b * unroll + r) * key_block, key_block)
            scores(off, 1)
            values(off, 0)
            scores(off + key_block, 0)
            values(off, 1)
        return carry

    lax.fori_loop(0, n_blocks // unroll - 1, body, 0)
    for r in range(unroll):
        off = (n_blocks - unroll + r) * key_block
        scores(off, 1)
        values(off, 0)
        if r < unroll - 1:
            scores(off + key_block, 0)
        values(off, 1)
    _write_heads(acc_ref, o_ref)


def _attn_bounded_kernel(bound_ref, q_ref, ck_ref, cvt_ref, k_ref, vt_ref, o_ref, qh_ref, acc_ref, *, n_keys, key_block):
    _split_heads(q_ref, qh_ref)
    shift = bound_ref[0, 0]

    def weighted_values(kblk, vtblk, j):
        s = lax.dot_general(kblk, qh_ref[j], _NT, preferred_element_type=_F32)
        return _dot(vtblk, jnp.exp2(s - shift).astype(_BF16))

    for j in range(N_KV_HEADS):
        acc_ref[j] = weighted_values(ck_ref[0], cvt_ref[0, j * VT_HEAD_ROWS:(j + 1) * VT_HEAD_ROWS, :], j)

    n_blocks = n_keys // key_block
    unroll = 2 if n_blocks % 2 == 0 else 1

    def body(b, carry):
        for r in range(unroll):
            off = pl.multiple_of((b * unroll + r) * key_block, key_block)
            for j in range(N_KV_HEADS):
                vtblk = vt_ref[0, j * VT_HEAD_ROWS:(j + 1) * VT_HEAD_ROWS, pl.ds(off, key_block)]
                acc_ref[j] = acc_ref[j] + weighted_values(k_ref[0, pl.ds(off, key_block), :], vtblk, j)
        return carry

    lax.fori_loop(0, n_blocks // unroll, body, 0)
    _write_heads(acc_ref, o_ref)


def _attention_specs(q, tq, q_run, n_keys):
    per_run = q_run // tq
    qmap = lambda b, i: (b, i, 0)
    in_specs = [
        pl.BlockSpec((1, tq, Q_W), qmap),
        pl.BlockSpec((1, n_keys, KV_W), lambda b, i: (b, i // per_run, 0)),
        pl.BlockSpec((1, VT_ROWS, n_keys), lambda b, i: (b, 0, i // per_run)),
    ]
    out_spec = pl.BlockSpec((1, tq, Q_W), qmap)
    out_shape = jax.ShapeDtypeStruct(q.shape, _BF16)
    qh_scratch = pltpu.VMEM((N_KV_HEADS, GROUP * tq, LANES), _BF16)
    acc_scratch = pltpu.VMEM((N_KV_HEADS, VT_HEAD_ROWS, GROUP * tq), _F32)
    return in_specs, out_spec, out_shape, qh_scratch, acc_scratch


def _attention_online(q, k, vt, q_run, n_keys):
    bsz, n, _ = q.shape
    tq = min(Q_TILE_ONLINE, q_run)
    key_block = min(KEY_BLOCK, n_keys)
    in_specs, out_spec, out_shape, qh_scratch, acc_scratch = _attention_specs(q, tq, q_run, n_keys)
    row_scratch = pltpu.VMEM((N_KV_HEADS, 1, GROUP * tq), _F32)
    return pl.pallas_call(
        functools.partial(_attn_kernel, n_keys=n_keys, key_block=key_block),
        grid=(bsz, n // tq),
        in_specs=in_specs,
        out_specs=out_spec,
        out_shape=out_shape,
        scratch_shapes=[qh_scratch, row_scratch, row_scratch, acc_scratch,
                        pltpu.VMEM((N_KV_HEADS, key_block, GROUP * tq), _F32)],
        compiler_params=_cparams(("parallel", "arbitrary")),
        name="attn_online",
    )(q, k, vt)


def _attention_cached(q, ck, cvt, k, vt, score_bound):
    bsz, n, _ = q.shape
    past = ck.shape[1]
    tq = min(Q_TILE, n)
    in_specs, out_spec, out_shape, qh_scratch, acc_scratch = _attention_specs(q, tq, n, n)
    cache_specs = [
        pl.BlockSpec((1, past, KV_W), lambda b, i: (b, 0, 0)),
        pl.BlockSpec((1, VT_ROWS, past), lambda b, i: (b, 0, 0)),
    ]

    def bounded(q, ck, cvt, k, vt):
        return pl.pallas_call(
            functools.partial(_attn_bounded_kernel, n_keys=n, key_block=min(KEY_BLOCK_BOUNDED, n)),
            grid=(bsz, n // tq),
            in_specs=[pl.BlockSpec(memory_space=pltpu.SMEM), in_specs[0]] + cache_specs + in_specs[1:],
            out_specs=out_spec,
            out_shape=out_shape,
            scratch_shapes=[qh_scratch, acc_scratch],
            compiler_params=_cparams(("parallel", "arbitrary")),
            name="attn_bounded",
        )(score_bound.reshape(1, 1), q, ck, cvt, k, vt)

    def online(q, ck, cvt, k, vt):
        return _attention_online(q, jnp.concatenate([ck, k], axis=1), jnp.concatenate([cvt, vt], axis=2),
                                 n, past + n)

    return lax.cond(score_bound < MAX_SCORE_BOUND, bounded, online, q, ck, cvt, k, vt)


def _mix_kernel(ap_ref, prev_ref, next_ref, u_ref, vn_ref, cw_ref, cb_ref, lg_ref, lb_ref, pw_ref, ps_ref,
                sw_ref, sb_ref, conv_ref, pool_ref, sgu_ref, buf_ref, xs_ref, cacc_ref, tmp_ref, *, seq_len):
    t = ap_ref.shape[1]
    i = pl.program_id(1)
    last = pl.num_programs(1) - 1
    buf_ref[0:HALO, :] = jnp.where(i > 0, prev_ref[0], 0.0)
    buf_ref[HALO:HALO + t, :] = ap_ref[0]
    buf_ref[HALO + t:HALO + t + HALO, :] = jnp.where(i < last, next_ref[0], 0.0)
    buf_ref[t + 2 * HALO:t + 2 * HALO + SUBLANES, :] = jnp.zeros((SUBLANES, buf_ref.shape[1]), _F32)

    n_sh = t + 2 * HALO - SUBLANES
    for r in range(1, SUBLANES):
        xs_ref[r - 1] = buf_ref[r:r + n_sh, 0:CONV_W]

    def conv_rows(ci, carry):
        base = pl.multiple_of(ci * CONV_ROWS, CONV_ROWS)
        for c0 in range(0, CONV_W, CONV_LANES):
            cols = slice(c0, c0 + CONV_LANES)
            acc = jnp.broadcast_to(cb_ref[:, cols], (CONV_ROWS, CONV_LANES))
            for k in range(CONV_K):
                q8, r = divmod(HALO - CONV_K // 2 + k, SUBLANES)
                rows = pl.ds(base + q8 * SUBLANES, CONV_ROWS)
                x = buf_ref[rows, cols] if r == 0 else xs_ref[r - 1, rows, cols]
                acc = acc + x * cw_ref[k:k + 1, cols]
            cacc_ref[pl.ds(base, CONV_ROWS), cols] = acc
        return carry

    lax.fori_loop(0, t // CONV_ROWS, conv_rows, 0)
    acc = cacc_ref[...]
    mu = jnp.mean(acc, axis=-1, keepdims=True)
    cen = acc - mu
    var = jnp.mean(cen * cen, axis=-1, keepdims=True)
    y = cen * lax.rsqrt(var + EPS) * lg_ref[...] + lb_ref[...]
    conv_ref[0] = (y * jax.nn.sigmoid(y)).astype(_BF16)

    pos = i * t + lax.broadcasted_iota(jnp.int32, (t, 1), 0)
    for g, w in enumerate(POOL_WINDOWS):
        cols = slice(CONV_W + g * POOL_GC, CONV_W + (g + 1) * POOL_GC)
        first = HALO - w // 2
        if w == 2:
            s = buf_ref[first:first + t, cols] + buf_ref[first + 1:first + 1 + t, cols]
        else:
            n = t + 2 * HALO
            tmp_ref[0, 0:n, :] = buf_ref[0:n, cols] + buf_ref[1:n + 1, cols]
            span, stage = 2, 0
            while span * 2 < w:
                n -= SUBLANES
                tmp_ref[stage + 1, 0:n, :] = tmp_ref[stage, 0:n, :] + tmp_ref[stage, span:span + n, :]
                span, stage = span * 2, stage + 1
            s = tmp_ref[stage, first:first + t, :] + tmp_ref[stage, first + span:first + span + t, :]
        lo = jnp.maximum(pos - w // 2, 0)
        hi = jnp.minimum(pos - w // 2 + w, seq_len)
        pooled = s / (hi - lo).astype(_F32) - buf_ref[HALO:HALO + t, cols]
        yg = _dot(pooled.astype(_BF16), pw_ref[g])
        pool_ref[0, :, g * POOL_GC:(g + 1) * POOL_GC] = (yg * ps_ref[:, g * POOL_GC:(g + 1) * POOL_GC]).astype(_BF16)

    n_chunk = t // SGU_CHUNK
    for g in range(SGU_GROUPS):
        cols = slice(g * SGU_GC, (g + 1) * SGU_GC)
        rhs = jnp.concatenate([vn_ref[0, c * SGU_CHUNK:(c + 1) * SGU_CHUNK, cols] for c in range(n_chunk)], axis=1)
        sv = _dot(sw_ref[g], rhs)
        for c in range(n_chunk):
            rows = slice(c * SGU_CHUNK, (c + 1) * SGU_CHUNK)
            gate = sv[:, c * SGU_GC:(c + 1) * SGU_GC] + sb_ref[:, cols]
            sgu_ref[0, rows, cols] = (u_ref[0, rows, cols].astype(_F32) * gate).astype(_BF16)


def _mix(ap, u, vn, lw, t):
    bsz, n, _ = ap.shape
    hb = t // HALO
    n_hb = n // HALO
    row = lambda b, i: (b, i, 0)
    const2 = lambda b, i: (0, 0)
    const3 = lambda b, i: (0, 0, 0)
    w_all = CONV_W + POOL_W
    in_specs = [
        pl.BlockSpec((1, t, w_all), row),
        pl.BlockSpec((1, HALO, w_all), lambda b, i: (b, jnp.maximum(i * hb - 1, 0), 0)),
        pl.BlockSpec((1, HALO, w_all), lambda b, i: (b, jnp.minimum((i + 1) * hb, n_hb - 1), 0)),
        pl.BlockSpec((1, t, SGU_W), row),
        pl.BlockSpec((1, t, SGU_W), row),
        pl.BlockSpec((CONV_K, CONV_W), const2),
        pl.BlockSpec((1, CONV_W), const2),
        pl.BlockSpec((1, CONV_W), const2),
        pl.BlockSpec((1, CONV_W), const2),
        pl.BlockSpec((POOL_GROUPS, POOL_GC, POOL_GC), const3),
        pl.BlockSpec((1, POOL_W), const2),
        pl.BlockSpec((SGU_GROUPS, SGU_CHUNK, SGU_CHUNK), const3),
        pl.BlockSpec((SGU_CHUNK, SGU_W), const2),
    ]
    o_spec = pl.BlockSpec((1, t, BRANCH_W), row)
    o_shape = jax.ShapeDtypeStruct((bsz, n, BRANCH_W), _BF16)
    return pl.pallas_call(
        functools.partial(_mix_kernel, seq_len=n),
        grid=(bsz, n // t),
        in_specs=in_specs,
        out_specs=[o_spec] * 3,
        out_shape=[o_shape] * 3,
        scratch_shapes=[
            pltpu.VMEM((t + 2 * HALO + SUBLANES, w_all), _F32),
            pltpu.VMEM((SUBLANES - 1, t + 2 * HALO - SUBLANES, CONV_W), _F32),
            pltpu.VMEM((t, CONV_W), _F32),
            pltpu.VMEM((3, t + 2 * HALO, POOL_GC), _F32),
        ],
        compiler_params=_cparams(("parallel", "parallel")),
        name="mix",
    )(ap, ap, ap, u, vn, lw['conv_w'], lw['conv_b'], lw['conv_ln_g'], lw['conv_ln_b'], lw['pool_w'],
      lw['pool_scale'], lw['sgu_w'], lw['sgu_bias'])


def _merge_kernel(x_ref, h_ref, b0_ref, b1_ref, b2_ref, b3_ref, mod_ref, wg_ref, bg_ref, wb_ref, wo_ref, o_ref):
    hb = h_ref[0]
    merged = None
    for idx, br_ref in enumerate((b0_ref, b1_ref, b2_ref, b3_ref)):
        cols = slice(idx * D_MODEL, (idx + 1) * D_MODEL)
        gate = jax.nn.sigmoid(_dot(hb, wg_ref[:, cols]) + bg_ref[:, cols])
        term = gate * _dot(br_ref[0], wb_ref[idx])
        merged = term if merged is None else merged + term
    m = _dot(merged.astype(_BF16), wo_ref[...])
    o_ref[0] = x_ref[0] + mod_ref[0, 2:3, :] * m


def _merge(x, h, branches, mod, lw):
    bsz, n, _ = x.shape
    t = TOK_TILE
    row = lambda b, i: (b, i, 0)
    const2 = lambda b, i: (0, 0)
    in_specs = [
        pl.BlockSpec((1, t, D_MODEL), row),
        pl.BlockSpec((1, t, D_MODEL), row),
    ] + [pl.BlockSpec((1, t, BRANCH_W), row)] * N_BRANCH + [
        pl.BlockSpec((1, 6, D_MODEL), lambda b, i: (b, 0, 0)),
        pl.BlockSpec((D_MODEL, N_BRANCH * D_MODEL), const2),
        pl.BlockSpec((1, N_BRANCH * D_MODEL), const2),
        pl.BlockSpec((N_BRANCH, BRANCH_W, D_MODEL), lambda b, i: (0, 0, 0)),
        pl.BlockSpec((D_MODEL, D_MODEL), const2),
    ]
    return pl.pallas_call(
        _merge_kernel,
        grid=(bsz, n // t),
        in_specs=in_specs,
        out_specs=pl.BlockSpec((1, t, D_MODEL), row),
        out_shape=jax.ShapeDtypeStruct((bsz, n, D_MODEL), _F32),
        compiler_params=_cparams(("parallel", "parallel")),
        name="merge",
    )(x, h, *branches, mod, lw['w_gate'], lw['b_gate'], lw['w_branch'], lw['w_out'])


def _mlp_kernel(x_ref, mod_ref, ng_ref, w1_ref, w2_ref, fg_ref, o_ref, hid_ref, *, final_norm):
    x = x_ref[0]
    xn = x * lax.rsqrt(jnp.mean(x * x, axis=-1, keepdims=True) + EPS) * ng_ref[...]
    hb = (xn * (1.0 + mod_ref[0, 4:5, :]) + mod_ref[0, 3:4, :]).astype(_BF16)
    for c in range(D_FF // D_MODEL):
        cols = slice(c * D_MODEL, (c + 1) * D_MODEL)
        a = jnp.maximum(_dot(hb, w1_ref[:, cols]), 0.0)
        hid_ref[:, cols] = (a * a).astype(_BF16)
    y = x + mod_ref[0, 5:6, :] * _dot(hid_ref[...], w2_ref[...])
    if final_norm:
        y = y * lax.rsqrt(jnp.mean(y * y, axis=-1, keepdims=True) + EPS) * fg_ref[...]
    o_ref[0] = y


def _mlp(x, mod, lw, final_g, final_norm):
    bsz, n, _ = x.shape
    t = TOK_TILE
    row = lambda b, i: (b, i, 0)
    const2 = lambda b, i: (0, 0)
    return pl.pallas_call(
        functools.partial(_mlp_kernel, final_norm=final_norm),
        grid=(bsz, n // t),
        in_specs=[
            pl.BlockSpec((1, t, D_MODEL), row),
            pl.BlockSpec((1, 6, D_MODEL), lambda b, i: (b, 0, 0)),
            pl.BlockSpec((1, D_MODEL), const2),
            pl.BlockSpec((D_MODEL, D_FF), const2),
            pl.BlockSpec((D_FF, D_MODEL), const2),
            pl.BlockSpec((1, D_MODEL), const2),
        ],
        out_specs=pl.BlockSpec((1, t, D_MODEL), row),
        out_shape=jax.ShapeDtypeStruct((bsz, n, D_MODEL), _F32),
        scratch_shapes=[pltpu.VMEM((t, D_FF), _BF16)],
        compiler_params=_cparams(("parallel", "parallel")),
        name="mlp",
    )(x, mod, lw['norm2_g'], lw['w_mlp_in'], lw['w_mlp_out'], final_g)


def _rope_tables(n):
    rows = n // GRID_W
    row = jnp.repeat(jnp.arange(rows, dtype=_F32), GRID_W)
    col = jnp.tile(jnp.arange(GRID_W, dtype=_F32), rows)
    inv = ROPE_THETA ** (-jnp.arange(0, AXIS_DIM, 2, dtype=_F32) / AXIS_DIM)
    ang = jnp.concatenate([row[:, None] * inv, col[:, None] * inv], axis=-1)
    cos = jnp.repeat(jnp.cos(ang), 2, axis=-1)
    sin = jnp.repeat(jnp.sin(ang), 2, axis=-1)
    even = (jnp.arange(HEAD_DIM) % 2 == 0)[None, :]
    se = jnp.where(even, -sin, 0.0)
    so = jnp.where(even, 0.0, sin)
    rep = LANES // HEAD_DIM
    return tuple(jnp.tile(tab, (1, rep)) for tab in (cos, se, so))


def _head_indicator(width):
    head = jnp.arange(width) // HEAD_DIM
    return ((head[:, None] == head[None, :]).astype(_F32) / HEAD_DIM).astype(_BF16)


def _layer_weights(l, w_in, norm1_g, q_norm_g, k_norm_g, conv_w, conv_b, conv_ln_g, conv_ln_b, pool_w,
                   pool_scale, sgu_norm_g, sgu_w, sgu_b, w_branch, w_gate, b_gate, w_out, norm2_g,
                   w_mlp_in, w_mlp_out):
    win = w_in[l].astype(_BF16)
    wq = win[:, :Q_W].reshape(D_MODEL, N_KV_HEADS, GROUP, HEAD_DIM).transpose(0, 2, 1, 3).reshape(D_MODEL, Q_W)
    win = jnp.concatenate([wq, win[:, Q_W:]], axis=1)
    wb = w_branch[l].astype(_BF16)
    wb0 = wb[0].reshape(N_KV_HEADS, GROUP, HEAD_DIM, D_MODEL).transpose(1, 0, 2, 3).reshape(Q_W, D_MODEL)
    wb = jnp.concatenate([wb0[None], wb[1:]], axis=0)
    row = lambda v: v.reshape(1, -1)
    return {
        'norm1_g': row(norm1_g[l]),
        'w_in': win,
        'q_g': row(jnp.tile(q_norm_g[l], N_HEADS)),
        'k_g': row(jnp.tile(k_norm_g[l], N_KV_HEADS)),
        'ind_q': _head_indicator(Q_W),
        'ind_k': _head_indicator(KV_W),
        'sgu_norm_g': row(sgu_norm_g[l]),
        'conv_w': conv_w[l],
        'conv_b': row(conv_b[l]),
        'conv_ln_g': row(conv_ln_g[l]),
        'conv_ln_b': row(conv_ln_b[l]),
        'pool_w': pool_w[l].astype(_BF16),
        'pool_scale': row(pool_scale[l]),
        'sgu_w': sgu_w[l].astype(_BF16),
        'sgu_bias': jnp.repeat(sgu_b[l].T, SGU_GC, axis=1),
        'w_gate': w_gate[l].astype(_BF16),
        'b_gate': row(b_gate[l]),
        'w_branch': wb,
        'w_out': w_out[l].astype(_BF16),
        'norm2_g': row(norm2_g[l]),
        'w_mlp_in': w_mlp_in[l].astype(_BF16),
        'w_mlp_out': w_mlp_out[l].astype(_BF16),
    }


def kernel(x_prompt, x_sample, cache_k, cache_v, c, c_ctx, w_mod, b_mod, norm1_g, w_in, q_norm_g, k_norm_g, conv_w, conv_b, conv_ln_g, conv_ln_b, pool_w, pool_scale, sgu_norm_g, sgu_w, sgu_b, w_branch, w_gate, b_gate, w_out, norm2_g, w_mlp_in, w_mlp_out, final_norm_g):
    batch, seq, _ = x_prompt.shape
    dec_batch, dec_seq, _ = x_sample.shape
    past = cache_k.shape[2]
    n_ctx = batch * seq

    c_all = jnp.concatenate([c, c_ctx[None, :], jnp.zeros((MOD_ROWS - dec_batch - 1, D_MODEL), _F32)], axis=0)
    mod = _modulation(c_all, w_mod, b_mod).reshape(DEPTH, MOD_ROWS, 6, D_MODEL)

    rope_tabs = _rope_tables(dec_seq)
    final_g = final_norm_g.reshape(1, D_MODEL)

    xp = x_prompt.reshape(1, n_ctx, D_MODEL)
    xs = x_sample
    new_k, new_v = [], []
    for l in range(DEPTH):
        lw = _layer_weights(l, w_in, norm1_g, q_norm_g, k_norm_g, conv_w, conv_b, conv_ln_g, conv_ln_b, pool_w,
                            pool_scale, sgu_norm_g, sgu_w, sgu_b, w_branch, w_gate, b_gate, w_out, norm2_g,
                            w_mlp_in, w_mlp_out)
        mod_lat = mod[l, :dec_batch]
        mod_ctx = mod[l, dec_batch:dec_batch + 1]
        last = l == DEPTH - 1

        q, k, vt, ap, u, vn, h, kf, vf = _pre(xp, mod_ctx, lw, None, True)
        new_k.append(kf.reshape(batch, seq, N_KV_HEADS, HEAD_DIM))
        new_v.append(vf.reshape(batch, seq, N_KV_HEADS, HEAD_DIM))
        attn = _attention_online(q, k, vt, seq, seq)
        conv, pool, sgu = _mix(ap.reshape(batch, seq, -1), u.reshape(batch, seq, -1), vn.reshape(batch, seq, -1),
                               lw, seq)
        branches = (attn, conv.reshape(1, n_ctx, -1), pool.reshape(1, n_ctx, -1), sgu.reshape(1, n_ctx, -1))
        xp = _merge(xp, h, branches, mod_ctx, lw)
        xp = _mlp(xp, mod_ctx, lw, final_g, last)

        q, k, vt, ap, u, vn, h = _pre(xs, mod_lat, lw, rope_tabs, False)
        ck = cache_k[:, l].reshape(dec_batch, past, KV_W).astype(_BF16)
        cvt = jnp.transpose(cache_v[:, l], (0, 2, 3, 1)).astype(_BF16)
        cvt = jnp.concatenate([cvt, jnp.ones((dec_batch, N_KV_HEADS, ONES_ROWS, past), _BF16)], axis=2)
        cvt = cvt.reshape(dec_batch, VT_ROWS, past)
        q_norm = HEAD_DIM ** 0.5 * jnp.max(jnp.abs(q_norm_g[l])) * Q_SCALE
        k_norm = HEAD_DIM ** 0.5 * jnp.max(jnp.abs(k_norm_g[l]))
        ck_sq = jnp.square(ck.astype(_F32)).reshape(dec_batch, past, N_KV_HEADS, HEAD_DIM)
        k_norm = jnp.maximum(k_norm, jnp.sqrt(jnp.max(jnp.sum(ck_sq, axis=-1))))
        attn = _attention_cached(q, ck, cvt, k, vt, BOUND_MARGIN * q_norm * k_norm)
        conv, pool, sgu = _mix(ap, u, vn, lw, TOK_TILE)
        xs = _merge(xs, h, (attn, conv, pool, sgu), mod_lat, lw)
        xs = _mlp(xs, mod_lat, lw, final_g, last)

    y_prompt = xp.reshape(batch, seq, D_MODEL)
    return (y_prompt, xs, jnp.stack(new_k, axis=1), jnp.stack(new_v, axis=1))
```

```python
import functools

import jax
import jax.numpy as jnp
from jax import lax
from jax.experimental import pallas as pl
from jax.experimental.pallas import tpu as pltpu

D_MODEL = 1024
DEPTH = 2
GRID_W = 64
N_HEADS = 8
N_KV_HEADS = 2
HEAD_DIM = 64
Q_W = N_HEADS * HEAD_DIM
KV_W = N_KV_HEADS * HEAD_DIM
GROUP = N_HEADS // N_KV_HEADS
AXIS_DIM = HEAD_DIM // 2
ROPE_THETA = 10000.0
CONV_W = 512
CONV_K = 31
POOL_W = 512
POOL_GROUPS = 4
POOL_GC = POOL_W // POOL_GROUPS
POOL_WINDOWS = (2, 4, 8, 16)
SGU_W = 512
SGU_GROUPS = 4
SGU_GC = SGU_W // SGU_GROUPS
SGU_CHUNK = 128
BRANCH_W = 512
N_BRANCH = 4
D_FF = 4 * D_MODEL
IN_W = Q_W + 2 * KV_W + 2 * CONV_W + POOL_W + 2 * SGU_W
EPS = 1e-6

O_Q = 0
O_KV = Q_W
O_A = Q_W + 2 * KV_W
O_P = O_A + 2 * CONV_W
O_S = O_P + POOL_W

LANES = 128
SUBLANES = 8
CONV_ROWS = 64
HALO = 16
TOK_TILE = 512
Q_TILE = 1024
Q_TILE_ONLINE = 512
KEY_BLOCK = 768
KEY_BLOCK_BOUNDED = 1024
ONES_ROWS = 16
VT_HEAD_ROWS = HEAD_DIM + ONES_ROWS
VT_ROWS = N_KV_HEADS * VT_HEAD_ROWS
MOD_ROWS = 16
MOD_TILE = 1536
VMEM_LIMIT = 56 * 1024 * 1024
Q_SCALE = HEAD_DIM ** -0.5 * 1.4426950408889634
BOUND_MARGIN = 1.02
MAX_SCORE_BOUND = 40.0

_F32 = jnp.float32
_BF16 = jnp.bfloat16


def _dot(a, b):
    return jnp.dot(a, b, preferred_element_type=_F32)


def _cparams(sem, flags=None):
    return pltpu.CompilerParams(dimension_semantics=sem, vmem_limit_bytes=VMEM_LIMIT, flags=flags)


def _mod_kernel(c_ref, w_ref, b_ref, o_ref):
    c = c_ref[...]
    cs = c * jax.nn.sigmoid(c)
    o_ref[0] = _dot(cs.astype(_BF16), w_ref[0].astype(_BF16)) + b_ref[0]


def _modulation(c_all, w_mod, b_mod):
    n_col = 6 * D_MODEL
    return pl.pallas_call(
        _mod_kernel,
        grid=(DEPTH, n_col // MOD_TILE),
        in_specs=[
            pl.BlockSpec((MOD_ROWS, D_MODEL), lambda l, j: (0, 0)),
            pl.BlockSpec((1, D_MODEL, MOD_TILE), lambda l, j: (l, 0, j)),
            pl.BlockSpec((1, 1, MOD_TILE), lambda l, j: (l, 0, j)),
        ],
        out_specs=pl.BlockSpec((1, MOD_ROWS, MOD_TILE), lambda l, j: (l, 0, j)),
        out_shape=jax.ShapeDtypeStruct((DEPTH, MOD_ROWS, n_col), _F32),
        compiler_params=_cparams(("parallel", "parallel")),
        name="modulation",
    )(c_all, w_mod, b_mod.reshape(DEPTH, 1, n_col))


def _rope(x, cos_ref, se_ref, so_ref):
    cos, se, so = cos_ref[...], se_ref[...], so_ref[...]
    outs = []
    for c in range(x.shape[1] // LANES):
        xc = x[:, c * LANES:(c + 1) * LANES]
        nxt = pltpu.roll(xc, LANES - 1, 1)
        prv = pltpu.roll(xc, 1, 1)
        outs.append(xc * cos + nxt * se + prv * so)
    return outs[0] if len(outs) == 1 else jnp.concatenate(outs, axis=1)


def _pre_kernel(*refs, rope, emit_kv):
    x_ref, mod_ref, ng_ref, win_ref, qg_ref, kg_ref, indq_ref, indk_ref, sg_ref = refs[:9]
    refs = refs[9:]
    if rope:
        cos_ref, se_ref, so_ref = refs[:3]
        refs = refs[3:]
    q_ref, k_ref, vt_ref, ap_ref, u_ref, vn_ref, h_ref = refs[:7]
    if emit_kv:
        kf_ref, vf_ref = refs[7:9]

    x = x_ref[0]
    xn = x * lax.rsqrt(jnp.mean(x * x, axis=-1, keepdims=True) + EPS) * ng_ref[...]
    h = xn * (1.0 + mod_ref[0, 1:2, :]) + mod_ref[0, 0:1, :]
    hb = h.astype(_BF16)
    h_ref[0] = hb

    zq = _dot(hb, win_ref[:, O_Q:O_Q + Q_W])
    msq = _dot((zq * zq).astype(_BF16), indq_ref[...])
    qn = zq * lax.rsqrt(msq + EPS) * qg_ref[...]
    if rope:
        qn = _rope(qn, cos_ref, se_ref, so_ref)
    q_ref[0] = (qn * Q_SCALE).astype(_BF16)

    zkv = _dot(hb, win_ref[:, O_KV:O_KV + 2 * KV_W])
    zk = zkv[:, :KV_W]
    zv = zkv[:, KV_W:]
    msk = _dot((zk * zk).astype(_BF16), indk_ref[...])
    kn = zk * lax.rsqrt(msk + EPS) * kg_ref[...]
    if emit_kv:
        kf_ref[0] = kn
        vf_ref[0] = zv
    if rope:
        kn = _rope(kn, cos_ref, se_ref, so_ref)
    k_ref[0] = kn.astype(_BF16)
    vt = zv.T.astype(_BF16)
    for j in range(N_KV_HEADS):
        vt_ref[0, j * VT_HEAD_ROWS:j * VT_HEAD_ROWS + HEAD_DIM, :] = vt[j * HEAD_DIM:(j + 1) * HEAD_DIM, :]
        vt_ref[0, j * VT_HEAD_ROWS + HEAD_DIM:(j + 1) * VT_HEAD_ROWS, :] = jnp.ones((ONES_ROWS, vt.shape[1]), _BF16)

    za = _dot(hb, win_ref[:, O_A:O_A + 2 * CONV_W])
    ap_ref[0, :, 0:CONV_W] = za[:, :CONV_W] * jax.nn.sigmoid(za[:, CONV_W:])
    ap_ref[0, :, CONV_W:CONV_W + POOL_W] = _dot(hb, win_ref[:, O_P:O_P + POOL_W])

    zs = jax.nn.gelu(_dot(hb, win_ref[:, O_S:O_S + 2 * SGU_W]))
    u_ref[0] = zs[:, :SGU_W].astype(_BF16)
    v = zs[:, SGU_W:]
    vn = v * lax.rsqrt(jnp.mean(v * v, axis=-1, keepdims=True) + EPS) * sg_ref[...]
    vn_ref[0] = vn.astype(_BF16)


def _pre(x, mod, lw, rope_tabs, emit_kv):
    bsz, n, _ = x.shape
    t = TOK_TILE
    rope = rope_tabs is not None
    row = lambda b, i: (b, i, 0)
    const2 = lambda b, i: (0, 0)
    in_specs = [
        pl.BlockSpec((1, t, D_MODEL), row),
        pl.BlockSpec((1, 6, D_MODEL), lambda b, i: (b, 0, 0)),
        pl.BlockSpec((1, D_MODEL), const2),
        pl.BlockSpec((D_MODEL, IN_W), const2),
        pl.BlockSpec((1, Q_W), const2),
        pl.BlockSpec((1, KV_W), const2),
        pl.BlockSpec((Q_W, Q_W), const2),
        pl.BlockSpec((KV_W, KV_W), const2),
        pl.BlockSpec((1, SGU_W), const2),
    ]
    args = [x, mod, lw['norm1_g'], lw['w_in'], lw['q_g'], lw['k_g'], lw['ind_q'], lw['ind_k'], lw['sgu_norm_g']]
    if rope:
        in_specs += [pl.BlockSpec((t, LANES), lambda b, i: (i, 0))] * 3
        args += list(rope_tabs)
    out_specs = [
        pl.BlockSpec((1, t, Q_W), row),
        pl.BlockSpec((1, t, KV_W), row),
        pl.BlockSpec((1, VT_ROWS, t), lambda b, i: (b, 0, i)),
        pl.BlockSpec((1, t, CONV_W + POOL_W), row),
        pl.BlockSpec((1, t, SGU_W), row),
        pl.BlockSpec((1, t, SGU_W), row),
        pl.BlockSpec((1, t, D_MODEL), row),
    ]
    out_shape = [
        jax.ShapeDtypeStruct((bsz, n, Q_W), _BF16),
        jax.ShapeDtypeStruct((bsz, n, KV_W), _BF16),
        jax.ShapeDtypeStruct((bsz, VT_ROWS, n), _BF16),
        jax.ShapeDtypeStruct((bsz, n, CONV_W + POOL_W), _F32),
        jax.ShapeDtypeStruct((bsz, n, SGU_W), _BF16),
        jax.ShapeDtypeStruct((bsz, n, SGU_W), _BF16),
        jax.ShapeDtypeStruct((bsz, n, D_MODEL), _BF16),
    ]
    if emit_kv:
        out_specs += [pl.BlockSpec((1, t, KV_W), row)] * 2
        out_shape += [jax.ShapeDtypeStruct((bsz, n, KV_W), _F32)] * 2
    return pl.pallas_call(
        functools.partial(_pre_kernel, rope=rope, emit_kv=emit_kv),
        grid=(bsz, n // t),
        in_specs=in_specs,
        out_specs=out_specs,
        out_shape=out_shape,
        compiler_params=_cparams(("parallel", "parallel")),
        name="pre_lat" if rope else "pre_ctx",
    )(*args)


def _split_heads(q_ref, qh_ref):
    tq = q_ref.shape[1]
    hi_half = lax.broadcasted_iota(jnp.int32, (1, LANES), 1) >= HEAD_DIM
    for g in range(GROUP):
        qc = q_ref[0, :, g * LANES:(g + 1) * LANES]
        for j in range(N_KV_HEADS):
            qh_ref[j, g * tq:(g + 1) * tq, :] = jnp.where(hi_half if j == 1 else jnp.logical_not(hi_half), qc,
                                                          jnp.zeros_like(qc))


def _write_heads(acc_ref, o_ref):
    tq = o_ref.shape[1]
    for g in range(GROUP):
        halves = []
        for j in range(N_KV_HEADS):
            acc = acc_ref[j, :, g * tq:(g + 1) * tq]
            halves.append(acc[0:HEAD_DIM, :] / acc[HEAD_DIM:HEAD_DIM + 1, :])
        o_ref[0, :, g * LANES:(g + 1) * LANES] = jnp.concatenate(halves, axis=0).T.astype(_BF16)


_NT = (((1,), (1,)), ((), ()))


def _attn_kernel(q_ref, k_ref, vt_ref, o_ref, qh_ref, m_ref, alpha_ref, acc_ref, s_ref, *, n_keys, key_block):
    _split_heads(q_ref, qh_ref)
    m_ref[...] = jnp.full(m_ref.shape, -1e30, _F32)
    acc_ref[...] = jnp.zeros(acc_ref.shape, _F32)

    def scores(off, j):
        s = lax.dot_general(k_ref[0, pl.ds(off, key_block), :], qh_ref[j], _NT, preferred_element_type=_F32)
        m_old = m_ref[j]
        m_new = jnp.maximum(m_old, jnp.max(s, axis=0, keepdims=True))
        m_ref[j] = m_new
        alpha_ref[j] = jnp.exp2(m_old - m_new)
        s_ref[j] = s

    def values(off, j):
        p = jnp.exp2(s_ref[j] - m_ref[j]).astype(_BF16)
        vblk = vt_ref[0, j * VT_HEAD_ROWS:(j + 1) * VT_HEAD_ROWS, pl.ds(off, key_block)]
        acc_ref[j] = alpha_ref[j] * acc_ref[j] + _dot(vblk, p)

    n_blocks = n_keys // key_block
    unroll = 2 if n_blocks % 2 == 0 else 1
    scores(0, 0)

    def body(b, carry):
        for r in range(unroll):
            off = pl.multiple_of((b * unroll + r) * key_block, key_block)
            scores(off, 1)
            values(off, 0)
            scores(off + key_block, 0)
            values(off, 1)
        return carry

    lax.fori_loop(0, n_blocks // unroll - 1, body, 0)
    for r in range(unroll):
        off = (n_blocks - unroll + r) * key_block
        scores(off, 1)
        values(off, 0)
        if r < unroll - 1:
            scores(off + key_block, 0)
        values(off, 1)
    _write_heads(acc_ref, o_ref)


def _attn_bounded_kernel(bound_ref, q_ref, ck_ref, cvt_ref, k_ref, vt_ref, o_ref, qh_ref, acc_ref, *, n_keys, key_block):
    _split_heads(q_ref, qh_ref)
    shift = bound_ref[0, 0]

    def weighted_values(kblk, vtblk, j):
        s = lax.dot_general(kblk, qh_ref[j], _NT, preferred_element_type=_F32)
        return _dot(vtblk, jnp.exp2(s - shift).astype(_BF16))

    for j in range(N_KV_HEADS):
        acc_ref[j] = weighted_values(ck_ref[0], cvt_ref[0, j * VT_HEAD_ROWS:(j + 1) * VT_HEAD_ROWS, :], j)

    n_blocks = n_keys // key_block
    unroll = 2 if n_blocks % 2 == 0 else 1

    def body(b, carry):
        for r in range(unroll):
            off = pl.multiple_of((b * unroll + r) * key_block, key_block)
            for j in range(N_KV_HEADS):
                vtblk = vt_ref[0, j * VT_HEAD_ROWS:(j + 1) * VT_HEAD_ROWS, pl.ds(off, key_block)]
                acc_ref[j] = acc_ref[j] + weighted_values(k_ref[0, pl.ds(off, key_block), :], vtblk, j)
        return carry

    lax.fori_loop(0, n_blocks // unroll, body, 0)
    _write_heads(acc_ref, o_ref)


def _attention_specs(q, tq, q_run, n_keys):
    per_run = q_run // tq
    qmap = lambda b, i: (b, i, 0)
    in_specs = [
        pl.BlockSpec((1, tq, Q_W), qmap),
        pl.BlockSpec((1, n_keys, KV_W), lambda b, i: (b, i // per_run, 0)),
        pl.BlockSpec((1, VT_ROWS, n_keys), lambda b, i: (b, 0, i // per_run)),
    ]
    out_spec = pl.BlockSpec((1, tq, Q_W), qmap)
    out_shape = jax.ShapeDtypeStruct(q.shape, _BF16)
    qh_scratch = pltpu.VMEM((N_KV_HEADS, GROUP * tq, LANES), _BF16)
    acc_scratch = pltpu.VMEM((N_KV_HEADS, VT_HEAD_ROWS, GROUP * tq), _F32)
    return in_specs, out_spec, out_shape, qh_scratch, acc_scratch


def _attention_online(q, k, vt, q_run, n_keys):
    bsz, n, _ = q.shape
    tq = min(Q_TILE_ONLINE, q_run)
    key_block = min(KEY_BLOCK, n_keys)
    in_specs, out_spec, out_shape, qh_scratch, acc_scratch = _attention_specs(q, tq, q_run, n_keys)
    row_scratch = pltpu.VMEM((N_KV_HEADS, 1, GROUP * tq), _F32)
    return pl.pallas_call(
        functools.partial(_attn_kernel, n_keys=n_keys, key_block=key_block),
        grid=(bsz, n // tq),
        in_specs=in_specs,
        out_specs=out_spec,
        out_shape=out_shape,
        scratch_shapes=[qh_scratch, row_scratch, row_scratch, acc_scratch,
                        pltpu.VMEM((N_KV_HEADS, key_block, GROUP * tq), _F32)],
        compiler_params=_cparams(("parallel", "arbitrary")),
        name="attn_online",
    )(q, k, vt)


def _attention_cached(q, ck, cvt, k, vt, score_bound):
    bsz, n, _ = q.shape
    past = ck.shape[1]
    tq = min(Q_TILE, n)
    in_specs, out_spec, out_shape, qh_scratch, acc_scratch = _attention_specs(q, tq, n, n)
    cache_specs = [
        pl.BlockSpec((1, past, KV_W), lambda b, i: (b, 0, 0)),
        pl.BlockSpec((1, VT_ROWS, past), lambda b, i: (b, 0, 0)),
    ]

    def bounded(q, ck, cvt, k, vt):
        return pl.pallas_call(
            functools.partial(_attn_bounded_kernel, n_keys=n, key_block=min(KEY_BLOCK_BOUNDED, n)),
            grid=(bsz, n // tq),
            in_specs=[pl.BlockSpec(memory_space=pltpu.SMEM), in_specs[0]] + cache_specs + in_specs[1:],
            out_specs=out_spec,
            out_shape=out_shape,
            scratch_shapes=[qh_scratch, acc_scratch],
            compiler_params=_cparams(("parallel", "arbitrary")),
            name="attn_bounded",
        )(score_bound.reshape(1, 1), q, ck, cvt, k, vt)

    def online(q, ck, cvt, k, vt):
        return _attention_online(q, jnp.concatenate([ck, k], axis=1), jnp.concatenate([cvt, vt], axis=2),
                                 n, past + n)

    return lax.cond(score_bound < MAX_SCORE_BOUND, bounded, online, q, ck, cvt, k, vt)


def _mix_merge_kernel(ap_ref, prev_ref, next_ref, u_ref, vn_ref, x_ref, h_ref, attn_ref, mod_ref, cw_ref, cb_ref,
                      lg_ref, lb_ref, pw_ref, ps_ref, sw_ref, sb_ref, wg_ref, bg_ref, wb_ref, wo_ref, o_ref,
                      buf_ref, xs_ref, cacc_ref, tmp_ref, br_ref, gate_ref, *, seq_len):
    t = ap_ref.shape[1]
    i = pl.program_id(1)
    last = pl.num_programs(1) - 1
    buf_ref[0:HALO, :] = jnp.where(i > 0, prev_ref[0], 0.0)
    buf_ref[HALO:HALO + t, :] = ap_ref[0]
    buf_ref[HALO + t:HALO + t + HALO, :] = jnp.where(i < last, next_ref[0], 0.0)
    buf_ref[t + 2 * HALO:t + 2 * HALO + SUBLANES, :] = jnp.zeros((SUBLANES, buf_ref.shape[1]), _F32)

    n_sh = t + 2 * HALO - SUBLANES
    for r in range(1, SUBLANES):
        xs_ref[r - 1] = buf_ref[r:r + n_sh, 0:CONV_W]

    n_iter = t // CONV_ROWS
    gate_cols = N_BRANCH * D_MODEL // n_iter
    for ci in range(n_iter):
        base = ci * CONV_ROWS
        acc = jnp.broadcast_to(cb_ref[...], (CONV_ROWS, CONV_W))
        for k in range(CONV_K):
            q8, r = divmod(HALO - CONV_K // 2 + k, SUBLANES)
            rows = pl.ds(base + q8 * SUBLANES, CONV_ROWS)
            x = buf_ref[rows, :CONV_W] if r == 0 else xs_ref[r - 1, rows, :]
            acc = acc + x * cw_ref[k:k + 1, :]
        cacc_ref[pl.ds(base, CONV_ROWS), :] = acc
        g0 = ci * gate_cols
        gate = jax.nn.sigmoid(_dot(h_ref[0], wg_ref[:, g0:g0 + gate_cols]) + bg_ref[:, g0:g0 + gate_cols])
        gate_ref[:, g0:g0 + gate_cols] = gate.astype(_BF16)
    acc = cacc_ref[...]
    mu = jnp.mean(acc, axis=-1, keepdims=True)
    cen = acc - mu
    var = jnp.mean(cen * cen, axis=-1, keepdims=True)
    y = cen * lax.rsqrt(var + EPS) * lg_ref[...] + lb_ref[...]
    br_ref[0] = (y * jax.nn.sigmoid(y)).astype(_BF16)

    pos = i * t + lax.broadcasted_iota(jnp.int32, (t, 1), 0)
    for g, w in enumerate(POOL_WINDOWS):
        cols = slice(CONV_W + g * POOL_GC, CONV_W + (g + 1) * POOL_GC)
        first = HALO - w // 2
        if w == 2:
            s = buf_ref[first:first + t, cols] + buf_ref[first + 1:first + 1 + t, cols]
        else:
            n = t + 2 * HALO
            tmp_ref[0, 0:n, :] = buf_ref[0:n, cols] + buf_ref[1:n + 1, cols]
            span, stage = 2, 0
            while span * 2 < w:
                n -= SUBLANES
                tmp_ref[stage + 1, 0:n, :] = tmp_ref[stage, 0:n, :] + tmp_ref[stage, span:span + n, :]
                span, stage = span * 2, stage + 1
            s = tmp_ref[stage, first:first + t, :] + tmp_ref[stage, first + span:first + span + t, :]
        lo = jnp.maximum(pos - w // 2, 0)
        hi = jnp.minimum(pos - w // 2 + w, seq_len)
        pooled = s / (hi - lo).astype(_F32) - buf_ref[HALO:HALO + t, cols]
        yg = _dot(pooled.astype(_BF16), pw_ref[g])
        br_ref[1, :, g * POOL_GC:(g + 1) * POOL_GC] = (yg * ps_ref[:, g * POOL_GC:(g + 1) * POOL_GC]).astype(_BF16)

    n_chunk = t // SGU_CHUNK
    for g in range(SGU_GROUPS):
        cols = slice(g * SGU_GC, (g + 1) * SGU_GC)
        rhs = jnp.concatenate([vn_ref[0, c * SGU_CHUNK:(c + 1) * SGU_CHUNK, cols] for c in range(n_chunk)], axis=1)
        sv = _dot(sw_ref[g], rhs)
        for c in range(n_chunk):
            rows = slice(c * SGU_CHUNK, (c + 1) * SGU_CHUNK)
            gate = sv[:, c * SGU_GC:(c + 1) * SGU_GC] + sb_ref[:, cols]
            br_ref[2, rows, cols] = (u_ref[0, rows, cols].astype(_F32) * gate).astype(_BF16)

    merged = None
    for idx in range(N_BRANCH):
        branch = attn_ref[0] if idx == 0 else br_ref[idx - 1]
        term = gate_ref[:, idx * D_MODEL:(idx + 1) * D_MODEL].astype(_F32) * _dot(branch, wb_ref[idx])
        merged = term if merged is None else merged + term
    m = _dot(merged.astype(_BF16), wo_ref[...])
    o_ref[0] = x_ref[0] + mod_ref[0, 2:3, :] * m


def _mix_merge(x, h, attn, ap, u, vn, mod, lw, t):
    bsz, n, _ = ap.shape
    hb = t // HALO
    n_hb = n // HALO
    row = lambda b, i: (b, i, 0)
    const2 = lambda b, i: (0, 0)
    const3 = lambda b, i: (0, 0, 0)
    w_all = CONV_W + POOL_W
    mod_map = (lambda b, i: (b, 0, 0)) if mod.shape[0] == bsz else (lambda b, i: (0, 0, 0))
    in_specs = [
        pl.BlockSpec((1, t, w_all), row),
        pl.BlockSpec((1, HALO, w_all), lambda b, i: (b, jnp.maximum(i * hb - 1, 0), 0)),
        pl.BlockSpec((1, HALO, w_all), lambda b, i: (b, jnp.minimum((i + 1) * hb, n_hb - 1), 0)),
        pl.BlockSpec((1, t, SGU_W), row),
        pl.BlockSpec((1, t, SGU_W), row),
        pl.BlockSpec((1, t, D_MODEL), row),
        pl.BlockSpec((1, t, D_MODEL), row),
        pl.BlockSpec((1, t, BRANCH_W), row),
        pl.BlockSpec((1, 6, D_MODEL), mod_map),
        pl.BlockSpec((CONV_K, CONV_W), const2),
        pl.BlockSpec((1, CONV_W), const2),
        pl.BlockSpec((1, CONV_W), const2),
        pl.BlockSpec((1, CONV_W), const2),
        pl.BlockSpec((POOL_GROUPS, POOL_GC, POOL_GC), const3),
        pl.BlockSpec((1, POOL_W), const2),
        pl.BlockSpec((SGU_GROUPS, SGU_CHUNK, SGU_CHUNK), const3),
        pl.BlockSpec((SGU_CHUNK, SGU_W), const2),
        pl.BlockSpec((D_MODEL, N_BRANCH * D_MODEL), const2, pipeline_mode=pl.Buffered(1)),
        pl.BlockSpec((1, N_BRANCH * D_MODEL), const2),
        pl.BlockSpec((N_BRANCH, BRANCH_W, D_MODEL), const3, pipeline_mode=pl.Buffered(1)),
        pl.BlockSpec((D_MODEL, D_MODEL), const2, pipeline_mode=pl.Buffered(1)),
    ]
    return pl.pallas_call(
        functools.partial(_mix_merge_kernel, seq_len=n),
        grid=(bsz, n // t),
        in_specs=in_specs,
        out_specs=pl.BlockSpec((1, t, D_MODEL), row),
        out_shape=jax.ShapeDtypeStruct((bsz, n, D_MODEL), _F32),
        scratch_shapes=[
            pltpu.VMEM((t + 2 * HALO + SUBLANES, w_all), _F32),
            pltpu.VMEM((SUBLANES - 1, t + 2 * HALO - SUBLANES, CONV_W), _F32),
            pltpu.VMEM((t, CONV_W), _F32),
            pltpu.VMEM((3, t + 2 * HALO, POOL_GC), _F32),
            pltpu.VMEM((N_BRANCH - 1, t, BRANCH_W), _BF16),
            pltpu.VMEM((t, N_BRANCH * D_MODEL), _BF16),
        ],
        compiler_params=_cparams(("parallel", "parallel")),
        name="mix_merge",
    )(ap, ap, ap, u, vn, x, h, attn, mod, lw['conv_w'], lw['conv_b'], lw['conv_ln_g'], lw['conv_ln_b'], lw['pool_w'],
      lw['pool_scale'], lw['sgu_w'], lw['sgu_bias'], lw['w_gate'], lw['b_gate'], lw['w_branch'], lw['w_out'])


def _mlp_kernel(x_ref, mod_ref, ng_ref, w1_ref, w2_ref, fg_ref, o_ref, hid_ref, *, final_norm):
    x = x_ref[0]
    xn = x * lax.rsqrt(jnp.mean(x * x, axis=-1, keepdims=True) + EPS) * ng_ref[...]
    hb = (xn * (1.0 + mod_ref[0, 4:5, :]) + mod_ref[0, 3:4, :]).astype(_BF16)
    for c in range(D_FF // D_MODEL):
        cols = slice(c * D_MODEL, (c + 1) * D_MODEL)
        a = jnp.maximum(_dot(hb, w1_ref[:, cols]), 0.0)
        hid_ref[:, cols] = (a * a).astype(_BF16)
    y = x + mod_ref[0, 5:6, :] * _dot(hid_ref[...], w2_ref[...])
    if final_norm:
        y = y * lax.rsqrt(jnp.mean(y * y, axis=-1, keepdims=True) + EPS) * fg_ref[...]
    o_ref[0] = y


def _mlp(x, mod, lw, final_g, final_norm):
    bsz, n, _ = x.shape
    t = TOK_TILE
    row = lambda b, i: (b, i, 0)
    const2 = lambda b, i: (0, 0)
    return pl.pallas_call(
        functools.partial(_mlp_kernel, final_norm=final_norm),
        grid=(bsz, n // t),
        in_specs=[
            pl.BlockSpec((1, t, D_MODEL), row),
            pl.BlockSpec((1, 6, D_MODEL), lambda b, i: (b, 0, 0)),
            pl.BlockSpec((1, D_MODEL), const2),
            pl.BlockSpec((D_MODEL, D_FF), const2),
            pl.BlockSpec((D_FF, D_MODEL), const2),
            pl.BlockSpec((1, D_MODEL), const2),
        ],
        out_specs=pl.BlockSpec((1, t, D_MODEL), row),
        out_shape=jax.ShapeDtypeStruct((bsz, n, D_MODEL), _F32),
        scratch_shapes=[pltpu.VMEM((t, D_FF), _BF16)],
        compiler_params=_cparams(("parallel", "parallel")),
        name="mlp",
    )(x, mod, lw['norm2_g'], lw['w_mlp_in'], lw['w_mlp_out'], final_g)


def _rope_tables(n):
    rows = n // GRID_W
    row = jnp.repeat(jnp.arange(rows, dtype=_F32), GRID_W)
    col = jnp.tile(jnp.arange(GRID_W, dtype=_F32), rows)
    inv = ROPE_THETA ** (-jnp.arange(0, AXIS_DIM, 2, dtype=_F32) / AXIS_DIM)
    ang = jnp.concatenate([row[:, None] * inv, col[:, None] * inv], axis=-1)
    cos = jnp.repeat(jnp.cos(ang), 2, axis=-1)
    sin = jnp.repeat(jnp.sin(ang), 2, axis=-1)
    even = (jnp.arange(HEAD_DIM) % 2 == 0)[None, :]
    se = jnp.where(even, -sin, 0.0)
    so = jnp.where(even, 0.0, sin)
    rep = LANES // HEAD_DIM
    return tuple(jnp.tile(tab, (1, rep)) for tab in (cos, se, so))


def _head_indicator(width):
    head = jnp.arange(width) // HEAD_DIM
    return ((head[:, None] == head[None, :]).astype(_F32) / HEAD_DIM).astype(_BF16)


def _layer_weights(l, w_in, norm1_g, q_norm_g, k_norm_g, conv_w, conv_b, conv_ln_g, conv_ln_b, pool_w,
                   pool_scale, sgu_norm_g, sgu_w, sgu_b, w_branch, w_gate, b_gate, w_out, norm2_g,
                   w_mlp_in, w_mlp_out):
    win = w_in[l].astype(_BF16)
    wq = win[:, :Q_W].reshape(D_MODEL, N_KV_HEADS, GROUP, HEAD_DIM).transpose(0, 2, 1, 3).reshape(D_MODEL, Q_W)
    win = jnp.concatenate([wq, win[:, Q_W:]], axis=1)
    wb = w_branch[l].astype(_BF16)
    wb0 = wb[0].reshape(N_KV_HEADS, GROUP, HEAD_DIM, D_MODEL).transpose(1, 0, 2, 3).reshape(Q_W, D_MODEL)
    wb = jnp.concatenate([wb0[None], wb[1:]], axis=0)
    row = lambda v: v.reshape(1, -1)
    return {
        'norm1_g': row(norm1_g[l]),
        'w_in': win,
        'q_g': row(jnp.tile(q_norm_g[l], N_HEADS)),
        'k_g': row(jnp.tile(k_norm_g[l], N_KV_HEADS)),
        'ind_q': _head_indicator(Q_W),
        'ind_k': _head_indicator(KV_W),
        'sgu_norm_g': row(sgu_norm_g[l]),
        'conv_w': conv_w[l],
        'conv_b': row(conv_b[l]),
        'conv_ln_g': row(conv_ln_g[l]),
        'conv_ln_b': row(conv_ln_b[l]),
        'pool_w': pool_w[l].astype(_BF16),
        'pool_scale': row(pool_scale[l]),
        'sgu_w': sgu_w[l].astype(_BF16),
        'sgu_bias': jnp.repeat(sgu_b[l].T, SGU_GC, axis=1),
        'w_gate': w_gate[l].astype(_BF16),
        'b_gate': row(b_gate[l]),
        'w_branch': wb,
        'w_out': w_out[l].astype(_BF16),
        'norm2_g': row(norm2_g[l]),
        'w_mlp_in': w_mlp_in[l].astype(_BF16),
        'w_mlp_out': w_mlp_out[l].astype(_BF16),
    }


def kernel(x_prompt, x_sample, cache_k, cache_v, c, c_ctx, w_mod, b_mod, norm1_g, w_in, q_norm_g, k_norm_g, conv_w, conv_b, conv_ln_g, conv_ln_b, pool_w, pool_scale, sgu_norm_g, sgu_w, sgu_b, w_branch, w_gate, b_gate, w_out, norm2_g, w_mlp_in, w_mlp_out, final_norm_g):
    batch, seq, _ = x_prompt.shape
    dec_batch, dec_seq, _ = x_sample.shape
    past = cache_k.shape[2]
    n_ctx = batch * seq

    c_all = jnp.concatenate([c, c_ctx[None, :], jnp.zeros((MOD_ROWS - dec_batch - 1, D_MODEL), _F32)], axis=0)
    mod = _modulation(c_all, w_mod, b_mod).reshape(DEPTH, MOD_ROWS, 6, D_MODEL)

    rope_tabs = _rope_tables(dec_seq)
    final_g = final_norm_g.reshape(1, D_MODEL)

    xp = x_prompt.reshape(1, n_ctx, D_MODEL)
    xs = x_sample
    new_k, new_v = [], []
    for l in range(DEPTH):
        lw = _layer_weights(l, w_in, norm1_g, q_norm_g, k_norm_g, conv_w, conv_b, conv_ln_g, conv_ln_b, pool_w,
                            pool_scale, sgu_norm_g, sgu_w, sgu_b, w_branch, w_gate, b_gate, w_out, norm2_g,
                            w_mlp_in, w_mlp_out)
        mod_lat = mod[l, :dec_batch]
        mod_ctx = mod[l, dec_batch:dec_batch + 1]
        last = l == DEPTH - 1

        q, k, vt, ap, u, vn, h, kf, vf = _pre(xp, mod_ctx, lw, None, True)
        new_k.append(kf.reshape(batch, seq, N_KV_HEADS, HEAD_DIM))
        new_v.append(vf.reshape(batch, seq, N_KV_HEADS, HEAD_DIM))
        attn = _attention_online(q, k, vt, seq, seq)
        per_seq = lambda a: a.reshape(batch, seq, a.shape[-1])
        xp = _mix_merge(per_seq(xp), per_seq(h), per_seq(attn), per_seq(ap), per_seq(u), per_seq(vn), mod_ctx, lw, seq)
        xp = _mlp(xp.reshape(1, n_ctx, D_MODEL), mod_ctx, lw, final_g, last)

        q, k, vt, ap, u, vn, h = _pre(xs, mod_lat, lw, rope_tabs, False)
        ck = cache_k[:, l].reshape(dec_batch, past, KV_W).astype(_BF16)
        cvt = jnp.transpose(cache_v[:, l], (0, 2, 3, 1)).astype(_BF16)
        cvt = jnp.concatenate([cvt, jnp.ones((dec_batch, N_KV_HEADS, ONES_ROWS, past), _BF16)], axis=2)
        cvt = cvt.reshape(dec_batch, VT_ROWS, past)
        q_norm = HEAD_DIM ** 0.5 * jnp.max(jnp.abs(q_norm_g[l])) * Q_SCALE
        k_norm = HEAD_DIM ** 0.5 * jnp.max(jnp.abs(k_norm_g[l]))
        ck_sq = jnp.square(ck.astype(_F32)).reshape(dec_batch, past, N_KV_HEADS, HEAD_DIM)
        k_norm = jnp.maximum(k_norm, jnp.sqrt(jnp.max(jnp.sum(ck_sq, axis=-1))))
        attn = _attention_cached(q, ck, cvt, k, vt, BOUND_MARGIN * q_norm * k_norm)
        xs = _mix_merge(xs, h, attn, ap, u, vn, mod_lat, lw, TOK_TILE)
        xs = _mlp(xs, mod_lat, lw, final_g, last)

    y_prompt = xp.reshape(batch, seq, D_MODEL)
    return (y_prompt, xs, jnp.stack(new_k, axis=1), jnp.stack(new_v, axis=1))
```

```python
import functools

import jax
import jax.numpy as jnp
from jax import lax
from jax.experimental import pallas as pl
from jax.experimental.pallas import tpu as pltpu

D_MODEL = 1024
DEPTH = 2
GRID_W = 64
N_HEADS = 8
N_KV_HEADS = 2
HEAD_DIM = 64
Q_W = N_HEADS * HEAD_DIM
KV_W = N_KV_HEADS * HEAD_DIM
GROUP = N_HEADS // N_KV_HEADS
AXIS_DIM = HEAD_DIM // 2
ROPE_THETA = 10000.0
CONV_W = 512
CONV_K = 31
POOL_W = 512
POOL_GROUPS = 4
POOL_GC = POOL_W // POOL_GROUPS
POOL_WINDOWS = (2, 4, 8, 16)
SGU_W = 512
SGU_GROUPS = 4
SGU_GC = SGU_W // SGU_GROUPS
SGU_CHUNK = 128
BRANCH_W = 512
N_BRANCH = 4
D_FF = 4 * D_MODEL
IN_W = Q_W + 2 * KV_W + 2 * CONV_W + POOL_W + 2 * SGU_W
EPS = 1e-6

O_Q = 0
O_KV = Q_W
O_A = Q_W + 2 * KV_W
O_P = O_A + 2 * CONV_W
O_S = O_P + POOL_W

LANES = 128
SUBLANES = 8
CONV_ROWS = 64
HALO = 16
TOK_TILE = 512
PRE_ROWS = 256
Q_TILE = 1024
Q_TILE_ONLINE = 512
KEY_BLOCK = 768
KEY_BLOCK_BOUNDED = 1024
ONES_ROWS = 16
VT_HEAD_ROWS = HEAD_DIM + ONES_ROWS
VT_ROWS = N_KV_HEADS * VT_HEAD_ROWS
MOD_ROWS = 16
MOD_TILE = 1536
VMEM_LIMIT = 56 * 1024 * 1024
Q_SCALE = HEAD_DIM ** -0.5 * 1.4426950408889634
BOUND_MARGIN = 1.02
MAX_SCORE_BOUND = 40.0

_F32 = jnp.float32
_BF16 = jnp.bfloat16


def _dot(a, b):
    return jnp.dot(a, b, preferred_element_type=_F32)


def _cparams(sem, flags=None):
    return pltpu.CompilerParams(dimension_semantics=sem, vmem_limit_bytes=VMEM_LIMIT, flags=flags)


def _mod_kernel(c_ref, w_ref, b_ref, o_ref):
    c = c_ref[...]
    cs = c * jax.nn.sigmoid(c)
    o_ref[0] = _dot(cs.astype(_BF16), w_ref[0].astype(_BF16)) + b_ref[0]


def _modulation(c_all, w_mod, b_mod):
    n_col = 6 * D_MODEL
    return pl.pallas_call(
        _mod_kernel,
        grid=(DEPTH, n_col // MOD_TILE),
        in_specs=[
            pl.BlockSpec((MOD_ROWS, D_MODEL), lambda l, j: (0, 0)),
            pl.BlockSpec((1, D_MODEL, MOD_TILE), lambda l, j: (l, 0, j)),
            pl.BlockSpec((1, 1, MOD_TILE), lambda l, j: (l, 0, j)),
        ],
        out_specs=pl.BlockSpec((1, MOD_ROWS, MOD_TILE), lambda l, j: (l, 0, j)),
        out_shape=jax.ShapeDtypeStruct((DEPTH, MOD_ROWS, n_col), _F32),
        compiler_params=_cparams(("parallel", "parallel")),
        name="modulation",
    )(c_all, w_mod, b_mod.reshape(DEPTH, 1, n_col))


def _rope(x, cos, se, so):
    outs = []
    for c in range(x.shape[1] // LANES):
        xc = x[:, c * LANES:(c + 1) * LANES]
        nxt = pltpu.roll(xc, LANES - 1, 1)
        prv = pltpu.roll(xc, 1, 1)
        outs.append(xc * cos + nxt * se + prv * so)
    return outs[0] if len(outs) == 1 else jnp.concatenate(outs, axis=1)


def _pre_kernel(*refs, rope, emit_kv):
    x_ref, mod_ref, ng_ref, win_ref, qg_ref, kg_ref, indq_ref, indk_ref, sg_ref = refs[:9]
    refs = refs[9:]
    if rope:
        cos_ref, se_ref, so_ref = refs[:3]
        refs = refs[3:]
    q_ref, k_ref, vt_ref, ap_ref, u_ref, vn_ref, h_ref = refs[:7]
    if emit_kv:
        kf_ref, vf_ref = refs[7:9]

    for r0 in range(0, x_ref.shape[1], PRE_ROWS):
        rs = slice(r0, r0 + PRE_ROWS)
        tabs = (cos_ref[rs, :], se_ref[rs, :], so_ref[rs, :]) if rope else None
        x = x_ref[0, rs, :]
        xn = x * lax.rsqrt(jnp.mean(x * x, axis=-1, keepdims=True) + EPS) * ng_ref[...]
        h = xn * (1.0 + mod_ref[0, 1:2, :]) + mod_ref[0, 0:1, :]
        hb = h.astype(_BF16)
        h_ref[0, rs, :] = hb

        zs = jax.nn.gelu(_dot(hb, win_ref[:, O_S:O_S + 2 * SGU_W]))
        u_ref[0, rs, :] = zs[:, :SGU_W].astype(_BF16)
        v = zs[:, SGU_W:]
        vn = v * lax.rsqrt(jnp.mean(v * v, axis=-1, keepdims=True) + EPS) * sg_ref[...]
        vn_ref[0, rs, :] = vn.astype(_BF16)

        za = _dot(hb, win_ref[:, O_A:O_A + 2 * CONV_W])
        ap_ref[0, rs, 0:CONV_W] = za[:, :CONV_W] * jax.nn.sigmoid(za[:, CONV_W:])

        zq = _dot(hb, win_ref[:, O_Q:O_Q + Q_W])
        msq = _dot((zq * zq).astype(_BF16), indq_ref[...])
        qn = zq * lax.rsqrt(msq + EPS) * qg_ref[...]
        if rope:
            qn = _rope(qn, *tabs)
        q_ref[0, rs, :] = (qn * Q_SCALE).astype(_BF16)

        zkv = _dot(hb, win_ref[:, O_KV:O_KV + 2 * KV_W])
        zk = zkv[:, :KV_W]
        zv = zkv[:, KV_W:]
        msk = _dot((zk * zk).astype(_BF16), indk_ref[...])
        kn = zk * lax.rsqrt(msk + EPS) * kg_ref[...]
        if emit_kv:
            kf_ref[0, rs, :] = kn
            vf_ref[0, rs, :] = zv
        if rope:
            kn = _rope(kn, *tabs)
        k_ref[0, rs, :] = kn.astype(_BF16)
        vt = zv.T.astype(_BF16)
        for j in range(N_KV_HEADS):
            vt_ref[0, j * VT_HEAD_ROWS:j * VT_HEAD_ROWS + HEAD_DIM, rs] = vt[j * HEAD_DIM:(j + 1) * HEAD_DIM, :]
            vt_ref[0, j * VT_HEAD_ROWS + HEAD_DIM:(j + 1) * VT_HEAD_ROWS, rs] = jnp.ones((ONES_ROWS, PRE_ROWS), _BF16)

        ap_ref[0, rs, CONV_W:CONV_W + POOL_W] = _dot(hb, win_ref[:, O_P:O_P + POOL_W])


def _pre(x, mod, lw, rope_tabs, emit_kv):
    bsz, n, _ = x.shape
    t = TOK_TILE
    rope = rope_tabs is not None
    row = lambda b, i: (b, i, 0)
    const2 = lambda b, i: (0, 0)
    in_specs = [
        pl.BlockSpec((1, t, D_MODEL), row),
        pl.BlockSpec((1, 6, D_MODEL), lambda b, i: (b, 0, 0)),
        pl.BlockSpec((1, D_MODEL), const2),
        pl.BlockSpec((D_MODEL, IN_W), const2),
        pl.BlockSpec((1, Q_W), const2),
        pl.BlockSpec((1, KV_W), const2),
        pl.BlockSpec((Q_W, Q_W), const2),
        pl.BlockSpec((KV_W, KV_W), const2),
        pl.BlockSpec((1, SGU_W), const2),
    ]
    args = [x, mod, lw['norm1_g'], lw['w_in'], lw['q_g'], lw['k_g'], lw['ind_q'], lw['ind_k'], lw['sgu_norm_g']]
    if rope:
        in_specs += [pl.BlockSpec((t, LANES), lambda b, i: (i, 0))] * 3
        args += list(rope_tabs)
    out_specs = [
        pl.BlockSpec((1, t, Q_W), row),
        pl.BlockSpec((1, t, KV_W), row),
        pl.BlockSpec((1, VT_ROWS, t), lambda b, i: (b, 0, i)),
        pl.BlockSpec((1, t, CONV_W + POOL_W), row),
        pl.BlockSpec((1, t, SGU_W), row),
        pl.BlockSpec((1, t, SGU_W), row),
        pl.BlockSpec((1, t, D_MODEL), row),
    ]
    out_shape = [
        jax.ShapeDtypeStruct((bsz, n, Q_W), _BF16),
        jax.ShapeDtypeStruct((bsz, n, KV_W), _BF16),
        jax.ShapeDtypeStruct((bsz, VT_ROWS, n), _BF16),
        jax.ShapeDtypeStruct((bsz, n, CONV_W + POOL_W), _F32),
        jax.ShapeDtypeStruct((bsz, n, SGU_W), _BF16),
        jax.ShapeDtypeStruct((bsz, n, SGU_W), _BF16),
        jax.ShapeDtypeStruct((bsz, n, D_MODEL), _BF16),
    ]
    if emit_kv:
        out_specs += [pl.BlockSpec((1, t, KV_W), row)] * 2
        out_shape += [jax.ShapeDtypeStruct((bsz, n, KV_W), _F32)] * 2
    return pl.pallas_call(
        functools.partial(_pre_kernel, rope=rope, emit_kv=emit_kv),
        grid=(bsz, n // t),
        in_specs=in_specs,
        out_specs=out_specs,
        out_shape=out_shape,
        compiler_params=_cparams(("parallel", "parallel")),
        name="pre_lat" if rope else "pre_ctx",
    )(*args)


def _split_heads(q_ref, qh_ref):
    tq = q_ref.shape[1]
    hi_half = lax.broadcasted_iota(jnp.int32, (1, LANES), 1) >= HEAD_DIM
    for g in range(GROUP):
        qc = q_ref[0, :, g * LANES:(g + 1) * LANES]
        for j in range(N_KV_HEADS):
            qh_ref[j, g * tq:(g + 1) * tq, :] = jnp.where(hi_half if j == 1 else jnp.logical_not(hi_half), qc,
                                                          jnp.zeros_like(qc))


def _write_heads(acc_ref, o_ref):
    tq = o_ref.shape[1]
    for g in range(GROUP):
        halves = []
        for j in range(N_KV_HEADS):
            acc = acc_ref[j, :, g * tq:(g + 1) * tq]
            halves.append(acc[0:HEAD_DIM, :] / acc[HEAD_DIM:HEAD_DIM + 1, :])
        o_ref[0, :, g * LANES:(g + 1) * LANES] = jnp.concatenate(halves, axis=0).T.astype(_BF16)


_NT = (((1,), (1,)), ((), ()))


def _attn_kernel(q_ref, k_ref, vt_ref, o_ref, qh_ref, m_ref, alpha_ref, acc_ref, s_ref, *, n_keys, key_block):
    _split_heads(q_ref, qh_ref)
    m_ref[...] = jnp.full(m_ref.shape, -1e30, _F32)
    acc_ref[...] = jnp.zeros(acc_ref.shape, _F32)

    def scores(off, j):
        s = lax.dot_general(k_ref[0, pl.ds(off, key_block), :], qh_ref[j], _NT, preferred_element_type=_F32)
        m_old = m_ref[j]
        m_new = jnp.maximum(m_old, jnp.max(s, axis=0, keepdims=True))
        m_ref[j] = m_new
        alpha_ref[j] = jnp.exp2(m_old - m_new)
        s_ref[j] = s

    def values(off, j):
        p = jnp.exp2(s_ref[j] - m_ref[j]).astype(_BF16)
        vblk = vt_ref[0, j * VT_HEAD_ROWS:(j + 1) * VT_HEAD_ROWS, pl.ds(off, key_block)]
        acc_ref[j] = alpha_ref[j] * acc_ref[j] + _dot(vblk, p)

    n_blocks = n_keys // key_block
    unroll = 2 if n_blocks % 2 == 0 else 1
    scores(0, 0)

    def body(b, carry):
        for r in range(unroll):
            off = pl.multiple_of((b * unroll + r) * key_block, key_block)
            scores(off, 1)
            values(off, 0)
            scores(off + key_block, 0)
            values(off, 1)
        return carry

    lax.fori_loop(0, n_blocks // unroll - 1, body, 0)
    for r in range(unroll):
        off = (n_blocks - unroll + r) * key_block
        scores(off, 1)
        values(off, 0)
        if r < unroll - 1:
            scores(off + key_block, 0)
        values(off, 1)
    _write_heads(acc_ref, o_ref)


def _attn_bounded_kernel(bound_ref, q_ref, ck_ref, cvt_ref, k_ref, vt_ref, o_ref, qh_ref, acc_ref, *, n_keys, key_block):
    _split_heads(q_ref, qh_ref)
    shift = bound_ref[0, 0]

    def weighted_values(kblk, vtblk, j):
        s = lax.dot_general(kblk, qh_ref[j], _NT, preferred_element_type=_F32)
        return _dot(vtblk, jnp.exp2(s - shift).astype(_BF16))

    for j in range(N_KV_HEADS):
        acc_ref[j] = weighted_values(ck_ref[0], cvt_ref[0, j * VT_HEAD_ROWS:(j + 1) * VT_HEAD_ROWS, :], j)

    n_blocks = n_keys // key_block
    unroll = 2 if n_blocks % 2 == 0 else 1

    def body(b, carry):
        for r in range(unroll):
            off = pl.multiple_of((b * unroll + r) * key_block, key_block)
            for j in range(N_KV_HEADS):
                vtblk = vt_ref[0, j * VT_HEAD_ROWS:(j + 1) * VT_HEAD_ROWS, pl.ds(off, key_block)]
                acc_ref[j] = acc_ref[j] + weighted_values(k_ref[0, pl.ds(off, key_block), :], vtblk, j)
        return carry

    lax.fori_loop(0, n_blocks // unroll, body, 0)
    _write_heads(acc_ref, o_ref)


def _attention_specs(q, tq, q_run, n_keys):
    per_run = q_run // tq
    qmap = lambda b, i: (b, i, 0)
    in_specs = [
        pl.BlockSpec((1, tq, Q_W), qmap),
        pl.BlockSpec((1, n_keys, KV_W), lambda b, i: (b, i // per_run, 0)),
        pl.BlockSpec((1, VT_ROWS, n_keys), lambda b, i: (b, 0, i // per_run)),
    ]
    out_spec = pl.BlockSpec((1, tq, Q_W), qmap)
    out_shape = jax.ShapeDtypeStruct(q.shape, _BF16)
    qh_scratch = pltpu.VMEM((N_KV_HEADS, GROUP * tq, LANES), _BF16)
    acc_scratch = pltpu.VMEM((N_KV_HEADS, VT_HEAD_ROWS, GROUP * tq), _F32)
    return in_specs, out_spec, out_shape, qh_scratch, acc_scratch


def _attention_online(q, k, vt, q_run, n_keys):
    bsz, n, _ = q.shape
    tq = min(Q_TILE_ONLINE, q_run)
    key_block = min(KEY_BLOCK, n_keys)
    in_specs, out_spec, out_shape, qh_scratch, acc_scratch = _attention_specs(q, tq, q_run, n_keys)
    row_scratch = pltpu.VMEM((N_KV_HEADS, 1, GROUP * tq), _F32)
    return pl.pallas_call(
        functools.partial(_attn_kernel, n_keys=n_keys, key_block=key_block),
        grid=(bsz, n // tq),
        in_specs=in_specs,
        out_specs=out_spec,
        out_shape=out_shape,
        scratch_shapes=[qh_scratch, row_scratch, row_scratch, acc_scratch,
                        pltpu.VMEM((N_KV_HEADS, key_block, GROUP * tq), _F32)],
        compiler_params=_cparams(("parallel", "arbitrary")),
        name="attn_online",
    )(q, k, vt)


def _attention_cached(q, ck, cvt, k, vt, score_bound):
    bsz, n, _ = q.shape
    past = ck.shape[1]
    tq = min(Q_TILE, n)
    in_specs, out_spec, out_shape, qh_scratch, acc_scratch = _attention_specs(q, tq, n, n)
    cache_specs = [
        pl.BlockSpec((1, past, KV_W), lambda b, i: (b, 0, 0)),
        pl.BlockSpec((1, VT_ROWS, past), lambda b, i: (b, 0, 0)),
    ]

    def bounded(q, ck, cvt, k, vt):
        return pl.pallas_call(
            functools.partial(_attn_bounded_kernel, n_keys=n, key_block=min(KEY_BLOCK_BOUNDED, n)),
            grid=(bsz, n // tq),
            in_specs=[pl.BlockSpec(memory_space=pltpu.SMEM), in_specs[0]] + cache_specs + in_specs[1:],
            out_specs=out_spec,
            out_shape=out_shape,
            scratch_shapes=[qh_scratch, acc_scratch],
            compiler_params=_cparams(("parallel", "arbitrary")),
            name="attn_bounded",
        )(score_bound.reshape(1, 1), q, ck, cvt, k, vt)

    def online(q, ck, cvt, k, vt):
        return _attention_online(q, jnp.concatenate([ck, k], axis=1), jnp.concatenate([cvt, vt], axis=2),
                                 n, past + n)

    return lax.cond(score_bound < MAX_SCORE_BOUND, bounded, online, q, ck, cvt, k, vt)


def _mix_merge_kernel(ap_ref, prev_ref, next_ref, u_ref, vn_ref, x_ref, h_ref, attn_ref, mod_ref, cw_ref, cb_ref,
                      lg_ref, lb_ref, pw_ref, ps_ref, sw_ref, sb_ref, wg_ref, bg_ref, wb_ref, wo_ref, o_ref,
                      buf_ref, xs_ref, cacc_ref, tmp_ref, br_ref, gate_ref, *, seq_len):
    t = ap_ref.shape[1]
    i = pl.program_id(1)
    last = pl.num_programs(1) - 1
    buf_ref[0:HALO, :] = jnp.where(i > 0, prev_ref[0], 0.0)
    buf_ref[HALO:HALO + t, :] = ap_ref[0]
    buf_ref[HALO + t:HALO + t + HALO, :] = jnp.where(i < last, next_ref[0], 0.0)
    buf_ref[t + 2 * HALO:t + 2 * HALO + SUBLANES, :] = jnp.zeros((SUBLANES, buf_ref.shape[1]), _F32)

    n_sh = t + 2 * HALO - SUBLANES
    for r in range(1, SUBLANES):
        xs_ref[r - 1] = buf_ref[r:r + n_sh, 0:CONV_W]

    n_iter = t // CONV_ROWS
    gate_cols = N_BRANCH * D_MODEL // n_iter
    for ci in range(n_iter):
        base = ci * CONV_ROWS
        acc = jnp.broadcast_to(cb_ref[...], (CONV_ROWS, CONV_W))
        for k in range(CONV_K):
            q8, r = divmod(HALO - CONV_K // 2 + k, SUBLANES)
            rows = pl.ds(base + q8 * SUBLANES, CONV_ROWS)
            x = buf_ref[rows, :CONV_W] if r == 0 else xs_ref[r - 1, rows, :]
            acc = acc + x * cw_ref[k:k + 1, :]
        cacc_ref[pl.ds(base, CONV_ROWS), :] = acc
        g0 = ci * gate_cols
        gate = jax.nn.sigmoid(_dot(h_ref[0], wg_ref[:, g0:g0 + gate_cols]) + bg_ref[:, g0:g0 + gate_cols])
        gate_ref[:, g0:g0 + gate_cols] = gate.astype(_BF16)
    acc = cacc_ref[...]
    mu = jnp.mean(acc, axis=-1, keepdims=True)
    cen = acc - mu
    var = jnp.mean(cen * cen, axis=-1, keepdims=True)
    y = cen * lax.rsqrt(var + EPS) * lg_ref[...] + lb_ref[...]
    br_ref[0] = (y * jax.nn.sigmoid(y)).astype(_BF16)

    pos = i * t + lax.broadcasted_iota(jnp.int32, (t, 1), 0)
    for g, w in enumerate(POOL_WINDOWS):
        cols = slice(CONV_W + g * POOL_GC, CONV_W + (g + 1) * POOL_GC)
        first = HALO - w // 2
        if w == 2:
            s = buf_ref[first:first + t, cols] + buf_ref[first + 1:first + 1 + t, cols]
        else:
            n = t + 2 * HALO
            tmp_ref[0, 0:n, :] = buf_ref[0:n, cols] + buf_ref[1:n + 1, cols]
            span, stage = 2, 0
            while span * 2 < w:
                n -= SUBLANES
                tmp_ref[stage + 1, 0:n, :] = tmp_ref[stage, 0:n, :] + tmp_ref[stage, span:span + n, :]
                span, stage = span * 2, stage + 1
            s = tmp_ref[stage, first:first + t, :] + tmp_ref[stage, first + span:first + span + t, :]
        lo = jnp.maximum(pos - w // 2, 0)
        hi = jnp.minimum(pos - w // 2 + w, seq_len)
        pooled = s / (hi - lo).astype(_F32) - buf_ref[HALO:HALO + t, cols]
        yg = _dot(pooled.astype(_BF16), pw_ref[g])
        br_ref[1, :, g * POOL_GC:(g + 1) * POOL_GC] = (yg * ps_ref[:, g * POOL_GC:(g + 1) * POOL_GC]).astype(_BF16)

    n_chunk = t // SGU_CHUNK
    for g in range(SGU_GROUPS):
        cols = slice(g * SGU_GC, (g + 1) * SGU_GC)
        rhs = jnp.concatenate([vn_ref[0, c * SGU_CHUNK:(c + 1) * SGU_CHUNK, cols] for c in range(n_chunk)], axis=1)
        sv = _dot(sw_ref[g], rhs)
        for c in range(n_chunk):
            rows = slice(c * SGU_CHUNK, (c + 1) * SGU_CHUNK)
            gate = sv[:, c * SGU_GC:(c + 1) * SGU_GC] + sb_ref[:, cols]
            br_ref[2, rows, cols] = (u_ref[0, rows, cols].astype(_F32) * gate).astype(_BF16)

    merged = None
    for idx in range(N_BRANCH):
        branch = attn_ref[0] if idx == 0 else br_ref[idx - 1]
        term = gate_ref[:, idx * D_MODEL:(idx + 1) * D_MODEL].astype(_F32) * _dot(branch, wb_ref[idx])
        merged = term if merged is None else merged + term
    m = _dot(merged.astype(_BF16), wo_ref[...])
    o_ref[0] = x_ref[0] + mod_ref[0, 2:3, :] * m


def _mix_merge(x, h, attn, ap, u, vn, mod, lw, t):
    bsz, n, _ = ap.shape
    hb = t // HALO
    n_hb = n // HALO
    row = lambda b, i: (b, i, 0)
    const2 = lambda b, i: (0, 0)
    const3 = lambda b, i: (0, 0, 0)
    w_all = CONV_W + POOL_W
    mod_map = (lambda b, i: (b, 0, 0)) if mod.shape[0] == bsz else (lambda b, i: (0, 0, 0))
    in_specs = [
        pl.BlockSpec((1, t, w_all), row),
        pl.BlockSpec((1, HALO, w_all), lambda b, i: (b, jnp.maximum(i * hb - 1, 0), 0)),
        pl.BlockSpec((1, HALO, w_all), lambda b, i: (b, jnp.minimum((i + 1) * hb, n_hb - 1), 0)),
        pl.BlockSpec((1, t, SGU_W), row),
        pl.BlockSpec((1, t, SGU_W), row),
        pl.BlockSpec((1, t, D_MODEL), row),
        pl.BlockSpec((1, t, D_MODEL), row),
        pl.BlockSpec((1, t, BRANCH_W), row),
        pl.BlockSpec((1, 6, D_MODEL), mod_map),
        pl.BlockSpec((CONV_K, CONV_W), const2),
        pl.BlockSpec((1, CONV_W), const2),
        pl.BlockSpec((1, CONV_W), const2),
        pl.BlockSpec((1, CONV_W), const2),
        pl.BlockSpec((POOL_GROUPS, POOL_GC, POOL_GC), const3),
        pl.BlockSpec((1, POOL_W), const2),
        pl.BlockSpec((SGU_GROUPS, SGU_CHUNK, SGU_CHUNK), const3),
        pl.BlockSpec((SGU_CHUNK, SGU_W), const2),
        pl.BlockSpec((D_MODEL, N_BRANCH * D_MODEL), const2, pipeline_mode=pl.Buffered(1)),
        pl.BlockSpec((1, N_BRANCH * D_MODEL), const2),
        pl.BlockSpec((N_BRANCH, BRANCH_W, D_MODEL), const3, pipeline_mode=pl.Buffered(1)),
        pl.BlockSpec((D_MODEL, D_MODEL), const2, pipeline_mode=pl.Buffered(1)),
    ]
    return pl.pallas_call(
        functools.partial(_mix_merge_kernel, seq_len=n),
        grid=(bsz, n // t),
        in_specs=in_specs,
        out_specs=pl.BlockSpec((1, t, D_MODEL), row),
        out_shape=jax.ShapeDtypeStruct((bsz, n, D_MODEL), _F32),
        scratch_shapes=[
            pltpu.VMEM((t + 2 * HALO + SUBLANES, w_all), _F32),
            pltpu.VMEM((SUBLANES - 1, t + 2 * HALO - SUBLANES, CONV_W), _F32),
            pltpu.VMEM((t, CONV_W), _F32),
            pltpu.VMEM((3, t + 2 * HALO, POOL_GC), _F32),
            pltpu.VMEM((N_BRANCH - 1, t, BRANCH_W), _BF16),
            pltpu.VMEM((t, N_BRANCH * D_MODEL), _BF16),
        ],
        compiler_params=_cparams(("parallel", "parallel")),
        name="mix_merge",
    )(ap, ap, ap, u, vn, x, h, attn, mod, lw['conv_w'], lw['conv_b'], lw['conv_ln_g'], lw['conv_ln_b'], lw['pool_w'],
      lw['pool_scale'], lw['sgu_w'], lw['sgu_bias'], lw['w_gate'], lw['b_gate'], lw['w_branch'], lw['w_out'])


def _mlp_kernel(x_ref, mod_ref, ng_ref, w1_ref, w2_ref, fg_ref, o_ref, hid_ref, *, final_norm):
    for r0 in range(0, x_ref.shape[1], PRE_ROWS):
        rs = slice(r0, r0 + PRE_ROWS)
        x = x_ref[0, rs, :]
        xn = x * lax.rsqrt(jnp.mean(x * x, axis=-1, keepdims=True) + EPS) * ng_ref[...]
        hb = (xn * (1.0 + mod_ref[0, 4:5, :]) + mod_ref[0, 3:4, :]).astype(_BF16)
        for c in range(D_FF // D_MODEL):
            cols = slice(c * D_MODEL, (c + 1) * D_MODEL)
            a = jnp.maximum(_dot(hb, w1_ref[:, cols]), 0.0)
            hid_ref[rs, cols] = (a * a).astype(_BF16)
        y = x + mod_ref[0, 5:6, :] * _dot(hid_ref[rs, :], w2_ref[...])
        if final_norm:
            y = y * lax.rsqrt(jnp.mean(y * y, axis=-1, keepdims=True) + EPS) * fg_ref[...]
        o_ref[0, rs, :] = y


def _mlp(x, mod, lw, final_g, final_norm):
    bsz, n, _ = x.shape
    t = TOK_TILE
    row = lambda b, i: (b, i, 0)
    const2 = lambda b, i: (0, 0)
    return pl.pallas_call(
        functools.partial(_mlp_kernel, final_norm=final_norm),
        grid=(bsz, n // t),
        in_specs=[
            pl.BlockSpec((1, t, D_MODEL), row),
            pl.BlockSpec((1, 6, D_MODEL), lambda b, i: (b, 0, 0)),
            pl.BlockSpec((1, D_MODEL), const2),
            pl.BlockSpec((D_MODEL, D_FF), const2),
            pl.BlockSpec((D_FF, D_MODEL), const2),
            pl.BlockSpec((1, D_MODEL), const2),
        ],
        out_specs=pl.BlockSpec((1, t, D_MODEL), row),
        out_shape=jax.ShapeDtypeStruct((bsz, n, D_MODEL), _F32),
        scratch_shapes=[pltpu.VMEM((t, D_FF), _BF16)],
        compiler_params=_cparams(("parallel", "parallel")),
        name="mlp",
    )(x, mod, lw['norm2_g'], lw['w_mlp_in'], lw['w_mlp_out'], final_g)


def _rope_tables(n):
    rows = n // GRID_W
    row = jnp.repeat(jnp.arange(rows, dtype=_F32), GRID_W)
    col = jnp.tile(jnp.arange(GRID_W, dtype=_F32), rows)
    inv = ROPE_THETA ** (-jnp.arange(0, AXIS_DIM, 2, dtype=_F32) / AXIS_DIM)
    ang = jnp.concatenate([row[:, None] * inv, col[:, None] * inv], axis=-1)
    cos = jnp.repeat(jnp.cos(ang), 2, axis=-1)
    sin = jnp.repeat(jnp.sin(ang), 2, axis=-1)
    even = (jnp.arange(HEAD_DIM) % 2 == 0)[None, :]
    se = jnp.where(even, -sin, 0.0)
    so = jnp.where(even, 0.0, sin)
    rep = LANES // HEAD_DIM
    return tuple(jnp.tile(tab, (1, rep)) for tab in (cos, se, so))


def _head_indicator(width):
    head = jnp.arange(width) // HEAD_DIM
    return ((head[:, None] == head[None, :]).astype(_F32) / HEAD_DIM).astype(_BF16)


def _layer_weights(l, w_in, norm1_g, q_norm_g, k_norm_g, conv_w, conv_b, conv_ln_g, conv_ln_b, pool_w,
                   pool_scale, sgu_norm_g, sgu_w, sgu_b, w_branch, w_gate, b_gate, w_out, norm2_g,
                   w_mlp_in, w_mlp_out):
    win = w_in[l].astype(_BF16)
    wq = win[:, :Q_W].reshape(D_MODEL, N_KV_HEADS, GROUP, HEAD_DIM).transpose(0, 2, 1, 3).reshape(D_MODEL, Q_W)
    win = jnp.concatenate([wq, win[:, Q_W:]], axis=1)
    wb = w_branch[l].astype(_BF16)
    wb0 = wb[0].reshape(N_KV_HEADS, GROUP, HEAD_DIM, D_MODEL).transpose(1, 0, 2, 3).reshape(Q_W, D_MODEL)
    wb = jnp.concatenate([wb0[None], wb[1:]], axis=0)
    row = lambda v: v.reshape(1, -1)
    return {
        'norm1_g': row(norm1_g[l]),
        'w_in': win,
        'q_g': row(jnp.tile(q_norm_g[l], N_HEADS)),
        'k_g': row(jnp.tile(k_norm_g[l], N_KV_HEADS)),
        'ind_q': _head_indicator(Q_W),
        'ind_k': _head_indicator(KV_W),
        'sgu_norm_g': row(sgu_norm_g[l]),
        'conv_w': conv_w[l],
        'conv_b': row(conv_b[l]),
        'conv_ln_g': row(conv_ln_g[l]),
        'conv_ln_b': row(conv_ln_b[l]),
        'pool_w': pool_w[l].astype(_BF16),
        'pool_scale': row(pool_scale[l]),
        'sgu_w': sgu_w[l].astype(_BF16),
        'sgu_bias': jnp.repeat(sgu_b[l].T, SGU_GC, axis=1),
        'w_gate': w_gate[l].astype(_BF16),
        'b_gate': row(b_gate[l]),
        'w_branch': wb,
        'w_out': w_out[l].astype(_BF16),
        'norm2_g': row(norm2_g[l]),
        'w_mlp_in': w_mlp_in[l].astype(_BF16),
        'w_mlp_out': w_mlp_out[l].astype(_BF16),
    }


def kernel(x_prompt, x_sample, cache_k, cache_v, c, c_ctx, w_mod, b_mod, norm1_g, w_in, q_norm_g, k_norm_g, conv_w, conv_b, conv_ln_g, conv_ln_b, pool_w, pool_scale, sgu_norm_g, sgu_w, sgu_b, w_branch, w_gate, b_gate, w_out, norm2_g, w_mlp_in, w_mlp_out, final_norm_g):
    batch, seq, _ = x_prompt.shape
    dec_batch, dec_seq, _ = x_sample.shape
    past = cache_k.shape[2]
    n_ctx = batch * seq

    c_all = jnp.concatenate([c, c_ctx[None, :], jnp.zeros((MOD_ROWS - dec_batch - 1, D_MODEL), _F32)], axis=0)
    mod = _modulation(c_all, w_mod, b_mod).reshape(DEPTH, MOD_ROWS, 6, D_MODEL)

    rope_tabs = _rope_tables(dec_seq)
    final_g = final_norm_g.reshape(1, D_MODEL)

    xp = x_prompt.reshape(1, n_ctx, D_MODEL)
    xs = x_sample
    new_k, new_v = [], []
    for l in range(DEPTH):
        lw = _layer_weights(l, w_in, norm1_g, q_norm_g, k_norm_g, conv_w, conv_b, conv_ln_g, conv_ln_b, pool_w,
                            pool_scale, sgu_norm_g, sgu_w, sgu_b, w_branch, w_gate, b_gate, w_out, norm2_g,
                            w_mlp_in, w_mlp_out)
        mod_lat = mod[l, :dec_batch]
        mod_ctx = mod[l, dec_batch:dec_batch + 1]
        last = l == DEPTH - 1

        q, k, vt, ap, u, vn, h, kf, vf = _pre(xp, mod_ctx, lw, None, True)
        new_k.append(kf.reshape(batch, seq, N_KV_HEADS, HEAD_DIM))
        new_v.append(vf.reshape(batch, seq, N_KV_HEADS, HEAD_DIM))
        attn = _attention_online(q, k, vt, seq, seq)
        per_seq = lambda a: a.reshape(batch, seq, a.shape[-1])
        xp = _mix_merge(per_seq(xp), per_seq(h), per_seq(attn), per_seq(ap), per_seq(u), per_seq(vn), mod_ctx, lw, seq)
        xp = _mlp(xp.reshape(1, n_ctx, D_MODEL), mod_ctx, lw, final_g, last)

        q, k, vt, ap, u, vn, h = _pre(xs, mod_lat, lw, rope_tabs, False)
        ck = cache_k[:, l].reshape(dec_batch, past, KV_W).astype(_BF16)
        cvt = jnp.transpose(cache_v[:, l], (0, 2, 3, 1)).astype(_BF16)
        cvt = jnp.concatenate([cvt, jnp.ones((dec_batch, N_KV_HEADS, ONES_ROWS, past), _BF16)], axis=2)
        cvt = cvt.reshape(dec_batch, VT_ROWS, past)
        q_norm = HEAD_DIM ** 0.5 * jnp.max(jnp.abs(q_norm_g[l])) * Q_SCALE
        k_norm = HEAD_DIM ** 0.5 * jnp.max(jnp.abs(k_norm_g[l]))
        ck_sq = jnp.square(ck.astype(_F32)).reshape(dec_batch, past, N_KV_HEADS, HEAD_DIM)
        k_norm = jnp.maximum(k_norm, jnp.sqrt(jnp.max(jnp.sum(ck_sq, axis=-1))))
        attn = _attention_cached(q, ck, cvt, k, vt, BOUND_MARGIN * q_norm * k_norm)
        xs = _mix_merge(xs, h, attn, ap, u, vn, mod_lat, lw, TOK_TILE)
        xs = _mlp(xs, mod_lat, lw, final_g, last)

    y_prompt = xp.reshape(batch, seq, D_MODEL)
    return (y_prompt, xs, jnp.stack(new_k, axis=1), jnp.stack(new_v, axis=1))
```

```python
import functools

import jax
import jax.numpy as jnp
from jax import lax
from jax.experimental import pallas as pl
from jax.experimental.pallas import tpu as pltpu

D_MODEL = 1024
DEPTH = 2
GRID_W = 64
N_HEADS = 8
N_KV_HEADS = 2
HEAD_DIM = 64
Q_W = N_HEADS * HEAD_DIM
KV_W = N_KV_HEADS * HEAD_DIM
GROUP = N_HEADS // N_KV_HEADS
AXIS_DIM = HEAD_DIM // 2
ROPE_THETA = 10000.0
CONV_W = 512
CONV_K = 31
POOL_W = 512
POOL_GROUPS = 4
POOL_GC = POOL_W // POOL_GROUPS
POOL_WINDOWS = (2, 4, 8, 16)
SGU_W = 512
SGU_GROUPS = 4
SGU_GC = SGU_W // SGU_GROUPS
SGU_CHUNK = 128
BRANCH_W = 512
N_BRANCH = 4
D_FF = 4 * D_MODEL
IN_W = Q_W + 2 * KV_W + 2 * CONV_W + POOL_W + 2 * SGU_W
EPS = 1e-6

O_Q = 0
O_KV = Q_W
O_A = Q_W + 2 * KV_W
O_P = O_A + 2 * CONV_W
O_S = O_P + POOL_W

LANES = 128
SUBLANES = 8
CONV_ROWS = 64
HALO = 16
TOK_TILE = 512
MLP_TILE = 1024
PRE_ROWS = 256
Q_TILE = 1024
Q_TILE_ONLINE = 512
KEY_BLOCK = 768
KEY_BLOCK_BOUNDED = 1024
ONES_ROWS = 16
VT_HEAD_ROWS = HEAD_DIM + ONES_ROWS
VT_ROWS = N_KV_HEADS * VT_HEAD_ROWS
MOD_ROWS = 16
MOD_TILE = 1536
VMEM_LIMIT = 56 * 1024 * 1024
Q_SCALE = HEAD_DIM ** -0.5 * 1.4426950408889634
BOUND_MARGIN = 1.02
MAX_SCORE_BOUND = 40.0

_F32 = jnp.float32
_BF16 = jnp.bfloat16


def _dot(a, b):
    return jnp.dot(a, b, preferred_element_type=_F32)


def _cparams(sem, flags=None):
    return pltpu.CompilerParams(dimension_semantics=sem, vmem_limit_bytes=VMEM_LIMIT, flags=flags)


def _mod_kernel(c_ref, w_ref, b_ref, o_ref):
    c = c_ref[...]
    cs = c * jax.nn.sigmoid(c)
    o_ref[0] = _dot(cs.astype(_BF16), w_ref[0].astype(_BF16)) + b_ref[0]


def _modulation(c_all, w_mod, b_mod):
    n_col = 6 * D_MODEL
    return pl.pallas_call(
        _mod_kernel,
        grid=(DEPTH, n_col // MOD_TILE),
        in_specs=[
            pl.BlockSpec((MOD_ROWS, D_MODEL), lambda l, j: (0, 0)),
            pl.BlockSpec((1, D_MODEL, MOD_TILE), lambda l, j: (l, 0, j)),
            pl.BlockSpec((1, 1, MOD_TILE), lambda l, j: (l, 0, j)),
        ],
        out_specs=pl.BlockSpec((1, MOD_ROWS, MOD_TILE), lambda l, j: (l, 0, j)),
        out_shape=jax.ShapeDtypeStruct((DEPTH, MOD_ROWS, n_col), _F32),
        compiler_params=_cparams(("parallel", "parallel")),
        name="modulation",
    )(c_all, w_mod, b_mod.reshape(DEPTH, 1, n_col))


def _rope(x, cos, se, so):
    outs = []
    for c in range(x.shape[1] // LANES):
        xc = x[:, c * LANES:(c + 1) * LANES]
        nxt = pltpu.roll(xc, LANES - 1, 1)
        prv = pltpu.roll(xc, 1, 1)
        outs.append(xc * cos + nxt * se + prv * so)
    return outs[0] if len(outs) == 1 else jnp.concatenate(outs, axis=1)


def _pre_kernel(*refs, rope, emit_kv):
    x_ref, mod_ref, ng_ref, win_ref, qg_ref, kg_ref, indq_ref, indk_ref, sg_ref = refs[:9]
    refs = refs[9:]
    if rope:
        cos_ref, se_ref, so_ref = refs[:3]
        refs = refs[3:]
    q_ref, k_ref, vt_ref, ap_ref, u_ref, vn_ref, h_ref = refs[:7]
    if emit_kv:
        kf_ref, vf_ref = refs[7:9]

    for r0 in range(0, x_ref.shape[1], PRE_ROWS):
        rs = slice(r0, r0 + PRE_ROWS)
        tabs = (cos_ref[rs, :], se_ref[rs, :], so_ref[rs, :]) if rope else None
        x = x_ref[0, rs, :]
        xn = x * lax.rsqrt(jnp.mean(x * x, axis=-1, keepdims=True) + EPS) * ng_ref[...]
        h = xn * (1.0 + mod_ref[0, 1:2, :]) + mod_ref[0, 0:1, :]
        hb = h.astype(_BF16)
        h_ref[0, rs, :] = hb

        zs = jax.nn.gelu(_dot(hb, win_ref[:, O_S:O_S + 2 * SGU_W]))
        u_ref[0, rs, :] = zs[:, :SGU_W].astype(_BF16)
        v = zs[:, SGU_W:]
        vn = v * lax.rsqrt(jnp.mean(v * v, axis=-1, keepdims=True) + EPS) * sg_ref[...]
        vn_ref[0, rs, :] = vn.astype(_BF16)

        za = _dot(hb, win_ref[:, O_A:O_A + 2 * CONV_W])
        ap_ref[0, rs, 0:CONV_W] = za[:, :CONV_W] * jax.nn.sigmoid(za[:, CONV_W:])

        zq = _dot(hb, win_ref[:, O_Q:O_Q + Q_W])
        msq = _dot((zq * zq).astype(_BF16), indq_ref[...])
        qn = zq * lax.rsqrt(msq + EPS) * qg_ref[...]
        if rope:
            qn = _rope(qn, *tabs)
        q_ref[0, rs, :] = (qn * Q_SCALE).astype(_BF16)

        zkv = _dot(hb, win_ref[:, O_KV:O_KV + 2 * KV_W])
        zk = zkv[:, :KV_W]
        zv = zkv[:, KV_W:]
        msk = _dot((zk * zk).astype(_BF16), indk_ref[...])
        kn = zk * lax.rsqrt(msk + EPS) * kg_ref[...]
        if emit_kv:
            kf_ref[0, rs, :] = kn
            vf_ref[0, rs, :] = zv
        if rope:
            kn = _rope(kn, *tabs)
        k_ref[0, rs, :] = kn.astype(_BF16)
        vt = zv.T.astype(_BF16)
        for j in range(N_KV_HEADS):
            vt_ref[0, j * VT_HEAD_ROWS:j * VT_HEAD_ROWS + HEAD_DIM, rs] = vt[j * HEAD_DIM:(j + 1) * HEAD_DIM, :]
            vt_ref[0, j * VT_HEAD_ROWS + HEAD_DIM:(j + 1) * VT_HEAD_ROWS, rs] = jnp.ones((ONES_ROWS, PRE_ROWS), _BF16)

        ap_ref[0, rs, CONV_W:CONV_W + POOL_W] = _dot(hb, win_ref[:, O_P:O_P + POOL_W])


def _pre(x, mod, lw, rope_tabs, emit_kv):
    bsz, n, _ = x.shape
    t = TOK_TILE
    rope = rope_tabs is not None
    row = lambda b, i: (b, i, 0)
    const2 = lambda b, i: (0, 0)
    in_specs = [
        pl.BlockSpec((1, t, D_MODEL), row),
        pl.BlockSpec((1, 6, D_MODEL), lambda b, i: (b, 0, 0)),
        pl.BlockSpec((1, D_MODEL), const2),
        pl.BlockSpec((D_MODEL, IN_W), const2),
        pl.BlockSpec((1, Q_W), const2),
        pl.BlockSpec((1, KV_W), const2),
        pl.BlockSpec((Q_W, Q_W), const2),
        pl.BlockSpec((KV_W, KV_W), const2),
        pl.BlockSpec((1, SGU_W), const2),
    ]
    args = [x, mod, lw['norm1_g'], lw['w_in'], lw['q_g'], lw['k_g'], lw['ind_q'], lw['ind_k'], lw['sgu_norm_g']]
    if rope:
        in_specs += [pl.BlockSpec((t, LANES), lambda b, i: (i, 0))] * 3
        args += list(rope_tabs)
    out_specs = [
        pl.BlockSpec((1, t, Q_W), row),
        pl.BlockSpec((1, t, KV_W), row),
        pl.BlockSpec((1, VT_ROWS, t), lambda b, i: (b, 0, i)),
        pl.BlockSpec((1, t, CONV_W + POOL_W), row),
        pl.BlockSpec((1, t, SGU_W), row),
        pl.BlockSpec((1, t, SGU_W), row),
        pl.BlockSpec((1, t, D_MODEL), row),
    ]
    out_shape = [
        jax.ShapeDtypeStruct((bsz, n, Q_W), _BF16),
        jax.ShapeDtypeStruct((bsz, n, KV_W), _BF16),
        jax.ShapeDtypeStruct((bsz, VT_ROWS, n), _BF16),
        jax.ShapeDtypeStruct((bsz, n, CONV_W + POOL_W), _F32),
        jax.ShapeDtypeStruct((bsz, n, SGU_W), _BF16),
        jax.ShapeDtypeStruct((bsz, n, SGU_W), _BF16),
        jax.ShapeDtypeStruct((bsz, n, D_MODEL), _BF16),
    ]
    if emit_kv:
        out_specs += [pl.BlockSpec((1, t, KV_W), row)] * 2
        out_shape += [jax.ShapeDtypeStruct((bsz, n, KV_W), _F32)] * 2
    return pl.pallas_call(
        functools.partial(_pre_kernel, rope=rope, emit_kv=emit_kv),
        grid=(bsz, n // t),
        in_specs=in_specs,
        out_specs=out_specs,
        out_shape=out_shape,
        compiler_params=_cparams(("parallel", "parallel")),
        name="pre_lat" if rope else "pre_ctx",
    )(*args)


def _split_heads(q_ref, qh_ref):
    tq = q_ref.shape[1]
    hi_half = lax.broadcasted_iota(jnp.int32, (1, LANES), 1) >= HEAD_DIM
    for g in range(GROUP):
        qc = q_ref[0, :, g * LANES:(g + 1) * LANES]
        for j in range(N_KV_HEADS):
            qh_ref[j, g * tq:(g + 1) * tq, :] = jnp.where(hi_half if j == 1 else jnp.logical_not(hi_half), qc,
                                                          jnp.zeros_like(qc))


def _write_heads(acc_ref, o_ref):
    tq = o_ref.shape[1]
    for g in range(GROUP):
        halves = []
        for j in range(N_KV_HEADS):
            acc = acc_ref[j, :, g * tq:(g + 1) * tq]
            halves.append(acc[0:HEAD_DIM, :] / acc[HEAD_DIM:HEAD_DIM + 1, :])
        o_ref[0, :, g * LANES:(g + 1) * LANES] = jnp.concatenate(halves, axis=0).T.astype(_BF16)


_NT = (((1,), (1,)), ((), ()))


def _attn_kernel(q_ref, k_ref, vt_ref, o_ref, qh_ref, m_ref, alpha_ref, acc_ref, s_ref, *, n_keys, key_block):
    _split_heads(q_ref, qh_ref)
    m_ref[...] = jnp.full(m_ref.shape, -1e30, _F32)
    acc_ref[...] = jnp.zeros(acc_ref.shape, _F32)

    def scores(off, j):
        s = lax.dot_general(k_ref[0, pl.ds(off, key_block), :], qh_ref[j], _NT, preferred_element_type=_F32)
        m_old = m_ref[j]
        m_new = jnp.maximum(m_old, jnp.max(s, axis=0, keepdims=True))
        m_ref[j] = m_new
        alpha_ref[j] = jnp.exp2(m_old - m_new)
        s_ref[j] = s

    def values(off, j):
        p = jnp.exp2(s_ref[j] - m_ref[j]).astype(_BF16)
        vblk = vt_ref[0, j * VT_HEAD_ROWS:(j + 1) * VT_HEAD_ROWS, pl.ds(off, key_block)]
        acc_ref[j] = alpha_ref[j] * acc_ref[j] + _dot(vblk, p)

    n_blocks = n_keys // key_block
    unroll = 2 if n_blocks % 2 == 0 else 1
    scores(0, 0)

    def body(b, carry):
        for r in range(unroll):
            off = pl.multiple_of((b * unroll + r) * key_block, key_block)
            scores(off, 1)
            values(off, 0)
            scores(off + key_block, 0)
            values(off, 1)
        return carry

    lax.fori_loop(0, n_blocks // unroll - 1, body, 0)
    for r in range(unroll):
        off = (n_blocks - unroll + r) * key_block
        scores(off, 1)
        values(off, 0)
        if r < unroll - 1:
            scores(off + key_block, 0)
        values(off, 1)
    _write_heads(acc_ref, o_ref)


def _attn_bounded_kernel(bound_ref, q_ref, ck_ref, cvt_ref, k_ref, vt_ref, o_ref, qh_ref, acc_ref, *, n_keys, key_block):
    _split_heads(q_ref, qh_ref)
    shift = bound_ref[0, 0]

    def weighted_values(kblk, vtblk, j):
        s = lax.dot_general(kblk, qh_ref[j], _NT, preferred_element_type=_F32)
        return _dot(vtblk, jnp.exp2(s - shift).astype(_BF16))

    for j in range(N_KV_HEADS):
        acc_ref[j] = weighted_values(ck_ref[0], cvt_ref[0, j * VT_HEAD_ROWS:(j + 1) * VT_HEAD_ROWS, :], j)

    n_blocks = n_keys // key_block
    unroll = 2 if n_blocks % 2 == 0 else 1

    def body(b, carry):
        for r in range(unroll):
            off = pl.multiple_of((b * unroll + r) * key_block, key_block)
            for j in range(N_KV_HEADS):
                vtblk = vt_ref[0, j * VT_HEAD_ROWS:(j + 1) * VT_HEAD_ROWS, pl.ds(off, key_block)]
                acc_ref[j] = acc_ref[j] + weighted_values(k_ref[0, pl.ds(off, key_block), :], vtblk, j)
        return carry

    lax.fori_loop(0, n_blocks // unroll, body, 0)
    _write_heads(acc_ref, o_ref)


def _attention_specs(q, tq, q_run, n_keys):
    per_run = q_run // tq
    qmap = lambda b, i: (b, i, 0)
    in_specs = [
        pl.BlockSpec((1, tq, Q_W), qmap),
        pl.BlockSpec((1, n_keys, KV_W), lambda b, i: (b, i // per_run, 0)),
        pl.BlockSpec((1, VT_ROWS, n_keys), lambda b, i: (b, 0, i // per_run)),
    ]
    out_spec = pl.BlockSpec((1, tq, Q_W), qmap)
    out_shape = jax.ShapeDtypeStruct(q.shape, _BF16)
    qh_scratch = pltpu.VMEM((N_KV_HEADS, GROUP * tq, LANES), _BF16)
    acc_scratch = pltpu.VMEM((N_KV_HEADS, VT_HEAD_ROWS, GROUP * tq), _F32)
    return in_specs, out_spec, out_shape, qh_scratch, acc_scratch


def _attention_online(q, k, vt, q_run, n_keys):
    bsz, n, _ = q.shape
    tq = min(Q_TILE_ONLINE, q_run)
    key_block = min(KEY_BLOCK, n_keys)
    in_specs, out_spec, out_shape, qh_scratch, acc_scratch = _attention_specs(q, tq, q_run, n_keys)
    row_scratch = pltpu.VMEM((N_KV_HEADS, 1, GROUP * tq), _F32)
    return pl.pallas_call(
        functools.partial(_attn_kernel, n_keys=n_keys, key_block=key_block),
        grid=(bsz, n // tq),
        in_specs=in_specs,
        out_specs=out_spec,
        out_shape=out_shape,
        scratch_shapes=[qh_scratch, row_scratch, row_scratch, acc_scratch,
                        pltpu.VMEM((N_KV_HEADS, key_block, GROUP * tq), _F32)],
        compiler_params=_cparams(("parallel", "arbitrary")),
        name="attn_online",
    )(q, k, vt)


def _attention_cached(q, ck, cvt, k, vt, score_bound):
    bsz, n, _ = q.shape
    past = ck.shape[1]
    tq = min(Q_TILE, n)
    in_specs, out_spec, out_shape, qh_scratch, acc_scratch = _attention_specs(q, tq, n, n)
    cache_specs = [
        pl.BlockSpec((1, past, KV_W), lambda b, i: (b, 0, 0)),
        pl.BlockSpec((1, VT_ROWS, past), lambda b, i: (b, 0, 0)),
    ]

    def bounded(q, ck, cvt, k, vt):
        return pl.pallas_call(
            functools.partial(_attn_bounded_kernel, n_keys=n, key_block=min(KEY_BLOCK_BOUNDED, n)),
            grid=(bsz, n // tq),
            in_specs=[pl.BlockSpec(memory_space=pltpu.SMEM), in_specs[0]] + cache_specs + in_specs[1:],
            out_specs=out_spec,
            out_shape=out_shape,
            scratch_shapes=[qh_scratch, acc_scratch],
            compiler_params=_cparams(("parallel", "arbitrary")),
            name="attn_bounded",
        )(score_bound.reshape(1, 1), q, ck, cvt, k, vt)

    def online(q, ck, cvt, k, vt):
        return _attention_online(q, jnp.concatenate([ck, k], axis=1), jnp.concatenate([cvt, vt], axis=2),
                                 n, past + n)

    return lax.cond(score_bound < MAX_SCORE_BOUND, bounded, online, q, ck, cvt, k, vt)


def _mix_merge_kernel(ap_ref, prev_ref, next_ref, u_ref, vn_ref, x_ref, h_ref, attn_ref, mod_ref, cw_ref, cb_ref,
                      lg_ref, lb_ref, pw_ref, ps_ref, sw_ref, sb_ref, wg_ref, bg_ref, wb_ref, wo_ref, o_ref,
                      buf_ref, xs_ref, cacc_ref, tmp_ref, br_ref, gate_ref, *, seq_len):
    t = ap_ref.shape[1]
    i = pl.program_id(1)
    last = pl.num_programs(1) - 1
    buf_ref[0:HALO, :] = jnp.where(i > 0, prev_ref[0], 0.0)
    buf_ref[HALO:HALO + t, :] = ap_ref[0]
    buf_ref[HALO + t:HALO + t + HALO, :] = jnp.where(i < last, next_ref[0], 0.0)
    buf_ref[t + 2 * HALO:t + 2 * HALO + SUBLANES, :] = jnp.zeros((SUBLANES, buf_ref.shape[1]), _F32)

    n_sh = t + 2 * HALO - SUBLANES
    for r in range(1, SUBLANES):
        xs_ref[r - 1] = buf_ref[r:r + n_sh, 0:CONV_W]

    n_iter = t // CONV_ROWS
    gate_cols = N_BRANCH * D_MODEL // n_iter
    for ci in range(n_iter):
        base = ci * CONV_ROWS
        acc = jnp.broadcast_to(cb_ref[...], (CONV_ROWS, CONV_W))
        for k in range(CONV_K):
            q8, r = divmod(HALO - CONV_K // 2 + k, SUBLANES)
            rows = pl.ds(base + q8 * SUBLANES, CONV_ROWS)
            x = buf_ref[rows, :CONV_W] if r == 0 else xs_ref[r - 1, rows, :]
            acc = acc + x * cw_ref[k:k + 1, :]
        cacc_ref[pl.ds(base, CONV_ROWS), :] = acc
        g0 = ci * gate_cols
        gate = jax.nn.sigmoid(_dot(h_ref[0], wg_ref[:, g0:g0 + gate_cols]) + bg_ref[:, g0:g0 + gate_cols])
        gate_ref[:, g0:g0 + gate_cols] = gate.astype(_BF16)
    acc = cacc_ref[...]
    mu = jnp.mean(acc, axis=-1, keepdims=True)
    cen = acc - mu
    var = jnp.mean(cen * cen, axis=-1, keepdims=True)
    y = cen * lax.rsqrt(var + EPS) * lg_ref[...] + lb_ref[...]
    br_ref[0] = (y * jax.nn.sigmoid(y)).astype(_BF16)

    pos = i * t + lax.broadcasted_iota(jnp.int32, (t, 1), 0)
    for g, w in enumerate(POOL_WINDOWS):
        cols = slice(CONV_W + g * POOL_GC, CONV_W + (g + 1) * POOL_GC)
        first = HALO - w // 2
        if w == 2:
            s = buf_ref[first:first + t, cols] + buf_ref[first + 1:first + 1 + t, cols]
        else:
            n = t + 2 * HALO
            tmp_ref[0, 0:n, :] = buf_ref[0:n, cols] + buf_ref[1:n + 1, cols]
            span, stage = 2, 0
            while span * 2 < w:
                n -= SUBLANES
                tmp_ref[stage + 1, 0:n, :] = tmp_ref[stage, 0:n, :] + tmp_ref[stage, span:span + n, :]
                span, stage = span * 2, stage + 1
            s = tmp_ref[stage, first:first + t, :] + tmp_ref[stage, first + span:first + span + t, :]
        lo = jnp.maximum(pos - w // 2, 0)
        hi = jnp.minimum(pos - w // 2 + w, seq_len)
        pooled = s / (hi - lo).astype(_F32) - buf_ref[HALO:HALO + t, cols]
        yg = _dot(pooled.astype(_BF16), pw_ref[g])
        br_ref[1, :, g * POOL_GC:(g + 1) * POOL_GC] = (yg * ps_ref[:, g * POOL_GC:(g + 1) * POOL_GC]).astype(_BF16)

    n_chunk = t // SGU_CHUNK
    for g in range(SGU_GROUPS):
        cols = slice(g * SGU_GC, (g + 1) * SGU_GC)
        rhs = jnp.concatenate([vn_ref[0, c * SGU_CHUNK:(c + 1) * SGU_CHUNK, cols] for c in range(n_chunk)], axis=1)
        sv = _dot(sw_ref[g], rhs)
        for c in range(n_chunk):
            rows = slice(c * SGU_CHUNK, (c + 1) * SGU_CHUNK)
            gate = sv[:, c * SGU_GC:(c + 1) * SGU_GC] + sb_ref[:, cols]
            br_ref[2, rows, cols] = (u_ref[0, rows, cols].astype(_F32) * gate).astype(_BF16)

    merged = None
    for idx in range(N_BRANCH):
        branch = attn_ref[0] if idx == 0 else br_ref[idx - 1]
        term = gate_ref[:, idx * D_MODEL:(idx + 1) * D_MODEL].astype(_F32) * _dot(branch, wb_ref[idx])
        merged = term if merged is None else merged + term
    m = _dot(merged.astype(_BF16), wo_ref[...])
    o_ref[0] = x_ref[0] + mod_ref[0, 2:3, :] * m


def _mix_merge(x, h, attn, ap, u, vn, mod, lw, t):
    bsz, n, _ = ap.shape
    hb = t // HALO
    n_hb = n // HALO
    row = lambda b, i: (b, i, 0)
    const2 = lambda b, i: (0, 0)
    const3 = lambda b, i: (0, 0, 0)
    w_all = CONV_W + POOL_W
    mod_map = (lambda b, i: (b, 0, 0)) if mod.shape[0] == bsz else (lambda b, i: (0, 0, 0))
    in_specs = [
        pl.BlockSpec((1, t, w_all), row),
        pl.BlockSpec((1, HALO, w_all), lambda b, i: (b, jnp.maximum(i * hb - 1, 0), 0)),
        pl.BlockSpec((1, HALO, w_all), lambda b, i: (b, jnp.minimum((i + 1) * hb, n_hb - 1), 0)),
        pl.BlockSpec((1, t, SGU_W), row),
        pl.BlockSpec((1, t, SGU_W), row),
        pl.BlockSpec((1, t, D_MODEL), row),
        pl.BlockSpec((1, t, D_MODEL), row),
        pl.BlockSpec((1, t, BRANCH_W), row),
        pl.BlockSpec((1, 6, D_MODEL), mod_map),
        pl.BlockSpec((CONV_K, CONV_W), const2),
        pl.BlockSpec((1, CONV_W), const2),
        pl.BlockSpec((1, CONV_W), const2),
        pl.BlockSpec((1, CONV_W), const2),
        pl.BlockSpec((POOL_GROUPS, POOL_GC, POOL_GC), const3),
        pl.BlockSpec((1, POOL_W), const2),
        pl.BlockSpec((SGU_GROUPS, SGU_CHUNK, SGU_CHUNK), const3),
        pl.BlockSpec((SGU_CHUNK, SGU_W), const2),
        pl.BlockSpec((D_MODEL, N_BRANCH * D_MODEL), const2, pipeline_mode=pl.Buffered(1)),
        pl.BlockSpec((1, N_BRANCH * D_MODEL), const2),
        pl.BlockSpec((N_BRANCH, BRANCH_W, D_MODEL), const3, pipeline_mode=pl.Buffered(1)),
        pl.BlockSpec((D_MODEL, D_MODEL), const2, pipeline_mode=pl.Buffered(1)),
    ]
    return pl.pallas_call(
        functools.partial(_mix_merge_kernel, seq_len=n),
        grid=(bsz, n // t),
        in_specs=in_specs,
        out_specs=pl.BlockSpec((1, t, D_MODEL), row),
        out_shape=jax.ShapeDtypeStruct((bsz, n, D_MODEL), _F32),
        scratch_shapes=[
            pltpu.VMEM((t + 2 * HALO + SUBLANES, w_all), _F32),
            pltpu.VMEM((SUBLANES - 1, t + 2 * HALO - SUBLANES, CONV_W), _F32),
            pltpu.VMEM((t, CONV_W), _F32),
            pltpu.VMEM((3, t + 2 * HALO, POOL_GC), _F32),
            pltpu.VMEM((N_BRANCH - 1, t, BRANCH_W), _BF16),
            pltpu.VMEM((t, N_BRANCH * D_MODEL), _BF16),
        ],
        compiler_params=_cparams(("parallel", "parallel")),
        name="mix_merge",
    )(ap, ap, ap, u, vn, x, h, attn, mod, lw['conv_w'], lw['conv_b'], lw['conv_ln_g'], lw['conv_ln_b'], lw['pool_w'],
      lw['pool_scale'], lw['sgu_w'], lw['sgu_bias'], lw['w_gate'], lw['b_gate'], lw['w_branch'], lw['w_out'])


def _mlp_kernel(x_ref, mod_ref, ng_ref, w1_ref, w2_ref, fg_ref, o_ref, hid_ref, *, final_norm):
    x = x_ref[0]
    xn = x * lax.rsqrt(jnp.mean(x * x, axis=-1, keepdims=True) + EPS) * ng_ref[...]
    hb = (xn * (1.0 + mod_ref[0, 4:5, :]) + mod_ref[0, 3:4, :]).astype(_BF16)
    for c in range(D_FF // D_MODEL):
        cols = slice(c * D_MODEL, (c + 1) * D_MODEL)
        a = jnp.maximum(_dot(hb, w1_ref[:, cols]), 0.0)
        hid_ref[:, cols] = (a * a).astype(_BF16)
    y = x + mod_ref[0, 5:6, :] * _dot(hid_ref[...], w2_ref[...])
    if final_norm:
        y = y * lax.rsqrt(jnp.mean(y * y, axis=-1, keepdims=True) + EPS) * fg_ref[...]
    o_ref[0] = y


def _mlp(x, mod, lw, final_g, final_norm):
    bsz, n, _ = x.shape
    t = MLP_TILE
    row = lambda b, i: (b, i, 0)
    const2 = lambda b, i: (0, 0)
    return pl.pallas_call(
        functools.partial(_mlp_kernel, final_norm=final_norm),
        grid=(bsz, n // t),
        in_specs=[
            pl.BlockSpec((1, t, D_MODEL), row),
            pl.BlockSpec((1, 6, D_MODEL), lambda b, i: (b, 0, 0)),
            pl.BlockSpec((1, D_MODEL), const2),
            pl.BlockSpec((D_MODEL, D_FF), const2, pipeline_mode=pl.Buffered(1)),
            pl.BlockSpec((D_FF, D_MODEL), const2, pipeline_mode=pl.Buffered(1)),
            pl.BlockSpec((1, D_MODEL), const2),
        ],
        out_specs=pl.BlockSpec((1, t, D_MODEL), row),
        out_shape=jax.ShapeDtypeStruct((bsz, n, D_MODEL), _F32),
        scratch_shapes=[pltpu.VMEM((t, D_FF), _BF16)],
        compiler_params=_cparams(("parallel", "parallel")),
        name="mlp",
    )(x, mod, lw['norm2_g'], lw['w_mlp_in'], lw['w_mlp_out'], final_g)


def _rope_tables(n):
    rows = n // GRID_W
    row = jnp.repeat(jnp.arange(rows, dtype=_F32), GRID_W)
    col = jnp.tile(jnp.arange(GRID_W, dtype=_F32), rows)
    inv = ROPE_THETA ** (-jnp.arange(0, AXIS_DIM, 2, dtype=_F32) / AXIS_DIM)
    ang = jnp.concatenate([row[:, None] * inv, col[:, None] * inv], axis=-1)
    cos = jnp.repeat(jnp.cos(ang), 2, axis=-1)
    sin = jnp.repeat(jnp.sin(ang), 2, axis=-1)
    even = (jnp.arange(HEAD_DIM) % 2 == 0)[None, :]
    se = jnp.where(even, -sin, 0.0)
    so = jnp.where(even, 0.0, sin)
    rep = LANES // HEAD_DIM
    return tuple(jnp.tile(tab, (1, rep)) for tab in (cos, se, so))


def _head_indicator(width):
    head = jnp.arange(width) // HEAD_DIM
    return ((head[:, None] == head[None, :]).astype(_F32) / HEAD_DIM).astype(_BF16)


def _layer_weights(l, w_in, norm1_g, q_norm_g, k_norm_g, conv_w, conv_b, conv_ln_g, conv_ln_b, pool_w,
                   pool_scale, sgu_norm_g, sgu_w, sgu_b, w_branch, w_gate, b_gate, w_out, norm2_g,
                   w_mlp_in, w_mlp_out):
    win = w_in[l].astype(_BF16)
    wq = win[:, :Q_W].reshape(D_MODEL, N_KV_HEADS, GROUP, HEAD_DIM).transpose(0, 2, 1, 3).reshape(D_MODEL, Q_W)
    win = jnp.concatenate([wq, win[:, Q_W:]], axis=1)
    wb = w_branch[l].astype(_BF16)
    wb0 = wb[0].reshape(N_KV_HEADS, GROUP, HEAD_DIM, D_MODEL).transpose(1, 0, 2, 3).reshape(Q_W, D_MODEL)
    wb = jnp.concatenate([wb0[None], wb[1:]], axis=0)
    row = lambda v: v.reshape(1, -1)
    return {
        'norm1_g': row(norm1_g[l]),
        'w_in': win,
        'q_g': row(jnp.tile(q_norm_g[l], N_HEADS)),
        'k_g': row(jnp.tile(k_norm_g[l], N_KV_HEADS)),
        'ind_q': _head_indicator(Q_W),
        'ind_k': _head_indicator(KV_W),
        'sgu_norm_g': row(sgu_norm_g[l]),
        'conv_w': conv_w[l],
        'conv_b': row(conv_b[l]),
        'conv_ln_g': row(conv_ln_g[l]),
        'conv_ln_b': row(conv_ln_b[l]),
        'pool_w': pool_w[l].astype(_BF16),
        'pool_scale': row(pool_scale[l]),
        'sgu_w': sgu_w[l].astype(_BF16),
        'sgu_bias': jnp.repeat(sgu_b[l].T, SGU_GC, axis=1),
        'w_gate': w_gate[l].astype(_BF16),
        'b_gate': row(b_gate[l]),
        'w_branch': wb,
        'w_out': w_out[l].astype(_BF16),
        'norm2_g': row(norm2_g[l]),
        'w_mlp_in': w_mlp_in[l].astype(_BF16),
        'w_mlp_out': w_mlp_out[l].astype(_BF16),
    }


def kernel(x_prompt, x_sample, cache_k, cache_v, c, c_ctx, w_mod, b_mod, norm1_g, w_in, q_norm_g, k_norm_g, conv_w, conv_b, conv_ln_g, conv_ln_b, pool_w, pool_scale, sgu_norm_g, sgu_w, sgu_b, w_branch, w_gate, b_gate, w_out, norm2_g, w_mlp_in, w_mlp_out, final_norm_g):
    batch, seq, _ = x_prompt.shape
    dec_batch, dec_seq, _ = x_sample.shape
    past = cache_k.shape[2]
    n_ctx = batch * seq

    c_all = jnp.concatenate([c, c_ctx[None, :], jnp.zeros((MOD_ROWS - dec_batch - 1, D_MODEL), _F32)], axis=0)
    mod = _modulation(c_all, w_mod, b_mod).reshape(DEPTH, MOD_ROWS, 6, D_MODEL)

    rope_tabs = _rope_tables(dec_seq)
    final_g = final_norm_g.reshape(1, D_MODEL)

    xp = x_prompt.reshape(1, n_ctx, D_MODEL)
    xs = x_sample
    new_k, new_v = [], []
    for l in range(DEPTH):
        lw = _layer_weights(l, w_in, norm1_g, q_norm_g, k_norm_g, conv_w, conv_b, conv_ln_g, conv_ln_b, pool_w,
                            pool_scale, sgu_norm_g, sgu_w, sgu_b, w_branch, w_gate, b_gate, w_out, norm2_g,
                            w_mlp_in, w_mlp_out)
        mod_lat = mod[l, :dec_batch]
        mod_ctx = mod[l, dec_batch:dec_batch + 1]
        last = l == DEPTH - 1

        q, k, vt, ap, u, vn, h, kf, vf = _pre(xp, mod_ctx, lw, None, True)
        new_k.append(kf.reshape(batch, seq, N_KV_HEADS, HEAD_DIM))
        new_v.append(vf.reshape(batch, seq, N_KV_HEADS, HEAD_DIM))
        attn = _attention_online(q, k, vt, seq, seq)
        per_seq = lambda a: a.reshape(batch, seq, a.shape[-1])
        xp = _mix_merge(per_seq(xp), per_seq(h), per_seq(attn), per_seq(ap), per_seq(u), per_seq(vn), mod_ctx, lw, seq)
        xp = _mlp(xp.reshape(1, n_ctx, D_MODEL), mod_ctx, lw, final_g, last)

        q, k, vt, ap, u, vn, h = _pre(xs, mod_lat, lw, rope_tabs, False)
        ck = cache_k[:, l].reshape(dec_batch, past, KV_W).astype(_BF16)
        cvt = jnp.transpose(cache_v[:, l], (0, 2, 3, 1)).astype(_BF16)
        cvt = jnp.concatenate([cvt, jnp.ones((dec_batch, N_KV_HEADS, ONES_ROWS, past), _BF16)], axis=2)
        cvt = cvt.reshape(dec_batch, VT_ROWS, past)
        q_norm = HEAD_DIM ** 0.5 * jnp.max(jnp.abs(q_norm_g[l])) * Q_SCALE
        k_norm = HEAD_DIM ** 0.5 * jnp.max(jnp.abs(k_norm_g[l]))
        ck_sq = jnp.square(ck.astype(_F32)).reshape(dec_batch, past, N_KV_HEADS, HEAD_DIM)
        k_norm = jnp.maximum(k_norm, jnp.sqrt(jnp.max(jnp.sum(ck_sq, axis=-1))))
        attn = _attention_cached(q, ck, cvt, k, vt, BOUND_MARGIN * q_norm * k_norm)
        xs = _mix_merge(xs, h, attn, ap, u, vn, mod_lat, lw, TOK_TILE)
        xs = _mlp(xs, mod_lat, lw, final_g, last)

    y_prompt = xp.reshape(batch, seq, D_MODEL)
    return (y_prompt, xs, jnp.stack(new_k, axis=1), jnp.stack(new_v, axis=1))
```

```python
import functools

import jax
import jax.numpy as jnp
from jax import lax
from jax.experimental import pallas as pl
from jax.experimental.pallas import tpu as pltpu

D_MODEL = 1024
DEPTH = 2
GRID_W = 64
N_HEADS = 8
N_KV_HEADS = 2
HEAD_DIM = 64
Q_W = N_HEADS * HEAD_DIM
KV_W = N_KV_HEADS * HEAD_DIM
GROUP = N_HEADS // N_KV_HEADS
AXIS_DIM = HEAD_DIM // 2
ROPE_THETA = 10000.0
CONV_W = 512
CONV_K = 31
POOL_W = 512
POOL_GROUPS = 4
POOL_GC = POOL_W // POOL_GROUPS
POOL_WINDOWS = (2, 4, 8, 16)
SGU_W = 512
SGU_GROUPS = 4
SGU_GC = SGU_W // SGU_GROUPS
SGU_CHUNK = 128
BRANCH_W = 512
N_BRANCH = 4
D_FF = 4 * D_MODEL
IN_W = Q_W + 2 * KV_W + 2 * CONV_W + POOL_W + 2 * SGU_W
EPS = 1e-6

O_Q = 0
O_KV = Q_W
O_A = Q_W + 2 * KV_W
O_P = O_A + 2 * CONV_W
O_S = O_P + POOL_W

LANES = 128
SUBLANES = 8
CONV_ROWS = 64
HALO = 16
TOK_TILE = 512
PRE_TILE = 1024
MLP_TILE = 1024
PRE_ROWS = 256
Q_TILE = 1024
Q_TILE_ONLINE = 512
KEY_BLOCK = 768
KEY_BLOCK_BOUNDED = 1024
ONES_ROWS = 16
VT_HEAD_ROWS = HEAD_DIM + ONES_ROWS
VT_ROWS = N_KV_HEADS * VT_HEAD_ROWS
MOD_ROWS = 16
MOD_TILE = 1536
VMEM_LIMIT = 56 * 1024 * 1024
Q_SCALE = HEAD_DIM ** -0.5 * 1.4426950408889634
BOUND_MARGIN = 1.02
MAX_SCORE_BOUND = 40.0

_F32 = jnp.float32
_BF16 = jnp.bfloat16


def _dot(a, b):
    return jnp.dot(a, b, preferred_element_type=_F32)


def _cparams(sem, flags=None):
    return pltpu.CompilerParams(dimension_semantics=sem, vmem_limit_bytes=VMEM_LIMIT, flags=flags)


def _mod_kernel(c_ref, w_ref, b_ref, o_ref):
    c = c_ref[...]
    cs = c * jax.nn.sigmoid(c)
    o_ref[0] = _dot(cs.astype(_BF16), w_ref[0].astype(_BF16)) + b_ref[0]


def _modulation(c_all, w_mod, b_mod):
    n_col = 6 * D_MODEL
    return pl.pallas_call(
        _mod_kernel,
        grid=(DEPTH, n_col // MOD_TILE),
        in_specs=[
            pl.BlockSpec((MOD_ROWS, D_MODEL), lambda l, j: (0, 0)),
            pl.BlockSpec((1, D_MODEL, MOD_TILE), lambda l, j: (l, 0, j)),
            pl.BlockSpec((1, 1, MOD_TILE), lambda l, j: (l, 0, j)),
        ],
        out_specs=pl.BlockSpec((1, MOD_ROWS, MOD_TILE), lambda l, j: (l, 0, j)),
        out_shape=jax.ShapeDtypeStruct((DEPTH, MOD_ROWS, n_col), _F32),
        compiler_params=_cparams(("parallel", "parallel")),
        name="modulation",
    )(c_all, w_mod, b_mod.reshape(DEPTH, 1, n_col))


def _rope(x, cos, se, so):
    outs = []
    for c in range(x.shape[1] // LANES):
        xc = x[:, c * LANES:(c + 1) * LANES]
        nxt = pltpu.roll(xc, LANES - 1, 1)
        prv = pltpu.roll(xc, 1, 1)
        outs.append(xc * cos + nxt * se + prv * so)
    return outs[0] if len(outs) == 1 else jnp.concatenate(outs, axis=1)


def _pre_kernel(*refs, rope, emit_kv):
    x_ref, mod_ref, ng_ref, win_ref, qg_ref, kg_ref, indq_ref, indk_ref, sg_ref = refs[:9]
    refs = refs[9:]
    if rope:
        cos_ref, se_ref, so_ref = refs[:3]
        refs = refs[3:]
    q_ref, k_ref, vt_ref, ap_ref, u_ref, vn_ref, h_ref = refs[:7]
    if emit_kv:
        kf_ref, vf_ref = refs[7:9]

    for r0 in range(0, x_ref.shape[1], PRE_ROWS):
        rs = slice(r0, r0 + PRE_ROWS)
        tabs = (cos_ref[rs, :], se_ref[rs, :], so_ref[rs, :]) if rope else None
        x = x_ref[0, rs, :]
        xn = x * lax.rsqrt(jnp.mean(x * x, axis=-1, keepdims=True) + EPS) * ng_ref[...]
        h = xn * (1.0 + mod_ref[0, 1:2, :]) + mod_ref[0, 0:1, :]
        hb = h.astype(_BF16)
        h_ref[0, rs, :] = hb

        zs = jax.nn.gelu(_dot(hb, win_ref[:, O_S:O_S + 2 * SGU_W]))
        u_ref[0, rs, :] = zs[:, :SGU_W].astype(_BF16)
        v = zs[:, SGU_W:]
        vn = v * lax.rsqrt(jnp.mean(v * v, axis=-1, keepdims=True) + EPS) * sg_ref[...]
        vn_ref[0, rs, :] = vn.astype(_BF16)

        za = _dot(hb, win_ref[:, O_A:O_A + 2 * CONV_W])
        ap_ref[0, rs, 0:CONV_W] = za[:, :CONV_W] * jax.nn.sigmoid(za[:, CONV_W:])

        zq = _dot(hb, win_ref[:, O_Q:O_Q + Q_W])
        msq = _dot((zq * zq).astype(_BF16), indq_ref[...])
        qn = zq * lax.rsqrt(msq + EPS) * qg_ref[...]
        if rope:
            qn = _rope(qn, *tabs)
        q_ref[0, rs, :] = (qn * Q_SCALE).astype(_BF16)

        zkv = _dot(hb, win_ref[:, O_KV:O_KV + 2 * KV_W])
        zk = zkv[:, :KV_W]
        zv = zkv[:, KV_W:]
        msk = _dot((zk * zk).astype(_BF16), indk_ref[...])
        kn = zk * lax.rsqrt(msk + EPS) * kg_ref[...]
        if emit_kv:
            kf_ref[0, rs, :] = kn
            vf_ref[0, rs, :] = zv
        if rope:
            kn = _rope(kn, *tabs)
        k_ref[0, rs, :] = kn.astype(_BF16)
        vt = zv.T.astype(_BF16)
        for j in range(N_KV_HEADS):
            vt_ref[0, j * VT_HEAD_ROWS:j * VT_HEAD_ROWS + HEAD_DIM, rs] = vt[j * HEAD_DIM:(j + 1) * HEAD_DIM, :]
            vt_ref[0, j * VT_HEAD_ROWS + HEAD_DIM:(j + 1) * VT_HEAD_ROWS, rs] = jnp.ones((ONES_ROWS, PRE_ROWS), _BF16)

        ap_ref[0, rs, CONV_W:CONV_W + POOL_W] = _dot(hb, win_ref[:, O_P:O_P + POOL_W])


def _pre(x, mod, lw, rope_tabs, emit_kv):
    bsz, n, _ = x.shape
    t = PRE_TILE
    rope = rope_tabs is not None
    row = lambda b, i: (b, i, 0)
    const2 = lambda b, i: (0, 0)
    in_specs = [
        pl.BlockSpec((1, t, D_MODEL), row),
        pl.BlockSpec((1, 6, D_MODEL), lambda b, i: (b, 0, 0)),
        pl.BlockSpec((1, D_MODEL), const2),
        pl.BlockSpec((D_MODEL, IN_W), const2, pipeline_mode=pl.Buffered(1)),
        pl.BlockSpec((1, Q_W), const2),
        pl.BlockSpec((1, KV_W), const2),
        pl.BlockSpec((Q_W, Q_W), const2),
        pl.BlockSpec((KV_W, KV_W), const2),
        pl.BlockSpec((1, SGU_W), const2),
    ]
    args = [x, mod, lw['norm1_g'], lw['w_in'], lw['q_g'], lw['k_g'], lw['ind_q'], lw['ind_k'], lw['sgu_norm_g']]
    if rope:
        in_specs += [pl.BlockSpec((t, LANES), lambda b, i: (i, 0))] * 3
        args += list(rope_tabs)
    out_specs = [
        pl.BlockSpec((1, t, Q_W), row),
        pl.BlockSpec((1, t, KV_W), row),
        pl.BlockSpec((1, VT_ROWS, t), lambda b, i: (b, 0, i)),
        pl.BlockSpec((1, t, CONV_W + POOL_W), row),
        pl.BlockSpec((1, t, SGU_W), row),
        pl.BlockSpec((1, t, SGU_W), row),
        pl.BlockSpec((1, t, D_MODEL), row),
    ]
    out_shape = [
        jax.ShapeDtypeStruct((bsz, n, Q_W), _BF16),
        jax.ShapeDtypeStruct((bsz, n, KV_W), _BF16),
        jax.ShapeDtypeStruct((bsz, VT_ROWS, n), _BF16),
        jax.ShapeDtypeStruct((bsz, n, CONV_W + POOL_W), _F32),
        jax.ShapeDtypeStruct((bsz, n, SGU_W), _BF16),
        jax.ShapeDtypeStruct((bsz, n, SGU_W), _BF16),
        jax.ShapeDtypeStruct((bsz, n, D_MODEL), _BF16),
    ]
    if emit_kv:
        out_specs += [pl.BlockSpec((1, t, KV_W), row)] * 2
        out_shape += [jax.ShapeDtypeStruct((bsz, n, KV_W), _F32)] * 2
    return pl.pallas_call(
        functools.partial(_pre_kernel, rope=rope, emit_kv=emit_kv),
        grid=(bsz, n // t),
        in_specs=in_specs,
        out_specs=out_specs,
        out_shape=out_shape,
        compiler_params=_cparams(("parallel", "parallel")),
        name="pre_lat" if rope else "pre_ctx",
    )(*args)


def _split_heads(q_ref, qh_ref):
    tq = q_ref.shape[1]
    hi_half = lax.broadcasted_iota(jnp.int32, (1, LANES), 1) >= HEAD_DIM
    for g in range(GROUP):
        qc = q_ref[0, :, g * LANES:(g + 1) * LANES]
        for j in range(N_KV_HEADS):
            qh_ref[j, g * tq:(g + 1) * tq, :] = jnp.where(hi_half if j == 1 else jnp.logical_not(hi_half), qc,
                                                          jnp.zeros_like(qc))


def _write_heads(acc_ref, o_ref):
    tq = o_ref.shape[1]
    for g in range(GROUP):
        halves = []
        for j in range(N_KV_HEADS):
            acc = acc_ref[j, :, g * tq:(g + 1) * tq]
            halves.append(acc[0:HEAD_DIM, :] / acc[HEAD_DIM:HEAD_DIM + 1, :])
        o_ref[0, :, g * LANES:(g + 1) * LANES] = jnp.concatenate(halves, axis=0).T.astype(_BF16)


_NT = (((1,), (1,)), ((), ()))


def _attn_kernel(q_ref, k_ref, vt_ref, o_ref, qh_ref, m_ref, alpha_ref, acc_ref, s_ref, *, n_keys, key_block):
    _split_heads(q_ref, qh_ref)
    m_ref[...] = jnp.full(m_ref.shape, -1e30, _F32)
    acc_ref[...] = jnp.zeros(acc_ref.shape, _F32)

    def scores(off, j):
        s = lax.dot_general(k_ref[0, pl.ds(off, key_block), :], qh_ref[j], _NT, preferred_element_type=_F32)
        m_old = m_ref[j]
        m_new = jnp.maximum(m_old, jnp.max(s, axis=0, keepdims=True))
        m_ref[j] = m_new
        alpha_ref[j] = jnp.exp2(m_old - m_new)
        s_ref[j] = s

    def values(off, j):
        p = jnp.exp2(s_ref[j] - m_ref[j]).astype(_BF16)
        vblk = vt_ref[0, j * VT_HEAD_ROWS:(j + 1) * VT_HEAD_ROWS, pl.ds(off, key_block)]
        acc_ref[j] = alpha_ref[j] * acc_ref[j] + _dot(vblk, p)

    n_blocks = n_keys // key_block
    unroll = 2 if n_blocks % 2 == 0 else 1
    scores(0, 0)

    def body(b, carry):
        for r in range(unroll):
            off = pl.multiple_of((b * unroll + r) * key_block, key_block)
            scores(off, 1)
            values(off, 0)
            scores(off + key_block, 0)
            values(off, 1)
        return carry

    lax.fori_loop(0, n_blocks // unroll - 1, body, 0)
    for r in range(unroll):
        off = (n_blocks - unroll + r) * key_block
        scores(off, 1)
        values(off, 0)
        if r < unroll - 1:
            scores(off + key_block, 0)
        values(off, 1)
    _write_heads(acc_ref, o_ref)


def _attn_bounded_kernel(bound_ref, q_ref, ck_ref, cvt_ref, k_ref, vt_ref, o_ref, qh_ref, acc_ref, *, n_keys, key_block):
    _split_heads(q_ref, qh_ref)
    shift = bound_ref[0, 0]

    def weighted_values(kblk, vtblk, j):
        s = lax.dot_general(kblk, qh_ref[j], _NT, preferred_element_type=_F32)
        return _dot(vtblk, jnp.exp2(s - shift).astype(_BF16))

    for j in range(N_KV_HEADS):
        acc_ref[j] = weighted_values(ck_ref[0], cvt_ref[0, j * VT_HEAD_ROWS:(j + 1) * VT_HEAD_ROWS, :], j)

    n_blocks = n_keys // key_block
    unroll = 2 if n_blocks % 2 == 0 else 1

    def body(b, carry):
        for r in range(unroll):
            off = pl.multiple_of((b * unroll + r) * key_block, key_block)
            for j in range(N_KV_HEADS):
                vtblk = vt_ref[0, j * VT_HEAD_ROWS:(j + 1) * VT_HEAD_ROWS, pl.ds(off, key_block)]
                acc_ref[j] = acc_ref[j] + weighted_values(k_ref[0, pl.ds(off, key_block), :], vtblk, j)
        return carry

    lax.fori_loop(0, n_blocks // unroll, body, 0)
    _write_heads(acc_ref, o_ref)


def _attention_specs(q, tq, q_run, n_keys):
    per_run = q_run // tq
    qmap = lambda b, i: (b, i, 0)
    in_specs = [
        pl.BlockSpec((1, tq, Q_W), qmap),
        pl.BlockSpec((1, n_keys, KV_W), lambda b, i: (b, i // per_run, 0)),
        pl.BlockSpec((1, VT_ROWS, n_keys), lambda b, i: (b, 0, i // per_run)),
    ]
    out_spec = pl.BlockSpec((1, tq, Q_W), qmap)
    out_shape = jax.ShapeDtypeStruct(q.shape, _BF16)
    qh_scratch = pltpu.VMEM((N_KV_HEADS, GROUP * tq, LANES), _BF16)
    acc_scratch = pltpu.VMEM((N_KV_HEADS, VT_HEAD_ROWS, GROUP * tq), _F32)
    return in_specs, out_spec, out_shape, qh_scratch, acc_scratch


def _attention_online(q, k, vt, q_run, n_keys):
    bsz, n, _ = q.shape
    tq = min(Q_TILE_ONLINE, q_run)
    key_block = min(KEY_BLOCK, n_keys)
    in_specs, out_spec, out_shape, qh_scratch, acc_scratch = _attention_specs(q, tq, q_run, n_keys)
    row_scratch = pltpu.VMEM((N_KV_HEADS, 1, GROUP * tq), _F32)
    return pl.pallas_call(
        functools.partial(_attn_kernel, n_keys=n_keys, key_block=key_block),
        grid=(bsz, n // tq),
        in_specs=in_specs,
        out_specs=out_spec,
        out_shape=out_shape,
        scratch_shapes=[qh_scratch, row_scratch, row_scratch, acc_scratch,
                        pltpu.VMEM((N_KV_HEADS, key_block, GROUP * tq), _F32)],
        compiler_params=_cparams(("parallel", "arbitrary")),
        name="attn_online",
    )(q, k, vt)


def _attention_cached(q, ck, cvt, k, vt, score_bound):
    bsz, n, _ = q.shape
    past = ck.shape[1]
    tq = min(Q_TILE, n)
    in_specs, out_spec, out_shape, qh_scratch, acc_scratch = _attention_specs(q, tq, n, n)
    cache_specs = [
        pl.BlockSpec((1, past, KV_W), lambda b, i: (b, 0, 0)),
        pl.BlockSpec((1, VT_ROWS, past), lambda b, i: (b, 0, 0)),
    ]

    def bounded(q, ck, cvt, k, vt):
        return pl.pallas_call(
            functools.partial(_attn_bounded_kernel, n_keys=n, key_block=min(KEY_BLOCK_BOUNDED, n)),
            grid=(bsz, n // tq),
            in_specs=[pl.BlockSpec(memory_space=pltpu.SMEM), in_specs[0]] + cache_specs + in_specs[1:],
            out_specs=out_spec,
            out_shape=out_shape,
            scratch_shapes=[qh_scratch, acc_scratch],
            compiler_params=_cparams(("parallel", "arbitrary")),
            name="attn_bounded",
        )(score_bound.reshape(1, 1), q, ck, cvt, k, vt)

    def online(q, ck, cvt, k, vt):
        return _attention_online(q, jnp.concatenate([ck, k], axis=1), jnp.concatenate([cvt, vt], axis=2),
                                 n, past + n)

    return lax.cond(score_bound < MAX_SCORE_BOUND, bounded, online, q, ck, cvt, k, vt)


def _mix_merge_kernel(ap_ref, prev_ref, next_ref, u_ref, vn_ref, x_ref, h_ref, attn_ref, mod_ref, cw_ref, cb_ref,
                      lg_ref, lb_ref, pw_ref, ps_ref, sw_ref, sb_ref, wg_ref, bg_ref, wb_ref, wo_ref, o_ref,
                      buf_ref, xs_ref, cacc_ref, tmp_ref, br_ref, gate_ref, *, seq_len):
    t = ap_ref.shape[1]
    i = pl.program_id(1)
    last = pl.num_programs(1) - 1
    buf_ref[0:HALO, :] = jnp.where(i > 0, prev_ref[0], 0.0)
    buf_ref[HALO:HALO + t, :] = ap_ref[0]
    buf_ref[HALO + t:HALO + t + HALO, :] = jnp.where(i < last, next_ref[0], 0.0)
    buf_ref[t + 2 * HALO:t + 2 * HALO + SUBLANES, :] = jnp.zeros((SUBLANES, buf_ref.shape[1]), _F32)

    n_sh = t + 2 * HALO - SUBLANES
    for r in range(1, SUBLANES):
        xs_ref[r - 1] = buf_ref[r:r + n_sh, 0:CONV_W]

    n_iter = t // CONV_ROWS
    gate_cols = N_BRANCH * D_MODEL // n_iter
    for ci in range(n_iter):
        base = ci * CONV_ROWS
        acc = jnp.broadcast_to(cb_ref[...], (CONV_ROWS, CONV_W))
        for k in range(CONV_K):
            q8, r = divmod(HALO - CONV_K // 2 + k, SUBLANES)
            rows = pl.ds(base + q8 * SUBLANES, CONV_ROWS)
            x = buf_ref[rows, :CONV_W] if r == 0 else xs_ref[r - 1, rows, :]
            acc = acc + x * cw_ref[k:k + 1, :]
        cacc_ref[pl.ds(base, CONV_ROWS), :] = acc
        g0 = ci * gate_cols
        gate = jax.nn.sigmoid(_dot(h_ref[0], wg_ref[:, g0:g0 + gate_cols]) + bg_ref[:, g0:g0 + gate_cols])
        gate_ref[:, g0:g0 + gate_cols] = gate.astype(_BF16)
    acc = cacc_ref[...]
    mu = jnp.mean(acc, axis=-1, keepdims=True)
    cen = acc - mu
    var = jnp.mean(cen * cen, axis=-1, keepdims=True)
    y = cen * lax.rsqrt(var + EPS) * lg_ref[...] + lb_ref[...]
    br_ref[0] = (y * jax.nn.sigmoid(y)).astype(_BF16)

    pos = i * t + lax.broadcasted_iota(jnp.int32, (t, 1), 0)
    for g, w in enumerate(POOL_WINDOWS):
        cols = slice(CONV_W + g * POOL_GC, CONV_W + (g + 1) * POOL_GC)
        first = HALO - w // 2
        if w == 2:
            s = buf_ref[first:first + t, cols] + buf_ref[first + 1:first + 1 + t, cols]
        else:
            n = t + 2 * HALO
            tmp_ref[0, 0:n, :] = buf_ref[0:n, cols] + buf_ref[1:n + 1, cols]
            span, stage = 2, 0
            while span * 2 < w:
                n -= SUBLANES
                tmp_ref[stage + 1, 0:n, :] = tmp_ref[stage, 0:n, :] + tmp_ref[stage, span:span + n, :]
                span, stage = span * 2, stage + 1
            s = tmp_ref[stage, first:first + t, :] + tmp_ref[stage, first + span:first + span + t, :]
        lo = jnp.maximum(pos - w // 2, 0)
        hi = jnp.minimum(pos - w // 2 + w, seq_len)
        pooled = s / (hi - lo).astype(_F32) - buf_ref[HALO:HALO + t, cols]
        yg = _dot(pooled.astype(_BF16), pw_ref[g])
        br_ref[1, :, g * POOL_GC:(g + 1) * POOL_GC] = (yg * ps_ref[:, g * POOL_GC:(g + 1) * POOL_GC]).astype(_BF16)

    n_chunk = t // SGU_CHUNK
    for g in range(SGU_GROUPS):
        cols = slice(g * SGU_GC, (g + 1) * SGU_GC)
        rhs = jnp.concatenate([vn_ref[0, c * SGU_CHUNK:(c + 1) * SGU_CHUNK, cols] for c in range(n_chunk)], axis=1)
        sv = _dot(sw_ref[g], rhs)
        for c in range(n_chunk):
            rows = slice(c * SGU_CHUNK, (c + 1) * SGU_CHUNK)
            gate = sv[:, c * SGU_GC:(c + 1) * SGU_GC] + sb_ref[:, cols]
            br_ref[2, rows, cols] = (u_ref[0, rows, cols].astype(_F32) * gate).astype(_BF16)

    merged = None
    for idx in range(N_BRANCH):
        branch = attn_ref[0] if idx == 0 else br_ref[idx - 1]
        term = gate_ref[:, idx * D_MODEL:(idx + 1) * D_MODEL].astype(_F32) * _dot(branch, wb_ref[idx])
        merged = term if merged is None else merged + term
    m = _dot(merged.astype(_BF16), wo_ref[...])
    o_ref[0] = x_ref[0] + mod_ref[0, 2:3, :] * m


def _mix_merge(x, h, attn, ap, u, vn, mod, lw, t):
    bsz, n, _ = ap.shape
    hb = t // HALO
    n_hb = n // HALO
    row = lambda b, i: (b, i, 0)
    const2 = lambda b, i: (0, 0)
    const3 = lambda b, i: (0, 0, 0)
    w_all = CONV_W + POOL_W
    mod_map = (lambda b, i: (b, 0, 0)) if mod.shape[0] == bsz else (lambda b, i: (0, 0, 0))
    in_specs = [
        pl.BlockSpec((1, t, w_all), row),
        pl.BlockSpec((1, HALO, w_all), lambda b, i: (b, jnp.maximum(i * hb - 1, 0), 0)),
        pl.BlockSpec((1, HALO, w_all), lambda b, i: (b, jnp.minimum((i + 1) * hb, n_hb - 1), 0)),
        pl.BlockSpec((1, t, SGU_W), row),
        pl.BlockSpec((1, t, SGU_W), row),
        pl.BlockSpec((1, t, D_MODEL), row),
        pl.BlockSpec((1, t, D_MODEL), row),
        pl.BlockSpec((1, t, BRANCH_W), row),
        pl.BlockSpec((1, 6, D_MODEL), mod_map),
        pl.BlockSpec((CONV_K, CONV_W), const2),
        pl.BlockSpec((1, CONV_W), const2),
        pl.BlockSpec((1, CONV_W), const2),
        pl.BlockSpec((1, CONV_W), const2),
        pl.BlockSpec((POOL_GROUPS, POOL_GC, POOL_GC), const3),
        pl.BlockSpec((1, POOL_W), const2),
        pl.BlockSpec((SGU_GROUPS, SGU_CHUNK, SGU_CHUNK), const3),
        pl.BlockSpec((SGU_CHUNK, SGU_W), const2),
        pl.BlockSpec((D_MODEL, N_BRANCH * D_MODEL), const2, pipeline_mode=pl.Buffered(1)),
        pl.BlockSpec((1, N_BRANCH * D_MODEL), const2),
        pl.BlockSpec((N_BRANCH, BRANCH_W, D_MODEL), const3, pipeline_mode=pl.Buffered(1)),
        pl.BlockSpec((D_MODEL, D_MODEL), const2, pipeline_mode=pl.Buffered(1)),
    ]
    return pl.pallas_call(
        functools.partial(_mix_merge_kernel, seq_len=n),
        grid=(bsz, n // t),
        in_specs=in_specs,
        out_specs=pl.BlockSpec((1, t, D_MODEL), row),
        out_shape=jax.ShapeDtypeStruct((bsz, n, D_MODEL), _F32),
        scratch_shapes=[
            pltpu.VMEM((t + 2 * HALO + SUBLANES, w_all), _F32),
            pltpu.VMEM((SUBLANES - 1, t + 2 * HALO - SUBLANES, CONV_W), _F32),
            pltpu.VMEM((t, CONV_W), _F32),
            pltpu.VMEM((3, t + 2 * HALO, POOL_GC), _F32),
            pltpu.VMEM((N_BRANCH - 1, t, BRANCH_W), _BF16),
            pltpu.VMEM((t, N_BRANCH * D_MODEL), _BF16),
        ],
        compiler_params=_cparams(("parallel", "parallel")),
        name="mix_merge",
    )(ap, ap, ap, u, vn, x, h, attn, mod, lw['conv_w'], lw['conv_b'], lw['conv_ln_g'], lw['conv_ln_b'], lw['pool_w'],
      lw['pool_scale'], lw['sgu_w'], lw['sgu_bias'], lw['w_gate'], lw['b_gate'], lw['w_branch'], lw['w_out'])


def _mlp_kernel(x_ref, mod_ref, ng_ref, w1_ref, w2_ref, fg_ref, o_ref, hid_ref, *, final_norm):
    x = x_ref[0]
    xn = x * lax.rsqrt(jnp.mean(x * x, axis=-1, keepdims=True) + EPS) * ng_ref[...]
    hb = (xn * (1.0 + mod_ref[0, 4:5, :]) + mod_ref[0, 3:4, :]).astype(_BF16)
    for c in range(D_FF // D_MODEL):
        cols = slice(c * D_MODEL, (c + 1) * D_MODEL)
        a = jnp.maximum(_dot(hb, w1_ref[:, cols]), 0.0)
        hid_ref[:, cols] = (a * a).astype(_BF16)
    y = x + mod_ref[0, 5:6, :] * _dot(hid_ref[...], w2_ref[...])
    if final_norm:
        y = y * lax.rsqrt(jnp.mean(y * y, axis=-1, keepdims=True) + EPS) * fg_ref[...]
    o_ref[0] = y


def _mlp(x, mod, lw, final_g, final_norm):
    bsz, n, _ = x.shape
    t = MLP_TILE
    row = lambda b, i: (b, i, 0)
    const2 = lambda b, i: (0, 0)
    return pl.pallas_call(
        functools.partial(_mlp_kernel, final_norm=final_norm),
        grid=(bsz, n // t),
        in_specs=[
            pl.BlockSpec((1, t, D_MODEL), row),
            pl.BlockSpec((1, 6, D_MODEL), lambda b, i: (b, 0, 0)),
            pl.BlockSpec((1, D_MODEL), const2),
            pl.BlockSpec((D_MODEL, D_FF), const2, pipeline_mode=pl.Buffered(1)),
            pl.BlockSpec((D_FF, D_MODEL), const2, pipeline_mode=pl.Buffered(1)),
            pl.BlockSpec((1, D_MODEL), const2),
        ],
        out_specs=pl.BlockSpec((1, t, D_MODEL), row),
        out_shape=jax.ShapeDtypeStruct((bsz, n, D_MODEL), _F32),
        scratch_shapes=[pltpu.VMEM((t, D_FF), _BF16)],
        compiler_params=_cparams(("parallel", "parallel")),
        name="mlp",
    )(x, mod, lw['norm2_g'], lw['w_mlp_in'], lw['w_mlp_out'], final_g)


def _rope_tables(n):
    rows = n // GRID_W
    row = jnp.repeat(jnp.arange(rows, dtype=_F32), GRID_W)
    col = jnp.tile(jnp.arange(GRID_W, dtype=_F32), rows)
    inv = ROPE_THETA ** (-jnp.arange(0, AXIS_DIM, 2, dtype=_F32) / AXIS_DIM)
    ang = jnp.concatenate([row[:, None] * inv, col[:, None] * inv], axis=-1)
    cos = jnp.repeat(jnp.cos(ang), 2, axis=-1)
    sin = jnp.repeat(jnp.sin(ang), 2, axis=-1)
    even = (jnp.arange(HEAD_DIM) % 2 == 0)[None, :]
    se = jnp.where(even, -sin, 0.0)
    so = jnp.where(even, 0.0, sin)
    rep = LANES // HEAD_DIM
    return tuple(jnp.tile(tab, (1, rep)) for tab in (cos, se, so))


def _head_indicator(width):
    head = jnp.arange(width) // HEAD_DIM
    return ((head[:, None] == head[None, :]).astype(_F32) / HEAD_DIM).astype(_BF16)


def _layer_weights(l, w_in, norm1_g, q_norm_g, k_norm_g, conv_w, conv_b, conv_ln_g, conv_ln_b, pool_w,
                   pool_scale, sgu_norm_g, sgu_w, sgu_b, w_branch, w_gate, b_gate, w_out, norm2_g,
                   w_mlp_in, w_mlp_out):
    win = w_in[l].astype(_BF16)
    wq = win[:, :Q_W].reshape(D_MODEL, N_KV_HEADS, GROUP, HEAD_DIM).transpose(0, 2, 1, 3).reshape(D_MODEL, Q_W)
    win = jnp.concatenate([wq, win[:, Q_W:]], axis=1)
    wb = w_branch[l].astype(_BF16)
    wb0 = wb[0].reshape(N_KV_HEADS, GROUP, HEAD_DIM, D_MODEL).transpose(1, 0, 2, 3).reshape(Q_W, D_MODEL)
    wb = jnp.concatenate([wb0[None], wb[1:]], axis=0)
    row = lambda v: v.reshape(1, -1)
    return {
        'norm1_g': row(norm1_g[l]),
        'w_in': win,
        'q_g': row(jnp.tile(q_norm_g[l], N_HEADS)),
        'k_g': row(jnp.tile(k_norm_g[l], N_KV_HEADS)),
        'ind_q': _head_indicator(Q_W),
        'ind_k': _head_indicator(KV_W),
        'sgu_norm_g': row(sgu_norm_g[l]),
        'conv_w': conv_w[l],
        'conv_b': row(conv_b[l]),
        'conv_ln_g': row(conv_ln_g[l]),
        'conv_ln_b': row(conv_ln_b[l]),
        'pool_w': pool_w[l].astype(_BF16),
        'pool_scale': row(pool_scale[l]),
        'sgu_w': sgu_w[l].astype(_BF16),
        'sgu_bias': jnp.repeat(sgu_b[l].T, SGU_GC, axis=1),
        'w_gate': w_gate[l].astype(_BF16),
        'b_gate': row(b_gate[l]),
        'w_branch': wb,
        'w_out': w_out[l].astype(_BF16),
        'norm2_g': row(norm2_g[l]),
        'w_mlp_in': w_mlp_in[l].astype(_BF16),
        'w_mlp_out': w_mlp_out[l].astype(_BF16),
    }


def kernel(x_prompt, x_sample, cache_k, cache_v, c, c_ctx, w_mod, b_mod, norm1_g, w_in, q_norm_g, k_norm_g, conv_w, conv_b, conv_ln_g, conv_ln_b, pool_w, pool_scale, sgu_norm_g, sgu_w, sgu_b, w_branch, w_gate, b_gate, w_out, norm2_g, w_mlp_in, w_mlp_out, final_norm_g):
    batch, seq, _ = x_prompt.shape
    dec_batch, dec_seq, _ = x_sample.shape
    past = cache_k.shape[2]
    n_ctx = batch * seq

    c_all = jnp.concatenate([c, c_ctx[None, :], jnp.zeros((MOD_ROWS - dec_batch - 1, D_MODEL), _F32)], axis=0)
    mod = _modulation(c_all, w_mod, b_mod).reshape(DEPTH, MOD_ROWS, 6, D_MODEL)

    rope_tabs = _rope_tables(dec_seq)
    final_g = final_norm_g.reshape(1, D_MODEL)

    xp = x_prompt.reshape(1, n_ctx, D_MODEL)
    xs = x_sample
    new_k, new_v = [], []
    for l in range(DEPTH):
        lw = _layer_weights(l, w_in, norm1_g, q_norm_g, k_norm_g, conv_w, conv_b, conv_ln_g, conv_ln_b, pool_w,
                            pool_scale, sgu_norm_g, sgu_w, sgu_b, w_branch, w_gate, b_gate, w_out, norm2_g,
                            w_mlp_in, w_mlp_out)
        mod_lat = mod[l, :dec_batch]
        mod_ctx = mod[l, dec_batch:dec_batch + 1]
        last = l == DEPTH - 1

        q, k, vt, ap, u, vn, h, kf, vf = _pre(xp, mod_ctx, lw, None, True)
        new_k.append(kf.reshape(batch, seq, N_KV_HEADS, HEAD_DIM))
        new_v.append(vf.reshape(batch, seq, N_KV_HEADS, HEAD_DIM))
        attn = _attention_online(q, k, vt, seq, seq)
        per_seq = lambda a: a.reshape(batch, seq, a.shape[-1])
        xp = _mix_merge(per_seq(xp), per_seq(h), per_seq(attn), per_seq(ap), per_seq(u), per_seq(vn), mod_ctx, lw, seq)
        xp = _mlp(xp.reshape(1, n_ctx, D_MODEL), mod_ctx, lw, final_g, last)

        q, k, vt, ap, u, vn, h = _pre(xs, mod_lat, lw, rope_tabs, False)
        ck = cache_k[:, l].reshape(dec_batch, past, KV_W).astype(_BF16)
        cvt = jnp.transpose(cache_v[:, l], (0, 2, 3, 1)).astype(_BF16)
        cvt = jnp.concatenate([cvt, jnp.ones((dec_batch, N_KV_HEADS, ONES_ROWS, past), _BF16)], axis=2)
        cvt = cvt.reshape(dec_batch, VT_ROWS, past)
        q_norm = HEAD_DIM ** 0.5 * jnp.max(jnp.abs(q_norm_g[l])) * Q_SCALE
        k_norm = HEAD_DIM ** 0.5 * jnp.max(jnp.abs(k_norm_g[l]))
        ck_sq = jnp.square(ck.astype(_F32)).reshape(dec_batch, past, N_KV_HEADS, HEAD_DIM)
        k_norm = jnp.maximum(k_norm, jnp.sqrt(jnp.max(jnp.sum(ck_sq, axis=-1))))
        attn = _attention_cached(q, ck, cvt, k, vt, BOUND_MARGIN * q_norm * k_norm)
        xs = _mix_merge(xs, h, attn, ap, u, vn, mod_lat, lw, TOK_TILE)
        xs = _mlp(xs, mod_lat, lw, final_g, last)

    y_prompt = xp.reshape(batch, seq, D_MODEL)
    return (y_prompt, xs, jnp.stack(new_k, axis=1), jnp.stack(new_v, axis=1))
```

```python
import functools

import jax
import jax.numpy as jnp
from jax import lax
from jax.experimental import pallas as pl
from jax.experimental.pallas import tpu as pltpu

D_MODEL = 1024
DEPTH = 2
GRID_W = 64
N_HEADS = 8
N_KV_HEADS = 2
HEAD_DIM = 64
Q_W = N_HEADS * HEAD_DIM
KV_W = N_KV_HEADS * HEAD_DIM
GROUP = N_HEADS // N_KV_HEADS
AXIS_DIM = HEAD_DIM // 2
ROPE_THETA = 10000.0
CONV_W = 512
CONV_K = 31
POOL_W = 512
POOL_GROUPS = 4
POOL_GC = POOL_W // POOL_GROUPS
POOL_WINDOWS = (2, 4, 8, 16)
SGU_W = 512
SGU_GROUPS = 4
SGU_GC = SGU_W // SGU_GROUPS
SGU_CHUNK = 128
BRANCH_W = 512
N_BRANCH = 4
D_FF = 4 * D_MODEL
IN_W = Q_W + 2 * KV_W + 2 * CONV_W + POOL_W + 2 * SGU_W
EPS = 1e-6

O_Q = 0
O_KV = Q_W
O_A = Q_W + 2 * KV_W
O_P = O_A + 2 * CONV_W
O_S = O_P + POOL_W

LANES = 128
SUBLANES = 8
CONV_ROWS = 64
HALO = 16
TOK_TILE = 512
PRE_TILE = 1024
MLP_TILE = 1024
PRE_ROWS = 256
Q_TILE = 2048
Q_TILE_ONLINE = 512
KEY_BLOCK = 768
KEY_BLOCK_BOUNDED = 512
ONES_ROWS = 16
VT_HEAD_ROWS = HEAD_DIM + ONES_ROWS
VT_ROWS = N_KV_HEADS * VT_HEAD_ROWS
MOD_ROWS = 16
MOD_TILE = 1536
VMEM_LIMIT = 56 * 1024 * 1024
Q_SCALE = HEAD_DIM ** -0.5 * 1.4426950408889634
BOUND_MARGIN = 1.02
MAX_SCORE_BOUND = 40.0

_F32 = jnp.float32
_BF16 = jnp.bfloat16


def _dot(a, b):
    return jnp.dot(a, b, preferred_element_type=_F32)


def _cparams(sem, flags=None):
    return pltpu.CompilerParams(dimension_semantics=sem, vmem_limit_bytes=VMEM_LIMIT, flags=flags)


def _mod_kernel(c_ref, w_ref, b_ref, o_ref):
    c = c_ref[...]
    cs = c * jax.nn.sigmoid(c)
    o_ref[0] = _dot(cs.astype(_BF16), w_ref[0].astype(_BF16)) + b_ref[0]


def _modulation(c_all, w_mod, b_mod):
    n_col = 6 * D_MODEL
    return pl.pallas_call(
        _mod_kernel,
        grid=(DEPTH, n_col // MOD_TILE),
        in_specs=[
            pl.BlockSpec((MOD_ROWS, D_MODEL), lambda l, j: (0, 0)),
            pl.BlockSpec((1, D_MODEL, MOD_TILE), lambda l, j: (l, 0, j)),
            pl.BlockSpec((1, 1, MOD_TILE), lambda l, j: (l, 0, j)),
        ],
        out_specs=pl.BlockSpec((1, MOD_ROWS, MOD_TILE), lambda l, j: (l, 0, j)),
        out_shape=jax.ShapeDtypeStruct((DEPTH, MOD_ROWS, n_col), _F32),
        compiler_params=_cparams(("parallel", "parallel")),
        name="modulation",
    )(c_all, w_mod, b_mod.reshape(DEPTH, 1, n_col))


def _rope(x, cos, se, so):
    outs = []
    for c in range(x.shape[1] // LANES):
        xc = x[:, c * LANES:(c + 1) * LANES]
        nxt = pltpu.roll(xc, LANES - 1, 1)
        prv = pltpu.roll(xc, 1, 1)
        outs.append(xc * cos + nxt * se + prv * so)
    return outs[0] if len(outs) == 1 else jnp.concatenate(outs, axis=1)


def _pre_kernel(*refs, rope, emit_kv):
    x_ref, mod_ref, ng_ref, win_ref, qg_ref, kg_ref, indq_ref, indk_ref, sg_ref = refs[:9]
    refs = refs[9:]
    if rope:
        cos_ref, se_ref, so_ref = refs[:3]
        refs = refs[3:]
    q_ref, k_ref, vt_ref, ap_ref, u_ref, vn_ref, h_ref = refs[:7]
    if emit_kv:
        kf_ref, vf_ref = refs[7:9]

    for r0 in range(0, x_ref.shape[1], PRE_ROWS):
        rs = slice(r0, r0 + PRE_ROWS)
        tabs = (cos_ref[rs, :], se_ref[rs, :], so_ref[rs, :]) if rope else None
        x = x_ref[0, rs, :]
        xn = x * lax.rsqrt(jnp.mean(x * x, axis=-1, keepdims=True) + EPS) * ng_ref[...]
        h = xn * (1.0 + mod_ref[0, 1:2, :]) + mod_ref[0, 0:1, :]
        hb = h.astype(_BF16)
        h_ref[0, rs, :] = hb

        zs = jax.nn.gelu(_dot(hb, win_ref[:, O_S:O_S + 2 * SGU_W]))
        u_ref[0, rs, :] = zs[:, :SGU_W].astype(_BF16)
        v = zs[:, SGU_W:]
        vn = v * lax.rsqrt(jnp.mean(v * v, axis=-1, keepdims=True) + EPS) * sg_ref[...]
        vn_ref[0, rs, :] = vn.astype(_BF16)

        za = _dot(hb, win_ref[:, O_A:O_A + 2 * CONV_W])
        ap_ref[0, rs, 0:CONV_W] = za[:, :CONV_W] * jax.nn.sigmoid(za[:, CONV_W:])

        zq = _dot(hb, win_ref[:, O_Q:O_Q + Q_W])
        msq = _dot((zq * zq).astype(_BF16), indq_ref[...])
        qn = zq * lax.rsqrt(msq + EPS) * qg_ref[...]
        if rope:
            qn = _rope(qn, *tabs)
        q_ref[0, rs, :] = (qn * Q_SCALE).astype(_BF16)

        zkv = _dot(hb, win_ref[:, O_KV:O_KV + 2 * KV_W])
        zk = zkv[:, :KV_W]
        zv = zkv[:, KV_W:]
        msk = _dot((zk * zk).astype(_BF16), indk_ref[...])
        kn = zk * lax.rsqrt(msk + EPS) * kg_ref[...]
        if emit_kv:
            kf_ref[0, rs, :] = kn
            vf_ref[0, rs, :] = zv
        if rope:
            kn = _rope(kn, *tabs)
        k_ref[0, rs, :] = kn.astype(_BF16)
        vt = zv.T.astype(_BF16)
        for j in range(N_KV_HEADS):
            vt_ref[0, j * VT_HEAD_ROWS:j * VT_HEAD_ROWS + HEAD_DIM, rs] = vt[j * HEAD_DIM:(j + 1) * HEAD_DIM, :]
            vt_ref[0, j * VT_HEAD_ROWS + HEAD_DIM:(j + 1) * VT_HEAD_ROWS, rs] = jnp.ones((ONES_ROWS, PRE_ROWS), _BF16)

        ap_ref[0, rs, CONV_W:CONV_W + POOL_W] = _dot(hb, win_ref[:, O_P:O_P + POOL_W])


def _pre(x, mod, lw, rope_tabs, emit_kv):
    bsz, n, _ = x.shape
    t = PRE_TILE
    rope = rope_tabs is not None
    row = lambda b, i: (b, i, 0)
    const2 = lambda b, i: (0, 0)
    in_specs = [
        pl.BlockSpec((1, t, D_MODEL), row),
        pl.BlockSpec((1, 6, D_MODEL), lambda b, i: (b, 0, 0)),
        pl.BlockSpec((1, D_MODEL), const2),
        pl.BlockSpec((D_MODEL, IN_W), const2, pipeline_mode=pl.Buffered(1)),
        pl.BlockSpec((1, Q_W), const2),
        pl.BlockSpec((1, KV_W), const2),
        pl.BlockSpec((Q_W, Q_W), const2),
        pl.BlockSpec((KV_W, KV_W), const2),
        pl.BlockSpec((1, SGU_W), const2),
    ]
    args = [x, mod, lw['norm1_g'], lw['w_in'], lw['q_g'], lw['k_g'], lw['ind_q'], lw['ind_k'], lw['sgu_norm_g']]
    if rope:
        in_specs += [pl.BlockSpec((t, LANES), lambda b, i: (i, 0))] * 3
        args += list(rope_tabs)
    out_specs = [
        pl.BlockSpec((1, t, Q_W), row),
        pl.BlockSpec((1, t, KV_W), row),
        pl.BlockSpec((1, VT_ROWS, t), lambda b, i: (b, 0, i)),
        pl.BlockSpec((1, t, CONV_W + POOL_W), row),
        pl.BlockSpec((1, t, SGU_W), row),
        pl.BlockSpec((1, t, SGU_W), row),
        pl.BlockSpec((1, t, D_MODEL), row),
    ]
    out_shape = [
        jax.ShapeDtypeStruct((bsz, n, Q_W), _BF16),
        jax.ShapeDtypeStruct((bsz, n, KV_W), _BF16),
        jax.ShapeDtypeStruct((bsz, VT_ROWS, n), _BF16),
        jax.ShapeDtypeStruct((bsz, n, CONV_W + POOL_W), _F32),
        jax.ShapeDtypeStruct((bsz, n, SGU_W), _BF16),
        jax.ShapeDtypeStruct((bsz, n, SGU_W), _BF16),
        jax.ShapeDtypeStruct((bsz, n, D_MODEL), _BF16),
    ]
    if emit_kv:
        out_specs += [pl.BlockSpec((1, t, KV_W), row)] * 2
        out_shape += [jax.ShapeDtypeStruct((bsz, n, KV_W), _F32)] * 2
    return pl.pallas_call(
        functools.partial(_pre_kernel, rope=rope, emit_kv=emit_kv),
        grid=(bsz, n // t),
        in_specs=in_specs,
        out_specs=out_specs,
        out_shape=out_shape,
        compiler_params=_cparams(("parallel", "parallel")),
        name="pre_lat" if rope else "pre_ctx",
    )(*args)


def _split_heads(q_ref, qh_ref):
    tq = q_ref.shape[1]
    hi_half = lax.broadcasted_iota(jnp.int32, (1, LANES), 1) >= HEAD_DIM
    for g in range(GROUP):
        qc = q_ref[0, :, g * LANES:(g + 1) * LANES]
        for j in range(N_KV_HEADS):
            qh_ref[j, g * tq:(g + 1) * tq, :] = jnp.where(hi_half if j == 1 else jnp.logical_not(hi_half), qc,
                                                          jnp.zeros_like(qc))


def _write_heads(acc_ref, o_ref):
    tq = o_ref.shape[1]
    for g in range(GROUP):
        halves = []
        for j in range(N_KV_HEADS):
            acc = acc_ref[j, :, g * tq:(g + 1) * tq]
            halves.append(acc[0:HEAD_DIM, :] / acc[HEAD_DIM:HEAD_DIM + 1, :])
        o_ref[0, :, g * LANES:(g + 1) * LANES] = jnp.concatenate(halves, axis=0).T.astype(_BF16)


_NT = (((1,), (1,)), ((), ()))


def _attn_kernel(q_ref, k_ref, vt_ref, o_ref, qh_ref, m_ref, alpha_ref, acc_ref, s_ref, *, n_keys, key_block):
    _split_heads(q_ref, qh_ref)
    m_ref[...] = jnp.full(m_ref.shape, -1e30, _F32)
    acc_ref[...] = jnp.zeros(acc_ref.shape, _F32)

    def scores(off, j):
        s = lax.dot_general(k_ref[0, pl.ds(off, key_block), :], qh_ref[j], _NT, preferred_element_type=_F32)
        m_old = m_ref[j]
        m_new = jnp.maximum(m_old, jnp.max(s, axis=0, keepdims=True))
        m_ref[j] = m_new
        alpha_ref[j] = jnp.exp2(m_old - m_new)
        s_ref[j] = s

    def values(off, j):
        p = jnp.exp2(s_ref[j] - m_ref[j]).astype(_BF16)
        vblk = vt_ref[0, j * VT_HEAD_ROWS:(j + 1) * VT_HEAD_ROWS, pl.ds(off, key_block)]
        acc_ref[j] = alpha_ref[j] * acc_ref[j] + _dot(vblk, p)

    n_blocks = n_keys // key_block
    unroll = 2 if n_blocks % 2 == 0 else 1
    scores(0, 0)

    def body(b, carry):
        for r in range(unroll):
            off = pl.multiple_of((b * unroll + r) * key_block, key_block)
            scores(off, 1)
            values(off, 0)
            scores(off + key_block, 0)
            values(off, 1)
        return carry

    lax.fori_loop(0, n_blocks // unroll - 1, body, 0)
    for r in range(unroll):
        off = (n_blocks - unroll + r) * key_block
        scores(off, 1)
        values(off, 0)
        if r < unroll - 1:
            scores(off + key_block, 0)
        values(off, 1)
    _write_heads(acc_ref, o_ref)


def _attn_bounded_kernel(bound_ref, q_ref, ck_ref, cvt_ref, k_ref, vt_ref, o_ref, qh_ref, acc_ref, *, n_keys, key_block):
    _split_heads(q_ref, qh_ref)
    shift = bound_ref[0, 0]

    def weighted_values(kblk, vtblk, j):
        s = lax.dot_general(kblk, qh_ref[j], _NT, preferred_element_type=_F32)
        return _dot(vtblk, jnp.exp2(s - shift).astype(_BF16))

    for j in range(N_KV_HEADS):
        acc_ref[j] = weighted_values(ck_ref[0], cvt_ref[0, j * VT_HEAD_ROWS:(j + 1) * VT_HEAD_ROWS, :], j)

    n_blocks = n_keys // key_block
    unroll = 2 if n_blocks % 2 == 0 else 1

    def body(b, carry):
        for r in range(unroll):
            off = pl.multiple_of((b * unroll + r) * key_block, key_block)
            for j in range(N_KV_HEADS):
                vtblk = vt_ref[0, j * VT_HEAD_ROWS:(j + 1) * VT_HEAD_ROWS, pl.ds(off, key_block)]
                acc_ref[j] = acc_ref[j] + weighted_values(k_ref[0, pl.ds(off, key_block), :], vtblk, j)
        return carry

    lax.fori_loop(0, n_blocks // unroll, body, 0)
    _write_heads(acc_ref, o_ref)


def _attention_specs(q, tq, q_run, n_keys):
    per_run = q_run // tq
    qmap = lambda b, i: (b, i, 0)
    in_specs = [
        pl.BlockSpec((1, tq, Q_W), qmap),
        pl.BlockSpec((1, n_keys, KV_W), lambda b, i: (b, i // per_run, 0)),
        pl.BlockSpec((1, VT_ROWS, n_keys), lambda b, i: (b, 0, i // per_run)),
    ]
    out_spec = pl.BlockSpec((1, tq, Q_W), qmap)
    out_shape = jax.ShapeDtypeStruct(q.shape, _BF16)
    qh_scratch = pltpu.VMEM((N_KV_HEADS, GROUP * tq, LANES), _BF16)
    acc_scratch = pltpu.VMEM((N_KV_HEADS, VT_HEAD_ROWS, GROUP * tq), _F32)
    return in_specs, out_spec, out_shape, qh_scratch, acc_scratch


def _attention_online(q, k, vt, q_run, n_keys):
    bsz, n, _ = q.shape
    tq = min(Q_TILE_ONLINE, q_run)
    key_block = min(KEY_BLOCK, n_keys)
    in_specs, out_spec, out_shape, qh_scratch, acc_scratch = _attention_specs(q, tq, q_run, n_keys)
    row_scratch = pltpu.VMEM((N_KV_HEADS, 1, GROUP * tq), _F32)
    return pl.pallas_call(
        functools.partial(_attn_kernel, n_keys=n_keys, key_block=key_block),
        grid=(bsz, n // tq),
        in_specs=in_specs,
        out_specs=out_spec,
        out_shape=out_shape,
        scratch_shapes=[qh_scratch, row_scratch, row_scratch, acc_scratch,
                        pltpu.VMEM((N_KV_HEADS, key_block, GROUP * tq), _F32)],
        compiler_params=_cparams(("parallel", "arbitrary")),
        name="attn_online",
    )(q, k, vt)


def _attention_cached(q, ck, cvt, k, vt, score_bound):
    bsz, n, _ = q.shape
    past = ck.shape[1]
    tq = min(Q_TILE, n)
    in_specs, out_spec, out_shape, qh_scratch, acc_scratch = _attention_specs(q, tq, n, n)
    cache_specs = [
        pl.BlockSpec((1, past, KV_W), lambda b, i: (b, 0, 0)),
        pl.BlockSpec((1, VT_ROWS, past), lambda b, i: (b, 0, 0)),
    ]

    def bounded(q, ck, cvt, k, vt):
        return pl.pallas_call(
            functools.partial(_attn_bounded_kernel, n_keys=n, key_block=min(KEY_BLOCK_BOUNDED, n)),
            grid=(bsz, n // tq),
            in_specs=[pl.BlockSpec(memory_space=pltpu.SMEM), in_specs[0]] + cache_specs + in_specs[1:],
            out_specs=out_spec,
            out_shape=out_shape,
            scratch_shapes=[qh_scratch, acc_scratch],
            compiler_params=_cparams(("parallel", "arbitrary")),
            name="attn_bounded",
        )(score_bound.reshape(1, 1), q, ck, cvt, k, vt)

    def online(q, ck, cvt, k, vt):
        return _attention_online(q, jnp.concatenate([ck, k], axis=1), jnp.concatenate([cvt, vt], axis=2),
                                 n, past + n)

    return lax.cond(score_bound < MAX_SCORE_BOUND, bounded, online, q, ck, cvt, k, vt)


def _mix_merge_kernel(ap_ref, prev_ref, next_ref, u_ref, vn_ref, x_ref, h_ref, attn_ref, mod_ref, cw_ref, cb_ref,
                      lg_ref, lb_ref, pw_ref, ps_ref, sw_ref, sb_ref, wg_ref, bg_ref, wb_ref, wo_ref, o_ref,
                      buf_ref, xs_ref, cacc_ref, tmp_ref, br_ref, gate_ref, *, seq_len):
    t = ap_ref.shape[1]
    i = pl.program_id(1)
    last = pl.num_programs(1) - 1
    buf_ref[0:HALO, :] = jnp.where(i > 0, prev_ref[0], 0.0)
    buf_ref[HALO:HALO + t, :] = ap_ref[0]
    buf_ref[HALO + t:HALO + t + HALO, :] = jnp.where(i < last, next_ref[0], 0.0)
    buf_ref[t + 2 * HALO:t + 2 * HALO + SUBLANES, :] = jnp.zeros((SUBLANES, buf_ref.shape[1]), _F32)

    n_sh = t + 2 * HALO - SUBLANES
    for r in range(1, SUBLANES):
        xs_ref[r - 1] = buf_ref[r:r + n_sh, 0:CONV_W]

    n_iter = t // CONV_ROWS
    gate_cols = N_BRANCH * D_MODEL // n_iter
    for ci in range(n_iter):
        base = ci * CONV_ROWS
        acc = jnp.broadcast_to(cb_ref[...], (CONV_ROWS, CONV_W))
        for k in range(CONV_K):
            q8, r = divmod(HALO - CONV_K // 2 + k, SUBLANES)
            rows = pl.ds(base + q8 * SUBLANES, CONV_ROWS)
            x = buf_ref[rows, :CONV_W] if r == 0 else xs_ref[r - 1, rows, :]
            acc = acc + x * cw_ref[k:k + 1, :]
        cacc_ref[pl.ds(base, CONV_ROWS), :] = acc
        g0 = ci * gate_cols
        gate = jax.nn.sigmoid(_dot(h_ref[0], wg_ref[:, g0:g0 + gate_cols]) + bg_ref[:, g0:g0 + gate_cols])
        gate_ref[:, g0:g0 + gate_cols] = gate.astype(_BF16)
    acc = cacc_ref[...]
    mu = jnp.mean(acc, axis=-1, keepdims=True)
    cen = acc - mu
    var = jnp.mean(cen * cen, axis=-1, keepdims=True)
    y = cen * lax.rsqrt(var + EPS) * lg_ref[...] + lb_ref[...]
    br_ref[0] = (y * jax.nn.sigmoid(y)).astype(_BF16)

    pos = i * t + lax.broadcasted_iota(jnp.int32, (t, 1), 0)
    for g, w in enumerate(POOL_WINDOWS):
        cols = slice(CONV_W + g * POOL_GC, CONV_W + (g + 1) * POOL_GC)
        first = HALO - w // 2
        if w == 2:
            s = buf_ref[first:first + t, cols] + buf_ref[first + 1:first + 1 + t, cols]
        else:
            n = t + 2 * HALO
            tmp_ref[0, 0:n, :] = buf_ref[0:n, cols] + buf_ref[1:n + 1, cols]
            span, stage = 2, 0
            while span * 2 < w:
                n -= SUBLANES
                tmp_ref[stage + 1, 0:n, :] = tmp_ref[stage, 0:n, :] + tmp_ref[stage, span:span + n, :]
                span, stage = span * 2, stage + 1
            s = tmp_ref[stage, first:first + t, :] + tmp_ref[stage, first + span:first + span + t, :]
        lo = jnp.maximum(pos - w // 2, 0)
        hi = jnp.minimum(pos - w // 2 + w, seq_len)
        pooled = s / (hi - lo).astype(_F32) - buf_ref[HALO:HALO + t, cols]
        yg = _dot(pooled.astype(_BF16), pw_ref[g])
        br_ref[1, :, g * POOL_GC:(g + 1) * POOL_GC] = (yg * ps_ref[:, g * POOL_GC:(g + 1) * POOL_GC]).astype(_BF16)

    n_chunk = t // SGU_CHUNK
    for g in range(SGU_GROUPS):
        cols = slice(g * SGU_GC, (g + 1) * SGU_GC)
        rhs = jnp.concatenate([vn_ref[0, c * SGU_CHUNK:(c + 1) * SGU_CHUNK, cols] for c in range(n_chunk)], axis=1)
        sv = _dot(sw_ref[g], rhs)
        for c in range(n_chunk):
            rows = slice(c * SGU_CHUNK, (c + 1) * SGU_CHUNK)
            gate = sv[:, c * SGU_GC:(c + 1) * SGU_GC] + sb_ref[:, cols]
            br_ref[2, rows, cols] = (u_ref[0, rows, cols].astype(_F32) * gate).astype(_BF16)

    merged = None
    for idx in range(N_BRANCH):
        branch = attn_ref[0] if idx == 0 else br_ref[idx - 1]
        term = gate_ref[:, idx * D_MODEL:(idx + 1) * D_MODEL].astype(_F32) * _dot(branch, wb_ref[idx])
        merged = term if merged is None else merged + term
    m = _dot(merged.astype(_BF16), wo_ref[...])
    o_ref[0] = x_ref[0] + mod_ref[0, 2:3, :] * m


def _mix_merge(x, h, attn, ap, u, vn, mod, lw, t):
    bsz, n, _ = ap.shape
    hb = t // HALO
    n_hb = n // HALO
    row = lambda b, i: (b, i, 0)
    const2 = lambda b, i: (0, 0)
    const3 = lambda b, i: (0, 0, 0)
    w_all = CONV_W + POOL_W
    mod_map = (lambda b, i: (b, 0, 0)) if mod.shape[0] == bsz else (lambda b, i: (0, 0, 0))
    in_specs = [
        pl.BlockSpec((1, t, w_all), row),
        pl.BlockSpec((1, HALO, w_all), lambda b, i: (b, jnp.maximum(i * hb - 1, 0), 0)),
        pl.BlockSpec((1, HALO, w_all), lambda b, i: (b, jnp.minimum((i + 1) * hb, n_hb - 1), 0)),
        pl.BlockSpec((1, t, SGU_W), row),
        pl.BlockSpec((1, t, SGU_W), row),
        pl.BlockSpec((1, t, D_MODEL), row),
        pl.BlockSpec((1, t, D_MODEL), row),
        pl.BlockSpec((1, t, BRANCH_W), row),
        pl.BlockSpec((1, 6, D_MODEL), mod_map),
        pl.BlockSpec((CONV_K, CONV_W), const2),
        pl.BlockSpec((1, CONV_W), const2),
        pl.BlockSpec((1, CONV_W), const2),
        pl.BlockSpec((1, CONV_W), const2),
        pl.BlockSpec((POOL_GROUPS, POOL_GC, POOL_GC), const3),
        pl.BlockSpec((1, POOL_W), const2),
        pl.BlockSpec((SGU_GROUPS, SGU_CHUNK, SGU_CHUNK), const3),
        pl.BlockSpec((SGU_CHUNK, SGU_W), const2),
        pl.BlockSpec((D_MODEL, N_BRANCH * D_MODEL), const2, pipeline_mode=pl.Buffered(1)),
        pl.BlockSpec((1, N_BRANCH * D_MODEL), const2),
        pl.BlockSpec((N_BRANCH, BRANCH_W, D_MODEL), const3, pipeline_mode=pl.Buffered(1)),
        pl.BlockSpec((D_MODEL, D_MODEL), const2, pipeline_mode=pl.Buffered(1)),
    ]
    return pl.pallas_call(
        functools.partial(_mix_merge_kernel, seq_len=n),
        grid=(bsz, n // t),
        in_specs=in_specs,
        out_specs=pl.BlockSpec((1, t, D_MODEL), row),
        out_shape=jax.ShapeDtypeStruct((bsz, n, D_MODEL), _F32),
        scratch_shapes=[
            pltpu.VMEM((t + 2 * HALO + SUBLANES, w_all), _F32),
            pltpu.VMEM((SUBLANES - 1, t + 2 * HALO - SUBLANES, CONV_W), _F32),
            pltpu.VMEM((t, CONV_W), _F32),
            pltpu.VMEM((3, t + 2 * HALO, POOL_GC), _F32),
            pltpu.VMEM((N_BRANCH - 1, t, BRANCH_W), _BF16),
            pltpu.VMEM((t, N_BRANCH * D_MODEL), _BF16),
        ],
        compiler_params=_cparams(("parallel", "parallel")),
        name="mix_merge",
    )(ap, ap, ap, u, vn, x, h, attn, mod, lw['conv_w'], lw['conv_b'], lw['conv_ln_g'], lw['conv_ln_b'], lw['pool_w'],
      lw['pool_scale'], lw['sgu_w'], lw['sgu_bias'], lw['w_gate'], lw['b_gate'], lw['w_branch'], lw['w_out'])


def _mlp_kernel(x_ref, mod_ref, ng_ref, w1_ref, w2_ref, fg_ref, o_ref, hid_ref, *, final_norm):
    x = x_ref[0]
    xn = x * lax.rsqrt(jnp.mean(x * x, axis=-1, keepdims=True) + EPS) * ng_ref[...]
    hb = (xn * (1.0 + mod_ref[0, 4:5, :]) + mod_ref[0, 3:4, :]).astype(_BF16)
    for c in range(D_FF // D_MODEL):
        cols = slice(c * D_MODEL, (c + 1) * D_MODEL)
        a = jnp.maximum(_dot(hb, w1_ref[:, cols]), 0.0)
        hid_ref[:, cols] = (a * a).astype(_BF16)
    y = x + mod_ref[0, 5:6, :] * _dot(hid_ref[...], w2_ref[...])
    if final_norm:
        y = y * lax.rsqrt(jnp.mean(y * y, axis=-1, keepdims=True) + EPS) * fg_ref[...]
    o_ref[0] = y


def _mlp(x, mod, lw, final_g, final_norm):
    bsz, n, _ = x.shape
    t = MLP_TILE
    row = lambda b, i: (b, i, 0)
    const2 = lambda b, i: (0, 0)
    return pl.pallas_call(
        functools.partial(_mlp_kernel, final_norm=final_norm),
        grid=(bsz, n // t),
        in_specs=[
            pl.BlockSpec((1, t, D_MODEL), row),
            pl.BlockSpec((1, 6, D_MODEL), lambda b, i: (b, 0, 0)),
            pl.BlockSpec((1, D_MODEL), const2),
            pl.BlockSpec((D_MODEL, D_FF), const2, pipeline_mode=pl.Buffered(1)),
            pl.BlockSpec((D_FF, D_MODEL), const2, pipeline_mode=pl.Buffered(1)),
            pl.BlockSpec((1, D_MODEL), const2),
        ],
        out_specs=pl.BlockSpec((1, t, D_MODEL), row),
        out_shape=jax.ShapeDtypeStruct((bsz, n, D_MODEL), _F32),
        scratch_shapes=[pltpu.VMEM((t, D_FF), _BF16)],
        compiler_params=_cparams(("parallel", "parallel")),
        name="mlp",
    )(x, mod, lw['norm2_g'], lw['w_mlp_in'], lw['w_mlp_out'], final_g)


def _rope_tables(n):
    rows = n // GRID_W
    row = jnp.repeat(jnp.arange(rows, dtype=_F32), GRID_W)
    col = jnp.tile(jnp.arange(GRID_W, dtype=_F32), rows)
    inv = ROPE_THETA ** (-jnp.arange(0, AXIS_DIM, 2, dtype=_F32) / AXIS_DIM)
    ang = jnp.concatenate([row[:, None] * inv, col[:, None] * inv], axis=-1)
    cos = jnp.repeat(jnp.cos(ang), 2, axis=-1)
    sin = jnp.repeat(jnp.sin(ang), 2, axis=-1)
    even = (jnp.arange(HEAD_DIM) % 2 == 0)[None, :]
    se = jnp.where(even, -sin, 0.0)
    so = jnp.where(even, 0.0, sin)
    rep = LANES // HEAD_DIM
    return tuple(jnp.tile(tab, (1, rep)) for tab in (cos, se, so))


def _head_indicator(width):
    head = jnp.arange(width) // HEAD_DIM
    return ((head[:, None] == head[None, :]).astype(_F32) / HEAD_DIM).astype(_BF16)


def _layer_weights(l, w_in, norm1_g, q_norm_g, k_norm_g, conv_w, conv_b, conv_ln_g, conv_ln_b, pool_w,
                   pool_scale, sgu_norm_g, sgu_w, sgu_b, w_branch, w_gate, b_gate, w_out, norm2_g,
                   w_mlp_in, w_mlp_out):
    win = w_in[l].astype(_BF16)
    wq = win[:, :Q_W].reshape(D_MODEL, N_KV_HEADS, GROUP, HEAD_DIM).transpose(0, 2, 1, 3).reshape(D_MODEL, Q_W)
    win = jnp.concatenate([wq, win[:, Q_W:]], axis=1)
    wb = w_branch[l].astype(_BF16)
    wb0 = wb[0].reshape(N_KV_HEADS, GROUP, HEAD_DIM, D_MODEL).transpose(1, 0, 2, 3).reshape(Q_W, D_MODEL)
    wb = jnp.concatenate([wb0[None], wb[1:]], axis=0)
    row = lambda v: v.reshape(1, -1)
    return {
        'norm1_g': row(norm1_g[l]),
        'w_in': win,
        'q_g': row(jnp.tile(q_norm_g[l], N_HEADS)),
        'k_g': row(jnp.tile(k_norm_g[l], N_KV_HEADS)),
        'ind_q': _head_indicator(Q_W),
        'ind_k': _head_indicator(KV_W),
        'sgu_norm_g': row(sgu_norm_g[l]),
        'conv_w': conv_w[l],
        'conv_b': row(conv_b[l]),
        'conv_ln_g': row(conv_ln_g[l]),
        'conv_ln_b': row(conv_ln_b[l]),
        'pool_w': pool_w[l].astype(_BF16),
        'pool_scale': row(pool_scale[l]),
        'sgu_w': sgu_w[l].astype(_BF16),
        'sgu_bias': jnp.repeat(sgu_b[l].T, SGU_GC, axis=1),
        'w_gate': w_gate[l].astype(_BF16),
        'b_gate': row(b_gate[l]),
        'w_branch': wb,
        'w_out': w_out[l].astype(_BF16),
        'norm2_g': row(norm2_g[l]),
        'w_mlp_in': w_mlp_in[l].astype(_BF16),
        'w_mlp_out': w_mlp_out[l].astype(_BF16),
    }


def kernel(x_prompt, x_sample, cache_k, cache_v, c, c_ctx, w_mod, b_mod, norm1_g, w_in, q_norm_g, k_norm_g, conv_w, conv_b, conv_ln_g, conv_ln_b, pool_w, pool_scale, sgu_norm_g, sgu_w, sgu_b, w_branch, w_gate, b_gate, w_out, norm2_g, w_mlp_in, w_mlp_out, final_norm_g):
    batch, seq, _ = x_prompt.shape
    dec_batch, dec_seq, _ = x_sample.shape
    past = cache_k.shape[2]
    n_ctx = batch * seq

    c_all = jnp.concatenate([c, c_ctx[None, :], jnp.zeros((MOD_ROWS - dec_batch - 1, D_MODEL), _F32)], axis=0)
    mod = _modulation(c_all, w_mod, b_mod).reshape(DEPTH, MOD_ROWS, 6, D_MODEL)

    rope_tabs = _rope_tables(dec_seq)
    final_g = final_norm_g.reshape(1, D_MODEL)

    xp = x_prompt.reshape(1, n_ctx, D_MODEL)
    xs = x_sample
    new_k, new_v = [], []
    for l in range(DEPTH):
        lw = _layer_weights(l, w_in, norm1_g, q_norm_g, k_norm_g, conv_w, conv_b, conv_ln_g, conv_ln_b, pool_w,
                            pool_scale, sgu_norm_g, sgu_w, sgu_b, w_branch, w_gate, b_gate, w_out, norm2_g,
                            w_mlp_in, w_mlp_out)
        mod_lat = mod[l, :dec_batch]
        mod_ctx = mod[l, dec_batch:dec_batch + 1]
        last = l == DEPTH - 1

        q, k, vt, ap, u, vn, h, kf, vf = _pre(xp, mod_ctx, lw, None, True)
        new_k.append(kf.reshape(batch, seq, N_KV_HEADS, HEAD_DIM))
        new_v.append(vf.reshape(batch, seq, N_KV_HEADS, HEAD_DIM))
        attn = _attention_online(q, k, vt, seq, seq)
        per_seq = lambda a: a.reshape(batch, seq, a.shape[-1])
        xp = _mix_merge(per_seq(xp), per_seq(h), per_seq(attn), per_seq(ap), per_seq(u), per_seq(vn), mod_ctx, lw, seq)
        xp = _mlp(xp.reshape(1, n_ctx, D_MODEL), mod_ctx, lw, final_g, last)

        q, k, vt, ap, u, vn, h = _pre(xs, mod_lat, lw, rope_tabs, False)
        ck = cache_k[:, l].reshape(dec_batch, past, KV_W).astype(_BF16)
        cvt = jnp.transpose(cache_v[:, l], (0, 2, 3, 1)).astype(_BF16)
        cvt = jnp.concatenate([cvt, jnp.ones((dec_batch, N_KV_HEADS, ONES_ROWS, past), _BF16)], axis=2)
        cvt = cvt.reshape(dec_batch, VT_ROWS, past)
        q_norm = HEAD_DIM ** 0.5 * jnp.max(jnp.abs(q_norm_g[l])) * Q_SCALE
        k_norm = HEAD_DIM ** 0.5 * jnp.max(jnp.abs(k_norm_g[l]))
        ck_sq = jnp.square(ck.astype(_F32)).reshape(dec_batch, past, N_KV_HEADS, HEAD_DIM)
        k_norm = jnp.maximum(k_norm, jnp.sqrt(jnp.max(jnp.sum(ck_sq, axis=-1))))
        attn = _attention_cached(q, ck, cvt, k, vt, BOUND_MARGIN * q_norm * k_norm)
        xs = _mix_merge(xs, h, attn, ap, u, vn, mod_lat, lw, TOK_TILE)
        xs = _mlp(xs, mod_lat, lw, final_g, last)

    y_prompt = xp.reshape(batch, seq, D_MODEL)
    return (y_prompt, xs, jnp.stack(new_k, axis=1), jnp.stack(new_v, axis=1))
```

```python
import functools

import jax
import jax.numpy as jnp
from jax import lax
from jax.experimental import pallas as pl
from jax.experimental.pallas import tpu as pltpu

D_MODEL = 1024
DEPTH = 2
GRID_W = 64
N_HEADS = 8
N_KV_HEADS = 2
HEAD_DIM = 64
Q_W = N_HEADS * HEAD_DIM
KV_W = N_KV_HEADS * HEAD_DIM
GROUP = N_HEADS // N_KV_HEADS
AXIS_DIM = HEAD_DIM // 2
ROPE_THETA = 10000.0
CONV_W = 512
CONV_K = 31
POOL_W = 512
POOL_GROUPS = 4
POOL_GC = POOL_W // POOL_GROUPS
POOL_WINDOWS = (2, 4, 8, 16)
SGU_W = 512
SGU_GROUPS = 4
SGU_GC = SGU_W // SGU_GROUPS
SGU_CHUNK = 128
BRANCH_W = 512
N_BRANCH = 4
D_FF = 4 * D_MODEL
IN_W = Q_W + 2 * KV_W + 2 * CONV_W + POOL_W + 2 * SGU_W
EPS = 1e-6

O_Q = 0
O_KV = Q_W
O_A = Q_W + 2 * KV_W
O_P = O_A + 2 * CONV_W
O_S = O_P + POOL_W

LANES = 128
SUBLANES = 8
CONV_ROWS = 64
HALO = 16
TOK_TILE = 512
PRE_TILE = 1024
MLP_TILE = 1024
PRE_ROWS = 256
Q_TILE = 2048
Q_TILE_ONLINE = 512
KEY_BLOCK = 768
KEY_BLOCK_BOUNDED = 512
ONES_ROWS = 16
VT_HEAD_ROWS = HEAD_DIM + ONES_ROWS
VT_ROWS = N_KV_HEADS * VT_HEAD_ROWS
MOD_ROWS = 16
MOD_TILE = 1536
VMEM_LIMIT = 56 * 1024 * 1024
Q_SCALE = HEAD_DIM ** -0.5 * 1.4426950408889634
BOUND_MARGIN = 1.02
MAX_SCORE_BOUND = 40.0

_F32 = jnp.float32
_BF16 = jnp.bfloat16


def _dot(a, b):
    return jnp.dot(a, b, preferred_element_type=_F32)


def _cparams(sem, flags=None):
    return pltpu.CompilerParams(dimension_semantics=sem, vmem_limit_bytes=VMEM_LIMIT, flags=flags)


def _mod_kernel(c_ref, w_ref, b_ref, o_ref):
    c = c_ref[...]
    cs = c * jax.nn.sigmoid(c)
    o_ref[0] = _dot(cs.astype(_BF16), w_ref[0].astype(_BF16)) + b_ref[0]


def _modulation(c_all, w_mod, b_mod):
    n_col = 6 * D_MODEL
    return pl.pallas_call(
        _mod_kernel,
        grid=(DEPTH, n_col // MOD_TILE),
        in_specs=[
            pl.BlockSpec((MOD_ROWS, D_MODEL), lambda l, j: (0, 0)),
            pl.BlockSpec((1, D_MODEL, MOD_TILE), lambda l, j: (l, 0, j)),
            pl.BlockSpec((1, 1, MOD_TILE), lambda l, j: (l, 0, j)),
        ],
        out_specs=pl.BlockSpec((1, MOD_ROWS, MOD_TILE), lambda l, j: (l, 0, j)),
        out_shape=jax.ShapeDtypeStruct((DEPTH, MOD_ROWS, n_col), _F32),
        compiler_params=_cparams(("parallel", "parallel")),
        name="modulation",
    )(c_all, w_mod, b_mod.reshape(DEPTH, 1, n_col))


def _rope(x, cos, se, so):
    outs = []
    for c in range(x.shape[1] // LANES):
        xc = x[:, c * LANES:(c + 1) * LANES]
        nxt = pltpu.roll(xc, LANES - 1, 1)
        prv = pltpu.roll(xc, 1, 1)
        outs.append(xc * cos + nxt * se + prv * so)
    return outs[0] if len(outs) == 1 else jnp.concatenate(outs, axis=1)


def _pre_kernel(*refs, rope, emit_kv):
    x_ref, mod_ref, ng_ref, win_ref, qg_ref, kg_ref, indq_ref, indk_ref, sg_ref = refs[:9]
    refs = refs[9:]
    if rope:
        cos_ref, se_ref, so_ref = refs[:3]
        refs = refs[3:]
    q_ref, k_ref, vt_ref, ap_ref, u_ref, vn_ref, h_ref = refs[:7]
    if emit_kv:
        kf_ref, vf_ref = refs[7:9]

    for r0 in range(0, x_ref.shape[1], PRE_ROWS):
        rs = slice(r0, r0 + PRE_ROWS)
        tabs = (cos_ref[rs, :], se_ref[rs, :], so_ref[rs, :]) if rope else None
        x = x_ref[0, rs, :]
        xn = x * lax.rsqrt(jnp.mean(x * x, axis=-1, keepdims=True) + EPS) * ng_ref[...]
        h = xn * (1.0 + mod_ref[0, 1:2, :]) + mod_ref[0, 0:1, :]
        hb = h.astype(_BF16)
        h_ref[0, rs, :] = hb

        zs = jax.nn.gelu(_dot(hb, win_ref[:, O_S:O_S + 2 * SGU_W]))
        u_ref[0, rs, :] = zs[:, :SGU_W].astype(_BF16)
        v = zs[:, SGU_W:]
        vn = v * lax.rsqrt(jnp.mean(v * v, axis=-1, keepdims=True) + EPS) * sg_ref[...]
        vn_ref[0, rs, :] = vn.astype(_BF16)

        za = _dot(hb, win_ref[:, O_A:O_A + 2 * CONV_W])
        ap_ref[0, rs, 0:CONV_W] = za[:, :CONV_W] * jax.nn.sigmoid(za[:, CONV_W:])

        zq = _dot(hb, win_ref[:, O_Q:O_Q + Q_W])
        msq = _dot((zq * zq).astype(_BF16), indq_ref[...])
        qn = zq * lax.rsqrt(msq + EPS) * qg_ref[...]
        if rope:
            qn = _rope(qn, *tabs)
        q_ref[0, rs, :] = (qn * Q_SCALE).astype(_BF16)

        zkv = _dot(hb, win_ref[:, O_KV:O_KV + 2 * KV_W])
        zk = zkv[:, :KV_W]
        zv = zkv[:, KV_W:]
        msk = _dot((zk * zk).astype(_BF16), indk_ref[...])
        kn = zk * lax.rsqrt(msk + EPS) * kg_ref[...]
        if emit_kv:
            kf_ref[0, rs, :] = kn
            vf_ref[0, rs, :] = zv
        if rope:
            kn = _rope(kn, *tabs)
        k_ref[0, rs, :] = kn.astype(_BF16)
        vt = zv.T.astype(_BF16)
        for j in range(N_KV_HEADS):
            vt_ref[0, j * VT_HEAD_ROWS:j * VT_HEAD_ROWS + HEAD_DIM, rs] = vt[j * HEAD_DIM:(j + 1) * HEAD_DIM, :]
            vt_ref[0, j * VT_HEAD_ROWS + HEAD_DIM:(j + 1) * VT_HEAD_ROWS, rs] = jnp.ones((ONES_ROWS, PRE_ROWS), _BF16)

        ap_ref[0, rs, CONV_W:CONV_W + POOL_W] = _dot(hb, win_ref[:, O_P:O_P + POOL_W])


def _pre(x, mod, lw, rope_tabs, emit_kv):
    bsz, n, _ = x.shape
    t = PRE_TILE
    rope = rope_tabs is not None
    row = lambda b, i: (b, i, 0)
    const2 = lambda b, i: (0, 0)
    in_specs = [
        pl.BlockSpec((1, t, D_MODEL), row),
        pl.BlockSpec((1, 6, D_MODEL), lambda b, i: (b, 0, 0)),
        pl.BlockSpec((1, D_MODEL), const2),
        pl.BlockSpec((D_MODEL, IN_W), const2, pipeline_mode=pl.Buffered(1)),
        pl.BlockSpec((1, Q_W), const2),
        pl.BlockSpec((1, KV_W), const2),
        pl.BlockSpec((Q_W, Q_W), const2),
        pl.BlockSpec((KV_W, KV_W), const2),
        pl.BlockSpec((1, SGU_W), const2),
    ]
    args = [x, mod, lw['norm1_g'], lw['w_in'], lw['q_g'], lw['k_g'], lw['ind_q'], lw['ind_k'], lw['sgu_norm_g']]
    if rope:
        in_specs += [pl.BlockSpec((t, LANES), lambda b, i: (i, 0))] * 3
        args += list(rope_tabs)
    out_specs = [
        pl.BlockSpec((1, t, Q_W), row),
        pl.BlockSpec((1, t, KV_W), row),
        pl.BlockSpec((1, VT_ROWS, t), lambda b, i: (b, 0, i)),
        pl.BlockSpec((1, t, CONV_W + POOL_W), row),
        pl.BlockSpec((1, t, SGU_W), row),
        pl.BlockSpec((1, t, SGU_W), row),
        pl.BlockSpec((1, t, D_MODEL), row),
    ]
    out_shape = [
        jax.ShapeDtypeStruct((bsz, n, Q_W), _BF16),
        jax.ShapeDtypeStruct((bsz, n, KV_W), _BF16),
        jax.ShapeDtypeStruct((bsz, VT_ROWS, n), _BF16),
        jax.ShapeDtypeStruct((bsz, n, CONV_W + POOL_W), _F32),
        jax.ShapeDtypeStruct((bsz, n, SGU_W), _BF16),
        jax.ShapeDtypeStruct((bsz, n, SGU_W), _BF16),
        jax.ShapeDtypeStruct((bsz, n, D_MODEL), _BF16),
    ]
    if emit_kv:
        out_specs += [pl.BlockSpec((1, t, KV_W), row)] * 2
        out_shape += [jax.ShapeDtypeStruct((bsz, n, KV_W), _F32)] * 2
    return pl.pallas_call(
        functools.partial(_pre_kernel, rope=rope, emit_kv=emit_kv),
        grid=(bsz, n // t),
        in_specs=in_specs,
        out_specs=out_specs,
        out_shape=out_shape,
        compiler_params=_cparams(("parallel", "parallel")),
        name="pre_lat" if rope else "pre_ctx",
    )(*args)


def _split_heads(q_ref, qh_ref):
    tq = q_ref.shape[1]
    hi_half = lax.broadcasted_iota(jnp.int32, (1, LANES), 1) >= HEAD_DIM
    for g in range(GROUP):
        qc = q_ref[0, :, g * LANES:(g + 1) * LANES]
        for j in range(N_KV_HEADS):
            qh_ref[j, g * tq:(g + 1) * tq, :] = jnp.where(hi_half if j == 1 else jnp.logical_not(hi_half), qc,
                                                          jnp.zeros_like(qc))


def _write_heads(acc_ref, o_ref):
    tq = o_ref.shape[1]
    for g in range(GROUP):
        halves = []
        for j in range(N_KV_HEADS):
            acc = acc_ref[j, :, g * tq:(g + 1) * tq]
            halves.append(acc[0:HEAD_DIM, :] / acc[HEAD_DIM:HEAD_DIM + 1, :])
        o_ref[0, :, g * LANES:(g + 1) * LANES] = jnp.concatenate(halves, axis=0).T.astype(_BF16)


_NT = (((1,), (1,)), ((), ()))


def _attn_kernel(q_ref, k_ref, vt_ref, o_ref, qh_ref, m_ref, alpha_ref, acc_ref, s_ref, *, n_keys, key_block):
    _split_heads(q_ref, qh_ref)
    m_ref[...] = jnp.full(m_ref.shape, -1e30, _F32)
    acc_ref[...] = jnp.zeros(acc_ref.shape, _F32)

    def scores(off, j):
        s = lax.dot_general(k_ref[0, pl.ds(off, key_block), :], qh_ref[j], _NT, preferred_element_type=_F32)
        m_old = m_ref[j]
        m_new = jnp.maximum(m_old, jnp.max(s, axis=0, keepdims=True))
        m_ref[j] = m_new
        alpha_ref[j] = jnp.exp2(m_old - m_new)
        s_ref[j] = s

    def values(off, j):
        p = jnp.exp2(s_ref[j] - m_ref[j]).astype(_BF16)
        vblk = vt_ref[0, j * VT_HEAD_ROWS:(j + 1) * VT_HEAD_ROWS, pl.ds(off, key_block)]
        acc_ref[j] = alpha_ref[j] * acc_ref[j] + _dot(vblk, p)

    n_blocks = n_keys // key_block
    unroll = 2 if n_blocks % 2 == 0 else 1
    scores(0, 0)

    def body(b, carry):
        for r in range(unroll):
            off = pl.multiple_of((b * unroll + r) * key_block, key_block)
            scores(off, 1)
            values(off, 0)
            scores(off + key_block, 0)
            values(off, 1)
        return carry

    lax.fori_loop(0, n_blocks // unroll - 1, body, 0)
    for r in range(unroll):
        off = (n_blocks - unroll + r) * key_block
        scores(off, 1)
        values(off, 0)
        if r < unroll - 1:
            scores(off + key_block, 0)
        values(off, 1)
    _write_heads(acc_ref, o_ref)


def _attn_bounded_kernel(bound_ref, q_ref, ck_ref, cvt_ref, k_ref, vt_ref, o_ref, qh_ref, acc_ref, *, n_keys, key_block):
    _split_heads(q_ref, qh_ref)
    shift = bound_ref[0, 0]

    def weighted_values(kblk, vtblk, j):
        s = lax.dot_general(kblk, qh_ref[j], _NT, preferred_element_type=_F32)
        return _dot(vtblk, jnp.exp2(s - shift).astype(_BF16))

    for j in range(N_KV_HEADS):
        acc_ref[j] = weighted_values(ck_ref[0], cvt_ref[0, j * VT_HEAD_ROWS:(j + 1) * VT_HEAD_ROWS, :], j)

    n_blocks = n_keys // key_block
    unroll = 2 if n_blocks % 2 == 0 else 1

    def body(b, carry):
        for r in range(unroll):
            off = pl.multiple_of((b * unroll + r) * key_block, key_block)
            for j in range(N_KV_HEADS):
                vtblk = vt_ref[0, j * VT_HEAD_ROWS:(j + 1) * VT_HEAD_ROWS, pl.ds(off, key_block)]
                acc_ref[j] = acc_ref[j] + weighted_values(k_ref[0, pl.ds(off, key_block), :], vtblk, j)
        return carry

    lax.fori_loop(0, n_blocks // unroll, body, 0)
    _write_heads(acc_ref, o_ref)


def _attention_specs(q, tq, q_run, n_keys):
    per_run = q_run // tq
    qmap = lambda b, i: (b, i, 0)
    in_specs = [
        pl.BlockSpec((1, tq, Q_W), qmap),
        pl.BlockSpec((1, n_keys, KV_W), lambda b, i: (b, i // per_run, 0)),
        pl.BlockSpec((1, VT_ROWS, n_keys), lambda b, i: (b, 0, i // per_run)),
    ]
    out_spec = pl.BlockSpec((1, tq, Q_W), qmap)
    out_shape = jax.ShapeDtypeStruct(q.shape, _BF16)
    qh_scratch = pltpu.VMEM((N_KV_HEADS, GROUP * tq, LANES), _BF16)
    acc_scratch = pltpu.VMEM((N_KV_HEADS, VT_HEAD_ROWS, GROUP * tq), _F32)
    return in_specs, out_spec, out_shape, qh_scratch, acc_scratch


def _attention_online(q, k, vt, q_run, n_keys):
    bsz, n, _ = q.shape
    tq = min(Q_TILE_ONLINE, q_run)
    key_block = min(KEY_BLOCK, n_keys)
    in_specs, out_spec, out_shape, qh_scratch, acc_scratch = _attention_specs(q, tq, q_run, n_keys)
    row_scratch = pltpu.VMEM((N_KV_HEADS, 1, GROUP * tq), _F32)
    return pl.pallas_call(
        functools.partial(_attn_kernel, n_keys=n_keys, key_block=key_block),
        grid=(bsz, n // tq),
        in_specs=in_specs,
        out_specs=out_spec,
        out_shape=out_shape,
        scratch_shapes=[qh_scratch, row_scratch, row_scratch, acc_scratch,
                        pltpu.VMEM((N_KV_HEADS, key_block, GROUP * tq), _F32)],
        compiler_params=_cparams(("parallel", "arbitrary")),
        name="attn_online",
    )(q, k, vt)


def _attention_cached(q, ck, cvt, k, vt, score_bound):
    bsz, n, _ = q.shape
    past = ck.shape[1]
    tq = min(Q_TILE, n)
    in_specs, out_spec, out_shape, qh_scratch, acc_scratch = _attention_specs(q, tq, n, n)
    cache_specs = [
        pl.BlockSpec((1, past, KV_W), lambda b, i: (b, 0, 0)),
        pl.BlockSpec((1, VT_ROWS, past), lambda b, i: (b, 0, 0)),
    ]

    def bounded(q, ck, cvt, k, vt):
        return pl.pallas_call(
            functools.partial(_attn_bounded_kernel, n_keys=n, key_block=min(KEY_BLOCK_BOUNDED, n)),
            grid=(bsz, n // tq),
            in_specs=[pl.BlockSpec(memory_space=pltpu.SMEM), in_specs[0]] + cache_specs + in_specs[1:],
            out_specs=out_spec,
            out_shape=out_shape,
            scratch_shapes=[qh_scratch, acc_scratch],
            compiler_params=_cparams(("parallel", "arbitrary")),
            name="attn_bounded",
        )(score_bound.reshape(1, 1), q, ck, cvt, k, vt)

    def online(q, ck, cvt, k, vt):
        return _attention_online(q, jnp.concatenate([ck, k], axis=1), jnp.concatenate([cvt, vt], axis=2),
                                 n, past + n)

    return lax.cond(score_bound < MAX_SCORE_BOUND, bounded, online, q, ck, cvt, k, vt)


def _mix_merge_kernel(ap_ref, prev_ref, next_ref, u_ref, vn_ref, x_ref, h_ref, attn_ref, mod_ref, cw_ref, cb_ref,
                      lg_ref, lb_ref, pw_ref, ps_ref, sw_ref, sb_ref, wg_ref, bg_ref, wb_ref, wo_ref, o_ref,
                      buf_ref, xs_ref, tmp_ref, br_ref, gate_ref, *, seq_len):
    t = ap_ref.shape[1]
    i = pl.program_id(1)
    last = pl.num_programs(1) - 1
    buf_ref[0:HALO, :] = jnp.where(i > 0, prev_ref[0], 0.0)
    buf_ref[HALO:HALO + t, :] = ap_ref[0]
    buf_ref[HALO + t:HALO + t + HALO, :] = jnp.where(i < last, next_ref[0], 0.0)
    buf_ref[t + 2 * HALO:t + 2 * HALO + SUBLANES, :] = jnp.zeros((SUBLANES, buf_ref.shape[1]), _F32)

    n_sh = t + 2 * HALO - SUBLANES
    for r in range(1, SUBLANES):
        xs_ref[r - 1] = buf_ref[r:r + n_sh, 0:CONV_W]

    n_iter = t // CONV_ROWS
    gate_cols = N_BRANCH * D_MODEL // n_iter
    for ci in range(n_iter):
        base = ci * CONV_ROWS
        acc = jnp.broadcast_to(cb_ref[...], (CONV_ROWS, CONV_W))
        for k in range(CONV_K):
            q8, r = divmod(HALO - CONV_K // 2 + k, SUBLANES)
            rows = pl.ds(base + q8 * SUBLANES, CONV_ROWS)
            x = buf_ref[rows, :CONV_W] if r == 0 else xs_ref[r - 1, rows, :]
            acc = acc + x * cw_ref[k:k + 1, :]
        mu = jnp.mean(acc, axis=-1, keepdims=True)
        cen = acc - mu
        var = jnp.mean(cen * cen, axis=-1, keepdims=True)
        y = cen * lax.rsqrt(var + EPS) * lg_ref[...] + lb_ref[...]
        br_ref[0, pl.ds(base, CONV_ROWS), :] = (y * jax.nn.sigmoid(y)).astype(_BF16)
        g0 = ci * gate_cols
        gate = jax.nn.sigmoid(_dot(h_ref[0], wg_ref[:, g0:g0 + gate_cols]) + bg_ref[:, g0:g0 + gate_cols])
        gate_ref[:, g0:g0 + gate_cols] = gate.astype(_BF16)

    pos = i * t + lax.broadcasted_iota(jnp.int32, (t, 1), 0)
    for g, w in enumerate(POOL_WINDOWS):
        cols = slice(CONV_W + g * POOL_GC, CONV_W + (g + 1) * POOL_GC)
        first = HALO - w // 2
        if w == 2:
            s = buf_ref[first:first + t, cols] + buf_ref[first + 1:first + 1 + t, cols]
        else:
            n = t + 2 * HALO
            tmp_ref[0, 0:n, :] = buf_ref[0:n, cols] + buf_ref[1:n + 1, cols]
            span, stage = 2, 0
            while span * 2 < w:
                n -= SUBLANES
                tmp_ref[stage + 1, 0:n, :] = tmp_ref[stage, 0:n, :] + tmp_ref[stage, span:span + n, :]
                span, stage = span * 2, stage + 1
            s = tmp_ref[stage, first:first + t, :] + tmp_ref[stage, first + span:first + span + t, :]
        lo = jnp.maximum(pos - w // 2, 0)
        hi = jnp.minimum(pos - w // 2 + w, seq_len)
        pooled = s / (hi - lo).astype(_F32) - buf_ref[HALO:HALO + t, cols]
        yg = _dot(pooled.astype(_BF16), pw_ref[g])
        br_ref[1, :, g * POOL_GC:(g + 1) * POOL_GC] = (yg * ps_ref[:, g * POOL_GC:(g + 1) * POOL_GC]).astype(_BF16)

    n_chunk = t // SGU_CHUNK
    for g in range(SGU_GROUPS):
        cols = slice(g * SGU_GC, (g + 1) * SGU_GC)
        rhs = jnp.concatenate([vn_ref[0, c * SGU_CHUNK:(c + 1) * SGU_CHUNK, cols] for c in range(n_chunk)], axis=1)
        sv = _dot(sw_ref[g], rhs)
        for c in range(n_chunk):
            rows = slice(c * SGU_CHUNK, (c + 1) * SGU_CHUNK)
            gate = sv[:, c * SGU_GC:(c + 1) * SGU_GC] + sb_ref[:, cols]
            br_ref[2, rows, cols] = (u_ref[0, rows, cols].astype(_F32) * gate).astype(_BF16)

    merged = None
    for idx in range(N_BRANCH):
        branch = attn_ref[0] if idx == 0 else br_ref[idx - 1]
        term = gate_ref[:, idx * D_MODEL:(idx + 1) * D_MODEL].astype(_F32) * _dot(branch, wb_ref[idx])
        merged = term if merged is None else merged + term
    m = _dot(merged.astype(_BF16), wo_ref[...])
    o_ref[0] = x_ref[0] + mod_ref[0, 2:3, :] * m


def _mix_merge(x, h, attn, ap, u, vn, mod, lw, t):
    bsz, n, _ = ap.shape
    hb = t // HALO
    n_hb = n // HALO
    row = lambda b, i: (b, i, 0)
    const2 = lambda b, i: (0, 0)
    const3 = lambda b, i: (0, 0, 0)
    w_all = CONV_W + POOL_W
    mod_map = (lambda b, i: (b, 0, 0)) if mod.shape[0] == bsz else (lambda b, i: (0, 0, 0))
    in_specs = [
        pl.BlockSpec((1, t, w_all), row),
        pl.BlockSpec((1, HALO, w_all), lambda b, i: (b, jnp.maximum(i * hb - 1, 0), 0)),
        pl.BlockSpec((1, HALO, w_all), lambda b, i: (b, jnp.minimum((i + 1) * hb, n_hb - 1), 0)),
        pl.BlockSpec((1, t, SGU_W), row),
        pl.BlockSpec((1, t, SGU_W), row),
        pl.BlockSpec((1, t, D_MODEL), row),
        pl.BlockSpec((1, t, D_MODEL), row),
        pl.BlockSpec((1, t, BRANCH_W), row),
        pl.BlockSpec((1, 6, D_MODEL), mod_map),
        pl.BlockSpec((CONV_K, CONV_W), const2),
        pl.BlockSpec((1, CONV_W), const2),
        pl.BlockSpec((1, CONV_W), const2),
        pl.BlockSpec((1, CONV_W), const2),
        pl.BlockSpec((POOL_GROUPS, POOL_GC, POOL_GC), const3),
        pl.BlockSpec((1, POOL_W), const2),
        pl.BlockSpec((SGU_GROUPS, SGU_CHUNK, SGU_CHUNK), const3),
        pl.BlockSpec((SGU_CHUNK, SGU_W), const2),
        pl.BlockSpec((D_MODEL, N_BRANCH * D_MODEL), const2, pipeline_mode=pl.Buffered(1)),
        pl.BlockSpec((1, N_BRANCH * D_MODEL), const2),
        pl.BlockSpec((N_BRANCH, BRANCH_W, D_MODEL), const3, pipeline_mode=pl.Buffered(1)),
        pl.BlockSpec((D_MODEL, D_MODEL), const2, pipeline_mode=pl.Buffered(1)),
    ]
    return pl.pallas_call(
        functools.partial(_mix_merge_kernel, seq_len=n),
        grid=(bsz, n // t),
        in_specs=in_specs,
        out_specs=pl.BlockSpec((1, t, D_MODEL), row),
        out_shape=jax.ShapeDtypeStruct((bsz, n, D_MODEL), _F32),
        scratch_shapes=[
            pltpu.VMEM((t + 2 * HALO + SUBLANES, w_all), _F32),
            pltpu.VMEM((SUBLANES - 1, t + 2 * HALO - SUBLANES, CONV_W), _F32),
            pltpu.VMEM((3, t + 2 * HALO, POOL_GC), _F32),
            pltpu.VMEM((N_BRANCH - 1, t, BRANCH_W), _BF16),
            pltpu.VMEM((t, N_BRANCH * D_MODEL), _BF16),
        ],
        compiler_params=_cparams(("parallel", "parallel")),
        name="mix_merge",
    )(ap, ap, ap, u, vn, x, h, attn, mod, lw['conv_w'], lw['conv_b'], lw['conv_ln_g'], lw['conv_ln_b'], lw['pool_w'],
      lw['pool_scale'], lw['sgu_w'], lw['sgu_bias'], lw['w_gate'], lw['b_gate'], lw['w_branch'], lw['w_out'])


def _mlp_kernel(x_ref, mod_ref, ng_ref, w1_ref, w2_ref, fg_ref, o_ref, hid_ref, *, final_norm):
    x = x_ref[0]
    xn = x * lax.rsqrt(jnp.mean(x * x, axis=-1, keepdims=True) + EPS) * ng_ref[...]
    hb = (xn * (1.0 + mod_ref[0, 4:5, :]) + mod_ref[0, 3:4, :]).astype(_BF16)
    for c in range(D_FF // D_MODEL):
        cols = slice(c * D_MODEL, (c + 1) * D_MODEL)
        a = jnp.maximum(_dot(hb, w1_ref[:, cols]), 0.0)
        hid_ref[:, cols] = (a * a).astype(_BF16)
    y = x + mod_ref[0, 5:6, :] * _dot(hid_ref[...], w2_ref[...])
    if final_norm:
        y = y * lax.rsqrt(jnp.mean(y * y, axis=-1, keepdims=True) + EPS) * fg_ref[...]
    o_ref[0] = y


def _mlp(x, mod, lw, final_g, final_norm):
    bsz, n, _ = x.shape
    t = MLP_TILE
    row = lambda b, i: (b, i, 0)
    const2 = lambda b, i: (0, 0)
    return pl.pallas_call(
        functools.partial(_mlp_kernel, final_norm=final_norm),
        grid=(bsz, n // t),
        in_specs=[
            pl.BlockSpec((1, t, D_MODEL), row),
            pl.BlockSpec((1, 6, D_MODEL), lambda b, i: (b, 0, 0)),
            pl.BlockSpec((1, D_MODEL), const2),
            pl.BlockSpec((D_MODEL, D_FF), const2, pipeline_mode=pl.Buffered(1)),
            pl.BlockSpec((D_FF, D_MODEL), const2, pipeline_mode=pl.Buffered(1)),
            pl.BlockSpec((1, D_MODEL), const2),
        ],
        out_specs=pl.BlockSpec((1, t, D_MODEL), row),
        out_shape=jax.ShapeDtypeStruct((bsz, n, D_MODEL), _F32),
        scratch_shapes=[pltpu.VMEM((t, D_FF), _BF16)],
        compiler_params=_cparams(("parallel", "parallel")),
        name="mlp",
    )(x, mod, lw['norm2_g'], lw['w_mlp_in'], lw['w_mlp_out'], final_g)


def _rope_tables(n):
    rows = n // GRID_W
    row = jnp.repeat(jnp.arange(rows, dtype=_F32), GRID_W)
    col = jnp.tile(jnp.arange(GRID_W, dtype=_F32), rows)
    inv = ROPE_THETA ** (-jnp.arange(0, AXIS_DIM, 2, dtype=_F32) / AXIS_DIM)
    ang = jnp.concatenate([row[:, None] * inv, col[:, None] * inv], axis=-1)
    cos = jnp.repeat(jnp.cos(ang), 2, axis=-1)
    sin = jnp.repeat(jnp.sin(ang), 2, axis=-1)
    even = (jnp.arange(HEAD_DIM) % 2 == 0)[None, :]
    se = jnp.where(even, -sin, 0.0)
    so = jnp.where(even, 0.0, sin)
    rep = LANES // HEAD_DIM
    return tuple(jnp.tile(tab, (1, rep)) for tab in (cos, se, so))


def _head_indicator(width):
    head = jnp.arange(width) // HEAD_DIM
    return ((head[:, None] == head[None, :]).astype(_F32) / HEAD_DIM).astype(_BF16)


def _layer_weights(l, w_in, norm1_g, q_norm_g, k_norm_g, conv_w, conv_b, conv_ln_g, conv_ln_b, pool_w,
                   pool_scale, sgu_norm_g, sgu_w, sgu_b, w_branch, w_gate, b_gate, w_out, norm2_g,
                   w_mlp_in, w_mlp_out):
    win = w_in[l].astype(_BF16)
    wq = win[:, :Q_W].reshape(D_MODEL, N_KV_HEADS, GROUP, HEAD_DIM).transpose(0, 2, 1, 3).reshape(D_MODEL, Q_W)
    win = jnp.concatenate([wq, win[:, Q_W:]], axis=1)
    wb = w_branch[l].astype(_BF16)
    wb0 = wb[0].reshape(N_KV_HEADS, GROUP, HEAD_DIM, D_MODEL).transpose(1, 0, 2, 3).reshape(Q_W, D_MODEL)
    wb = jnp.concatenate([wb0[None], wb[1:]], axis=0)
    row = lambda v: v.reshape(1, -1)
    return {
        'norm1_g': row(norm1_g[l]),
        'w_in': win,
        'q_g': row(jnp.tile(q_norm_g[l], N_HEADS)),
        'k_g': row(jnp.tile(k_norm_g[l], N_KV_HEADS)),
        'ind_q': _head_indicator(Q_W),
        'ind_k': _head_indicator(KV_W),
        'sgu_norm_g': row(sgu_norm_g[l]),
        'conv_w': conv_w[l],
        'conv_b': row(conv_b[l]),
        'conv_ln_g': row(conv_ln_g[l]),
        'conv_ln_b': row(conv_ln_b[l]),
        'pool_w': pool_w[l].astype(_BF16),
        'pool_scale': row(pool_scale[l]),
        'sgu_w': sgu_w[l].astype(_BF16),
        'sgu_bias': jnp.repeat(sgu_b[l].T, SGU_GC, axis=1),
        'w_gate': w_gate[l].astype(_BF16),
        'b_gate': row(b_gate[l]),
        'w_branch': wb,
        'w_out': w_out[l].astype(_BF16),
        'norm2_g': row(norm2_g[l]),
        'w_mlp_in': w_mlp_in[l].astype(_BF16),
        'w_mlp_out': w_mlp_out[l].astype(_BF16),
    }


def kernel(x_prompt, x_sample, cache_k, cache_v, c, c_ctx, w_mod, b_mod, norm1_g, w_in, q_norm_g, k_norm_g, conv_w, conv_b, conv_ln_g, conv_ln_b, pool_w, pool_scale, sgu_norm_g, sgu_w, sgu_b, w_branch, w_gate, b_gate, w_out, norm2_g, w_mlp_in, w_mlp_out, final_norm_g):
    batch, seq, _ = x_prompt.shape
    dec_batch, dec_seq, _ = x_sample.shape
    past = cache_k.shape[2]
    n_ctx = batch * seq

    c_all = jnp.concatenate([c, c_ctx[None, :], jnp.zeros((MOD_ROWS - dec_batch - 1, D_MODEL), _F32)], axis=0)
    mod = _modulation(c_all, w_mod, b_mod).reshape(DEPTH, MOD_ROWS, 6, D_MODEL)

    rope_tabs = _rope_tables(dec_seq)
    final_g = final_norm_g.reshape(1, D_MODEL)

    xp = x_prompt.reshape(1, n_ctx, D_MODEL)
    xs = x_sample
    new_k, new_v = [], []
    for l in range(DEPTH):
        lw = _layer_weights(l, w_in, norm1_g, q_norm_g, k_norm_g, conv_w, conv_b, conv_ln_g, conv_ln_b, pool_w,
                            pool_scale, sgu_norm_g, sgu_w, sgu_b, w_branch, w_gate, b_gate, w_out, norm2_g,
                            w_mlp_in, w_mlp_out)
        mod_lat = mod[l, :dec_batch]
        mod_ctx = mod[l, dec_batch:dec_batch + 1]
        last = l == DEPTH - 1

        q, k, vt, ap, u, vn, h, kf, vf = _pre(xp, mod_ctx, lw, None, True)
        new_k.append(kf.reshape(batch, seq, N_KV_HEADS, HEAD_DIM))
        new_v.append(vf.reshape(batch, seq, N_KV_HEADS, HEAD_DIM))
        attn = _attention_online(q, k, vt, seq, seq)
        per_seq = lambda a: a.reshape(batch, seq, a.shape[-1])
        xp = _mix_merge(per_seq(xp), per_seq(h), per_seq(attn), per_seq(ap), per_seq(u), per_seq(vn), mod_ctx, lw, seq)
        xp = _mlp(xp.reshape(1, n_ctx, D_MODEL), mod_ctx, lw, final_g, last)

        q, k, vt, ap, u, vn, h = _pre(xs, mod_lat, lw, rope_tabs, False)
        ck = cache_k[:, l].reshape(dec_batch, past, KV_W).astype(_BF16)
        cvt = jnp.transpose(cache_v[:, l], (0, 2, 3, 1)).astype(_BF16)
        cvt = jnp.concatenate([cvt, jnp.ones((dec_batch, N_KV_HEADS, ONES_ROWS, past), _BF16)], axis=2)
        cvt = cvt.reshape(dec_batch, VT_ROWS, past)
        q_norm = HEAD_DIM ** 0.5 * jnp.max(jnp.abs(q_norm_g[l])) * Q_SCALE
        k_norm = HEAD_DIM ** 0.5 * jnp.max(jnp.abs(k_norm_g[l]))
        ck_sq = jnp.square(ck.astype(_F32)).reshape(dec_batch, past, N_KV_HEADS, HEAD_DIM)
        k_norm = jnp.maximum(k_norm, jnp.sqrt(jnp.max(jnp.sum(ck_sq, axis=-1))))
        attn = _attention_cached(q, ck, cvt, k, vt, BOUND_MARGIN * q_norm * k_norm)
        xs = _mix_merge(xs, h, attn, ap, u, vn, mod_lat, lw, TOK_TILE)
        xs = _mlp(xs, mod_lat, lw, final_g, last)

    y_prompt = xp.reshape(batch, seq, D_MODEL)
    return (y_prompt, xs, jnp.stack(new_k, axis=1), jnp.stack(new_v, axis=1))
```

```python
import functools

import jax
import jax.numpy as jnp
from jax import lax
from jax.experimental import pallas as pl
from jax.experimental.pallas import tpu as pltpu

D_MODEL = 1024
DEPTH = 2
GRID_W = 64
N_HEADS = 8
N_KV_HEADS = 2
HEAD_DIM = 64
Q_W = N_HEADS * HEAD_DIM
KV_W = N_KV_HEADS * HEAD_DIM
GROUP = N_HEADS // N_KV_HEADS
AXIS_DIM = HEAD_DIM // 2
ROPE_THETA = 10000.0
CONV_W = 512
CONV_K = 31
POOL_W = 512
POOL_GROUPS = 4
POOL_GC = POOL_W // POOL_GROUPS
POOL_WINDOWS = (2, 4, 8, 16)
SGU_W = 512
SGU_GROUPS = 4
SGU_GC = SGU_W // SGU_GROUPS
SGU_CHUNK = 128
BRANCH_W = 512
N_BRANCH = 4
D_FF = 4 * D_MODEL
IN_W = Q_W + 2 * KV_W + 2 * CONV_W + POOL_W + 2 * SGU_W
EPS = 1e-6

O_Q = 0
O_KV = Q_W
O_A = Q_W + 2 * KV_W
O_P = O_A + 2 * CONV_W
O_S = O_P + POOL_W

LANES = 128
SUBLANES = 8
CONV_ROWS = 64
HALO = 16
TOK_TILE = 512
PRE_TILE = 1024
MLP_TILE = 1024
PRE_ROWS = 256
Q_TILE = 2048
Q_TILE_ONLINE = 512
KEY_BLOCK = 768
KEY_BLOCK_BOUNDED = 512
ONES_ROWS = 16
VT_HEAD_ROWS = HEAD_DIM + ONES_ROWS
VT_ROWS = N_KV_HEADS * VT_HEAD_ROWS
MOD_ROWS = 16
MOD_TILE = 1536
VMEM_LIMIT = 56 * 1024 * 1024
Q_SCALE = HEAD_DIM ** -0.5 * 1.4426950408889634
BOUND_MARGIN = 1.02
MAX_SCORE_BOUND = 40.0

_F32 = jnp.float32
_BF16 = jnp.bfloat16


def _dot(a, b):
    return jnp.dot(a, b, preferred_element_type=_F32)


def _cparams(sem, flags=None):
    return pltpu.CompilerParams(dimension_semantics=sem, vmem_limit_bytes=VMEM_LIMIT, flags=flags)


def _mod_kernel(c_ref, w_ref, b_ref, o_ref):
    c = c_ref[...]
    cs = c * jax.nn.sigmoid(c)
    o_ref[0] = _dot(cs.astype(_BF16), w_ref[0].astype(_BF16)) + b_ref[0]


def _modulation(c_all, w_mod, b_mod):
    n_col = 6 * D_MODEL
    return pl.pallas_call(
        _mod_kernel,
        grid=(DEPTH, n_col // MOD_TILE),
        in_specs=[
            pl.BlockSpec((MOD_ROWS, D_MODEL), lambda l, j: (0, 0)),
            pl.BlockSpec((1, D_MODEL, MOD_TILE), lambda l, j: (l, 0, j)),
            pl.BlockSpec((1, 1, MOD_TILE), lambda l, j: (l, 0, j)),
        ],
        out_specs=pl.BlockSpec((1, MOD_ROWS, MOD_TILE), lambda l, j: (l, 0, j)),
        out_shape=jax.ShapeDtypeStruct((DEPTH, MOD_ROWS, n_col), _F32),
        compiler_params=_cparams(("parallel", "parallel")),
        name="modulation",
    )(c_all, w_mod, b_mod.reshape(DEPTH, 1, n_col))


def _rope(x, cos, se, so):
    outs = []
    for c in range(x.shape[1] // LANES):
        xc = x[:, c * LANES:(c + 1) * LANES]
        nxt = pltpu.roll(xc, LANES - 1, 1)
        prv = pltpu.roll(xc, 1, 1)
        outs.append(xc * cos + nxt * se + prv * so)
    return outs[0] if len(outs) == 1 else jnp.concatenate(outs, axis=1)


def _pre_kernel(*refs, rope, emit_kv):
    x_ref, mod_ref, ng_ref, win_ref, qg_ref, kg_ref, indq_ref, indk_ref, sg_ref = refs[:9]
    refs = refs[9:]
    if rope:
        cos_ref, se_ref, so_ref = refs[:3]
        refs = refs[3:]
    q_ref, k_ref, vt_ref, ap_ref, u_ref, vn_ref, h_ref = refs[:7]
    if emit_kv:
        kf_ref, vf_ref = refs[7:9]

    for r0 in range(0, x_ref.shape[1], PRE_ROWS):
        rs = slice(r0, r0 + PRE_ROWS)
        tabs = (cos_ref[rs, :], se_ref[rs, :], so_ref[rs, :]) if rope else None
        x = x_ref[0, rs, :]
        xn = x * lax.rsqrt(jnp.mean(x * x, axis=-1, keepdims=True) + EPS) * ng_ref[...]
        h = xn * (1.0 + mod_ref[0, 1:2, :]) + mod_ref[0, 0:1, :]
        hb = h.astype(_BF16)
        h_ref[0, rs, :] = hb

        zs = jax.nn.gelu(_dot(hb, win_ref[:, O_S:O_S + 2 * SGU_W]))
        u_ref[0, rs, :] = zs[:, :SGU_W].astype(_BF16)
        v = zs[:, SGU_W:]
        vn = v * lax.rsqrt(jnp.mean(v * v, axis=-1, keepdims=True) + EPS) * sg_ref[...]
        vn_ref[0, rs, :] = vn.astype(_BF16)

        za = _dot(hb, win_ref[:, O_A:O_A + 2 * CONV_W])
        ap_ref[0, rs, 0:CONV_W] = za[:, :CONV_W] * jax.nn.sigmoid(za[:, CONV_W:])

        zq = _dot(hb, win_ref[:, O_Q:O_Q + Q_W])
        msq = _dot((zq * zq).astype(_BF16), indq_ref[...])
        qn = zq * lax.rsqrt(msq + EPS) * qg_ref[...]
        if rope:
            qn = _rope(qn, *tabs)
        q_ref[0, rs, :] = (qn * Q_SCALE).astype(_BF16)

        zkv = _dot(hb, win_ref[:, O_KV:O_KV + 2 * KV_W])
        zk = zkv[:, :KV_W]
        zv = zkv[:, KV_W:]
        msk = _dot((zk * zk).astype(_BF16), indk_ref[...])
        kn = zk * lax.rsqrt(msk + EPS) * kg_ref[...]
        if emit_kv:
            kf_ref[0, rs, :] = kn
            vf_ref[0, rs, :] = zv
        if rope:
            kn = _rope(kn, *tabs)
        k_ref[0, rs, :] = kn.astype(_BF16)
        vt = zv.T.astype(_BF16)
        for j in range(N_KV_HEADS):
            vt_ref[0, j * VT_HEAD_ROWS:j * VT_HEAD_ROWS + HEAD_DIM, rs] = vt[j * HEAD_DIM:(j + 1) * HEAD_DIM, :]
            vt_ref[0, j * VT_HEAD_ROWS + HEAD_DIM:(j + 1) * VT_HEAD_ROWS, rs] = jnp.ones((ONES_ROWS, PRE_ROWS), _BF16)

        ap_ref[0, rs, CONV_W:CONV_W + POOL_W] = _dot(hb, win_ref[:, O_P:O_P + POOL_W])


def _pre(x, mod, lw, rope_tabs, emit_kv):
    bsz, n, _ = x.shape
    t = PRE_TILE
    rope = rope_tabs is not None
    row = lambda b, i: (b, i, 0)
    const2 = lambda b, i: (0, 0)
    in_specs = [
        pl.BlockSpec((1, t, D_MODEL), row),
        pl.BlockSpec((1, 6, D_MODEL), lambda b, i: (b, 0, 0)),
        pl.BlockSpec((1, D_MODEL), const2),
        pl.BlockSpec((D_MODEL, IN_W), const2, pipeline_mode=pl.Buffered(1)),
        pl.BlockSpec((1, Q_W), const2),
        pl.BlockSpec((1, KV_W), const2),
        pl.BlockSpec((Q_W, Q_W), const2),
        pl.BlockSpec((KV_W, KV_W), const2),
        pl.BlockSpec((1, SGU_W), const2),
    ]
    args = [x, mod, lw['norm1_g'], lw['w_in'], lw['q_g'], lw['k_g'], lw['ind_q'], lw['ind_k'], lw['sgu_norm_g']]
    if rope:
        in_specs += [pl.BlockSpec((t, LANES), lambda b, i: (i, 0))] * 3
        args += list(rope_tabs)
    out_specs = [
        pl.BlockSpec((1, t, Q_W), row),
        pl.BlockSpec((1, t, KV_W), row),
        pl.BlockSpec((1, VT_ROWS, t), lambda b, i: (b, 0, i)),
        pl.BlockSpec((1, t, CONV_W + POOL_W), row),
        pl.BlockSpec((1, t, SGU_W), row),
        pl.BlockSpec((1, t, SGU_W), row),
        pl.BlockSpec((1, t, D_MODEL), row),
    ]
    out_shape = [
        jax.ShapeDtypeStruct((bsz, n, Q_W), _BF16),
        jax.ShapeDtypeStruct((bsz, n, KV_W), _BF16),
        jax.ShapeDtypeStruct((bsz, VT_ROWS, n), _BF16),
        jax.ShapeDtypeStruct((bsz, n, CONV_W + POOL_W), _F32),
        jax.ShapeDtypeStruct((bsz, n, SGU_W), _BF16),
        jax.ShapeDtypeStruct((bsz, n, SGU_W), _BF16),
        jax.ShapeDtypeStruct((bsz, n, D_MODEL), _BF16),
    ]
    if emit_kv:
        out_specs += [pl.BlockSpec((1, t, KV_W), row)] * 2
        out_shape += [jax.ShapeDtypeStruct((bsz, n, KV_W), _F32)] * 2
    return pl.pallas_call(
        functools.partial(_pre_kernel, rope=rope, emit_kv=emit_kv),
        grid=(bsz, n // t),
        in_specs=in_specs,
        out_specs=out_specs,
        out_shape=out_shape,
        compiler_params=_cparams(("parallel", "parallel")),
        name="pre_lat" if rope else "pre_ctx",
    )(*args)


def _split_heads(q_ref, qh_ref):
    tq = q_ref.shape[1]
    hi_half = lax.broadcasted_iota(jnp.int32, (1, LANES), 1) >= HEAD_DIM
    for g in range(GROUP):
        qc = q_ref[0, :, g * LANES:(g + 1) * LANES]
        for j in range(N_KV_HEADS):
            qh_ref[j, g * tq:(g + 1) * tq, :] = jnp.where(hi_half if j == 1 else jnp.logical_not(hi_half), qc,
                                                          jnp.zeros_like(qc))


def _write_heads(acc_ref, o_ref):
    tq = o_ref.shape[1]
    for g in range(GROUP):
        halves = []
        for j in range(N_KV_HEADS):
            acc = acc_ref[j, :, g * tq:(g + 1) * tq]
            halves.append(acc[0:HEAD_DIM, :] / acc[HEAD_DIM:HEAD_DIM + 1, :])
        o_ref[0, :, g * LANES:(g + 1) * LANES] = jnp.concatenate(halves, axis=0).T.astype(_BF16)


_NT = (((1,), (1,)), ((), ()))


def _attn_kernel(q_ref, k_ref, vt_ref, o_ref, qh_ref, m_ref, alpha_ref, acc_ref, s_ref, *, n_keys, key_block):
    _split_heads(q_ref, qh_ref)
    m_ref[...] = jnp.full(m_ref.shape, -1e30, _F32)
    acc_ref[...] = jnp.zeros(acc_ref.shape, _F32)

    def scores(off, j):
        s = lax.dot_general(k_ref[0, pl.ds(off, key_block), :], qh_ref[j], _NT, preferred_element_type=_F32)
        m_old = m_ref[j]
        m_new = jnp.maximum(m_old, jnp.max(s, axis=0, keepdims=True))
        m_ref[j] = m_new
        alpha_ref[j] = jnp.exp2(m_old - m_new)
        s_ref[j] = s

    def values(off, j):
        p = jnp.exp2(s_ref[j] - m_ref[j]).astype(_BF16)
        vblk = vt_ref[0, j * VT_HEAD_ROWS:(j + 1) * VT_HEAD_ROWS, pl.ds(off, key_block)]
        acc_ref[j] = alpha_ref[j] * acc_ref[j] + _dot(vblk, p)

    n_blocks = n_keys // key_block
    unroll = 2 if n_blocks % 2 == 0 else 1
    scores(0, 0)

    def body(b, carry):
        for r in range(unroll):
            off = pl.multiple_of((b * unroll + r) * key_block, key_block)
            scores(off, 1)
            values(off, 0)
            scores(off + key_block, 0)
            values(off, 1)
        return carry

    lax.fori_loop(0, n_blocks // unroll - 1, body, 0)
    for r in range(unroll):
        off = (n_blocks - unroll + r) * key_block
        scores(off, 1)
        values(off, 0)
        if r < unroll - 1:
            scores(off + key_block, 0)
        values(off, 1)
    _write_heads(acc_ref, o_ref)


def _attn_bounded_kernel(bound_ref, q_ref, ck_ref, cvt_ref, k_ref, vt_ref, o_ref, qh_ref, acc_ref, *, n_keys, key_block):
    _split_heads(q_ref, qh_ref)
    shift = bound_ref[0, 0]

    def weighted_values(kblk, vtblk, j):
        s = lax.dot_general(kblk, qh_ref[j], _NT, preferred_element_type=_F32)
        return _dot(vtblk, jnp.exp2(s - shift).astype(_BF16))

    for j in range(N_KV_HEADS):
        acc_ref[j] = weighted_values(ck_ref[0], cvt_ref[0, j * VT_HEAD_ROWS:(j + 1) * VT_HEAD_ROWS, :], j)

    n_blocks = n_keys // key_block
    unroll = 2 if n_blocks % 2 == 0 else 1

    def body(b, carry):
        for r in range(unroll):
            off = pl.multiple_of((b * unroll + r) * key_block, key_block)
            for j in range(N_KV_HEADS):
                vtblk = vt_ref[0, j * VT_HEAD_ROWS:(j + 1) * VT_HEAD_ROWS, pl.ds(off, key_block)]
                acc_ref[j] = acc_ref[j] + weighted_values(k_ref[0, pl.ds(off, key_block), :], vtblk, j)
        return carry

    lax.fori_loop(0, n_blocks // unroll, body, 0)
    _write_heads(acc_ref, o_ref)


def _attention_specs(q, tq, q_run, n_keys):
    per_run = q_run // tq
    qmap = lambda b, i: (b, i, 0)
    in_specs = [
        pl.BlockSpec((1, tq, Q_W), qmap),
        pl.BlockSpec((1, n_keys, KV_W), lambda b, i: (b, i // per_run, 0)),
        pl.BlockSpec((1, VT_ROWS, n_keys), lambda b, i: (b, 0, i // per_run)),
    ]
    out_spec = pl.BlockSpec((1, tq, Q_W), qmap)
    out_shape = jax.ShapeDtypeStruct(q.shape, _BF16)
    qh_scratch = pltpu.VMEM((N_KV_HEADS, GROUP * tq, LANES), _BF16)
    acc_scratch = pltpu.VMEM((N_KV_HEADS, VT_HEAD_ROWS, GROUP * tq), _F32)
    return in_specs, out_spec, out_shape, qh_scratch, acc_scratch


def _attention_online(q, k, vt, q_run, n_keys):
    bsz, n, _ = q.shape
    tq = min(Q_TILE_ONLINE, q_run)
    key_block = min(KEY_BLOCK, n_keys)
    in_specs, out_spec, out_shape, qh_scratch, acc_scratch = _attention_specs(q, tq, q_run, n_keys)
    row_scratch = pltpu.VMEM((N_KV_HEADS, 1, GROUP * tq), _F32)
    return pl.pallas_call(
        functools.partial(_attn_kernel, n_keys=n_keys, key_block=key_block),
        grid=(bsz, n // tq),
        in_specs=in_specs,
        out_specs=out_spec,
        out_shape=out_shape,
        scratch_shapes=[qh_scratch, row_scratch, row_scratch, acc_scratch,
                        pltpu.VMEM((N_KV_HEADS, key_block, GROUP * tq), _F32)],
        compiler_params=_cparams(("parallel", "arbitrary")),
        name="attn_online",
    )(q, k, vt)


def _attention_cached(q, ck, cvt, k, vt, score_bound):
    bsz, n, _ = q.shape
    past = ck.shape[1]
    tq = min(Q_TILE, n)
    in_specs, out_spec, out_shape, qh_scratch, acc_scratch = _attention_specs(q, tq, n, n)
    cache_specs = [
        pl.BlockSpec((1, past, KV_W), lambda b, i: (b, 0, 0)),
        pl.BlockSpec((1, VT_ROWS, past), lambda b, i: (b, 0, 0)),
    ]

    def bounded(q, ck, cvt, k, vt):
        return pl.pallas_call(
            functools.partial(_attn_bounded_kernel, n_keys=n, key_block=min(KEY_BLOCK_BOUNDED, n)),
            grid=(bsz, n // tq),
            in_specs=[pl.BlockSpec(memory_space=pltpu.SMEM), in_specs[0]] + cache_specs + in_specs[1:],
            out_specs=out_spec,
            out_shape=out_shape,
            scratch_shapes=[qh_scratch, acc_scratch],
            compiler_params=_cparams(("parallel", "arbitrary")),
            name="attn_bounded",
        )(score_bound.reshape(1, 1), q, ck, cvt, k, vt)

    def online(q, ck, cvt, k, vt):
        return _attention_online(q, jnp.concatenate([ck, k], axis=1), jnp.concatenate([cvt, vt], axis=2),
                                 n, past + n)

    return lax.cond(score_bound < MAX_SCORE_BOUND, bounded, online, q, ck, cvt, k, vt)


def _mix_merge_kernel(ap_ref, prev_ref, next_ref, u_ref, vn_ref, x_ref, h_ref, attn_ref, mod_ref, cw_ref, cb_ref,
                      lg_ref, lb_ref, pw_ref, ps_ref, sw_ref, sb_ref, wg_ref, bg_ref, wb_ref, wo_ref, o_ref,
                      buf_ref, xs_ref, cacc_ref, tmp_ref, br_ref, gate_ref, *, seq_len, gates_with_conv):
    t = ap_ref.shape[1]
    i = pl.program_id(1)
    last = pl.num_programs(1) - 1
    buf_ref[0:HALO, :] = jnp.where(i > 0, prev_ref[0], 0.0)
    buf_ref[HALO:HALO + t, :] = ap_ref[0]
    buf_ref[HALO + t:HALO + t + HALO, :] = jnp.where(i < last, next_ref[0], 0.0)
    buf_ref[t + 2 * HALO:t + 2 * HALO + SUBLANES, :] = jnp.zeros((SUBLANES, buf_ref.shape[1]), _F32)

    n_sh = t + 2 * HALO - SUBLANES
    for r in range(1, SUBLANES):
        xs_ref[r - 1] = buf_ref[r:r + n_sh, 0:CONV_W]

    n_iter = t // CONV_ROWS
    gate_cols = N_BRANCH * D_MODEL // n_iter
    for ci in range(n_iter):
        base = ci * CONV_ROWS
        acc = jnp.broadcast_to(cb_ref[...], (CONV_ROWS, CONV_W))
        for k in range(CONV_K):
            q8, r = divmod(HALO - CONV_K // 2 + k, SUBLANES)
            rows = pl.ds(base + q8 * SUBLANES, CONV_ROWS)
            x = buf_ref[rows, :CONV_W] if r == 0 else xs_ref[r - 1, rows, :]
            acc = acc + x * cw_ref[k:k + 1, :]
        cacc_ref[pl.ds(base, CONV_ROWS), :] = acc
        if gates_with_conv:
            g0 = ci * gate_cols
            gate = jax.nn.sigmoid(_dot(h_ref[0], wg_ref[:, g0:g0 + gate_cols]) + bg_ref[:, g0:g0 + gate_cols])
            gate_ref[:, g0:g0 + gate_cols] = gate.astype(_BF16)
    acc = cacc_ref[...]
    mu = jnp.mean(acc, axis=-1, keepdims=True)
    cen = acc - mu
    var = jnp.mean(cen * cen, axis=-1, keepdims=True)
    y = cen * lax.rsqrt(var + EPS) * lg_ref[...] + lb_ref[...]
    br_ref[0] = (y * jax.nn.sigmoid(y)).astype(_BF16)

    pos = i * t + lax.broadcasted_iota(jnp.int32, (t, 1), 0)
    for g, w in enumerate(POOL_WINDOWS):
        cols = slice(CONV_W + g * POOL_GC, CONV_W + (g + 1) * POOL_GC)
        first = HALO - w // 2
        if w == 2:
            s = buf_ref[first:first + t, cols] + buf_ref[first + 1:first + 1 + t, cols]
        else:
            n = t + 2 * HALO
            tmp_ref[0, 0:n, :] = buf_ref[0:n, cols] + buf_ref[1:n + 1, cols]
            span, stage = 2, 0
            while span * 2 < w:
                n -= SUBLANES
                tmp_ref[stage + 1, 0:n, :] = tmp_ref[stage, 0:n, :] + tmp_ref[stage, span:span + n, :]
                span, stage = span * 2, stage + 1
            s = tmp_ref[stage, first:first + t, :] + tmp_ref[stage, first + span:first + span + t, :]
        lo = jnp.maximum(pos - w // 2, 0)
        hi = jnp.minimum(pos - w // 2 + w, seq_len)
        pooled = s / (hi - lo).astype(_F32) - buf_ref[HALO:HALO + t, cols]
        yg = _dot(pooled.astype(_BF16), pw_ref[g])
        br_ref[1, :, g * POOL_GC:(g + 1) * POOL_GC] = (yg * ps_ref[:, g * POOL_GC:(g + 1) * POOL_GC]).astype(_BF16)

    n_chunk = t // SGU_CHUNK
    for g in range(SGU_GROUPS):
        cols = slice(g * SGU_GC, (g + 1) * SGU_GC)
        rhs = jnp.concatenate([vn_ref[0, c * SGU_CHUNK:(c + 1) * SGU_CHUNK, cols] for c in range(n_chunk)], axis=1)
        sv = _dot(sw_ref[g], rhs)
        for c in range(n_chunk):
            rows = slice(c * SGU_CHUNK, (c + 1) * SGU_CHUNK)
            gate = sv[:, c * SGU_GC:(c + 1) * SGU_GC] + sb_ref[:, cols]
            br_ref[2, rows, cols] = (u_ref[0, rows, cols].astype(_F32) * gate).astype(_BF16)

    merged = None
    for idx in range(N_BRANCH):
        branch = attn_ref[0] if idx == 0 else br_ref[idx - 1]
        cols = slice(idx * D_MODEL, (idx + 1) * D_MODEL)
        if gates_with_conv:
            gate = gate_ref[:, cols].astype(_F32)
        else:
            gate = jax.nn.sigmoid(_dot(h_ref[0], wg_ref[:, cols]) + bg_ref[:, cols])
        term = gate * _dot(branch, wb_ref[idx])
        merged = term if merged is None else merged + term
    m = _dot(merged.astype(_BF16), wo_ref[...])
    o_ref[0] = x_ref[0] + mod_ref[0, 2:3, :] * m


def _mix_merge(x, h, attn, ap, u, vn, mod, lw, t):
    bsz, n, _ = ap.shape
    hb = t // HALO
    n_hb = n // HALO
    row = lambda b, i: (b, i, 0)
    const2 = lambda b, i: (0, 0)
    const3 = lambda b, i: (0, 0, 0)
    w_all = CONV_W + POOL_W
    mod_map = (lambda b, i: (b, 0, 0)) if mod.shape[0] == bsz else (lambda b, i: (0, 0, 0))
    in_specs = [
        pl.BlockSpec((1, t, w_all), row),
        pl.BlockSpec((1, HALO, w_all), lambda b, i: (b, jnp.maximum(i * hb - 1, 0), 0)),
        pl.BlockSpec((1, HALO, w_all), lambda b, i: (b, jnp.minimum((i + 1) * hb, n_hb - 1), 0)),
        pl.BlockSpec((1, t, SGU_W), row),
        pl.BlockSpec((1, t, SGU_W), row),
        pl.BlockSpec((1, t, D_MODEL), row),
        pl.BlockSpec((1, t, D_MODEL), row),
        pl.BlockSpec((1, t, BRANCH_W), row),
        pl.BlockSpec((1, 6, D_MODEL), mod_map),
        pl.BlockSpec((CONV_K, CONV_W), const2),
        pl.BlockSpec((1, CONV_W), const2),
        pl.BlockSpec((1, CONV_W), const2),
        pl.BlockSpec((1, CONV_W), const2),
        pl.BlockSpec((POOL_GROUPS, POOL_GC, POOL_GC), const3),
        pl.BlockSpec((1, POOL_W), const2),
        pl.BlockSpec((SGU_GROUPS, SGU_CHUNK, SGU_CHUNK), const3),
        pl.BlockSpec((SGU_CHUNK, SGU_W), const2),
        pl.BlockSpec((D_MODEL, N_BRANCH * D_MODEL), const2, pipeline_mode=pl.Buffered(1)),
        pl.BlockSpec((1, N_BRANCH * D_MODEL), const2),
        pl.BlockSpec((N_BRANCH, BRANCH_W, D_MODEL), const3, pipeline_mode=pl.Buffered(1)),
        pl.BlockSpec((D_MODEL, D_MODEL), const2, pipeline_mode=pl.Buffered(1)),
    ]
    return pl.pallas_call(
        functools.partial(_mix_merge_kernel, seq_len=n, gates_with_conv=t < TOK_TILE),
        grid=(bsz, n // t),
        in_specs=in_specs,
        out_specs=pl.BlockSpec((1, t, D_MODEL), row),
        out_shape=jax.ShapeDtypeStruct((bsz, n, D_MODEL), _F32),
        scratch_shapes=[
            pltpu.VMEM((t + 2 * HALO + SUBLANES, w_all), _F32),
            pltpu.VMEM((SUBLANES - 1, t + 2 * HALO - SUBLANES, CONV_W), _F32),
            pltpu.VMEM((t, CONV_W), _F32),
            pltpu.VMEM((3, t + 2 * HALO, POOL_GC), _F32),
            pltpu.VMEM((N_BRANCH - 1, t, BRANCH_W), _BF16),
            pltpu.VMEM((t, N_BRANCH * D_MODEL), _BF16),
        ],
        compiler_params=_cparams(("parallel", "parallel")),
        name="mix_merge",
    )(ap, ap, ap, u, vn, x, h, attn, mod, lw['conv_w'], lw['conv_b'], lw['conv_ln_g'], lw['conv_ln_b'], lw['pool_w'],
      lw['pool_scale'], lw['sgu_w'], lw['sgu_bias'], lw['w_gate'], lw['b_gate'], lw['w_branch'], lw['w_out'])


def _mlp_kernel(x_ref, mod_ref, ng_ref, w1_ref, w2_ref, fg_ref, o_ref, hid_ref, *, final_norm):
    x = x_ref[0]
    xn = x * lax.rsqrt(jnp.mean(x * x, axis=-1, keepdims=True) + EPS) * ng_ref[...]
    hb = (xn * (1.0 + mod_ref[0, 4:5, :]) + mod_ref[0, 3:4, :]).astype(_BF16)
    for c in range(D_FF // D_MODEL):
        cols = slice(c * D_MODEL, (c + 1) * D_MODEL)
        a = jnp.maximum(_dot(hb, w1_ref[:, cols]), 0.0)
        hid_ref[:, cols] = (a * a).astype(_BF16)
    y = x + mod_ref[0, 5:6, :] * _dot(hid_ref[...], w2_ref[...])
    if final_norm:
        y = y * lax.rsqrt(jnp.mean(y * y, axis=-1, keepdims=True) + EPS) * fg_ref[...]
    o_ref[0] = y


def _mlp(x, mod, lw, final_g, final_norm):
    bsz, n, _ = x.shape
    t = MLP_TILE
    row = lambda b, i: (b, i, 0)
    const2 = lambda b, i: (0, 0)
    return pl.pallas_call(
        functools.partial(_mlp_kernel, final_norm=final_norm),
        grid=(bsz, n // t),
        in_specs=[
            pl.BlockSpec((1, t, D_MODEL), row),
            pl.BlockSpec((1, 6, D_MODEL), lambda b, i: (b, 0, 0)),
            pl.BlockSpec((1, D_MODEL), const2),
            pl.BlockSpec((D_MODEL, D_FF), const2, pipeline_mode=pl.Buffered(1)),
            pl.BlockSpec((D_FF, D_MODEL), const2, pipeline_mode=pl.Buffered(1)),
            pl.BlockSpec((1, D_MODEL), const2),
        ],
        out_specs=pl.BlockSpec((1, t, D_MODEL), row),
        out_shape=jax.ShapeDtypeStruct((bsz, n, D_MODEL), _F32),
        scratch_shapes=[pltpu.VMEM((t, D_FF), _BF16)],
        compiler_params=_cparams(("parallel", "parallel")),
        name="mlp",
    )(x, mod, lw['norm2_g'], lw['w_mlp_in'], lw['w_mlp_out'], final_g)


def _rope_tables(n):
    rows = n // GRID_W
    row = jnp.repeat(jnp.arange(rows, dtype=_F32), GRID_W)
    col = jnp.tile(jnp.arange(GRID_W, dtype=_F32), rows)
    inv = ROPE_THETA ** (-jnp.arange(0, AXIS_DIM, 2, dtype=_F32) / AXIS_DIM)
    ang = jnp.concatenate([row[:, None] * inv, col[:, None] * inv], axis=-1)
    cos = jnp.repeat(jnp.cos(ang), 2, axis=-1)
    sin = jnp.repeat(jnp.sin(ang), 2, axis=-1)
    even = (jnp.arange(HEAD_DIM) % 2 == 0)[None, :]
    se = jnp.where(even, -sin, 0.0)
    so = jnp.where(even, 0.0, sin)
    rep = LANES // HEAD_DIM
    return tuple(jnp.tile(tab, (1, rep)) for tab in (cos, se, so))


def _head_indicator(width):
    head = jnp.arange(width) // HEAD_DIM
    return ((head[:, None] == head[None, :]).astype(_F32) / HEAD_DIM).astype(_BF16)


def _layer_weights(l, w_in, norm1_g, q_norm_g, k_norm_g, conv_w, conv_b, conv_ln_g, conv_ln_b, pool_w,
                   pool_scale, sgu_norm_g, sgu_w, sgu_b, w_branch, w_gate, b_gate, w_out, norm2_g,
                   w_mlp_in, w_mlp_out):
    win = w_in[l].astype(_BF16)
    wq = win[:, :Q_W].reshape(D_MODEL, N_KV_HEADS, GROUP, HEAD_DIM).transpose(0, 2, 1, 3).reshape(D_MODEL, Q_W)
    win = jnp.concatenate([wq, win[:, Q_W:]], axis=1)
    wb = w_branch[l].astype(_BF16)
    wb0 = wb[0].reshape(N_KV_HEADS, GROUP, HEAD_DIM, D_MODEL).transpose(1, 0, 2, 3).reshape(Q_W, D_MODEL)
    wb = jnp.concatenate([wb0[None], wb[1:]], axis=0)
    row = lambda v: v.reshape(1, -1)
    return {
        'norm1_g': row(norm1_g[l]),
        'w_in': win,
        'q_g': row(jnp.tile(q_norm_g[l], N_HEADS)),
        'k_g': row(jnp.tile(k_norm_g[l], N_KV_HEADS)),
        'ind_q': _head_indicator(Q_W),
        'ind_k': _head_indicator(KV_W),
        'sgu_norm_g': row(sgu_norm_g[l]),
        'conv_w': conv_w[l],
        'conv_b': row(conv_b[l]),
        'conv_ln_g': row(conv_ln_g[l]),
        'conv_ln_b': row(conv_ln_b[l]),
        'pool_w': pool_w[l].astype(_BF16),
        'pool_scale': row(pool_scale[l]),
        'sgu_w': sgu_w[l].astype(_BF16),
        'sgu_bias': jnp.repeat(sgu_b[l].T, SGU_GC, axis=1),
        'w_gate': w_gate[l].astype(_BF16),
        'b_gate': row(b_gate[l]),
        'w_branch': wb,
        'w_out': w_out[l].astype(_BF16),
        'norm2_g': row(norm2_g[l]),
        'w_mlp_in': w_mlp_in[l].astype(_BF16),
        'w_mlp_out': w_mlp_out[l].astype(_BF16),
    }


def kernel(x_prompt, x_sample, cache_k, cache_v, c, c_ctx, w_mod, b_mod, norm1_g, w_in, q_norm_g, k_norm_g, conv_w, conv_b, conv_ln_g, conv_ln_b, pool_w, pool_scale, sgu_norm_g, sgu_w, sgu_b, w_branch, w_gate, b_gate, w_out, norm2_g, w_mlp_in, w_mlp_out, final_norm_g):
    batch, seq, _ = x_prompt.shape
    dec_batch, dec_seq, _ = x_sample.shape
    past = cache_k.shape[2]
    n_ctx = batch * seq

    c_all = jnp.concatenate([c, c_ctx[None, :], jnp.zeros((MOD_ROWS - dec_batch - 1, D_MODEL), _F32)], axis=0)
    mod = _modulation(c_all, w_mod, b_mod).reshape(DEPTH, MOD_ROWS, 6, D_MODEL)

    rope_tabs = _rope_tables(dec_seq)
    final_g = final_norm_g.reshape(1, D_MODEL)

    xp = x_prompt.reshape(1, n_ctx, D_MODEL)
    xs = x_sample
    new_k, new_v = [], []
    for l in range(DEPTH):
        lw = _layer_weights(l, w_in, norm1_g, q_norm_g, k_norm_g, conv_w, conv_b, conv_ln_g, conv_ln_b, pool_w,
                            pool_scale, sgu_norm_g, sgu_w, sgu_b, w_branch, w_gate, b_gate, w_out, norm2_g,
                            w_mlp_in, w_mlp_out)
        mod_lat = mod[l, :dec_batch]
        mod_ctx = mod[l, dec_batch:dec_batch + 1]
        last = l == DEPTH - 1

        q, k, vt, ap, u, vn, h, kf, vf = _pre(xp, mod_ctx, lw, None, True)
        new_k.append(kf.reshape(batch, seq, N_KV_HEADS, HEAD_DIM))
        new_v.append(vf.reshape(batch, seq, N_KV_HEADS, HEAD_DIM))
        attn = _attention_online(q, k, vt, seq, seq)
        per_seq = lambda a: a.reshape(batch, seq, a.shape[-1])
        xp = _mix_merge(per_seq(xp), per_seq(h), per_seq(attn), per_seq(ap), per_seq(u), per_seq(vn), mod_ctx, lw, seq)
        xp = _mlp(xp.reshape(1, n_ctx, D_MODEL), mod_ctx, lw, final_g, last)

        q, k, vt, ap, u, vn, h = _pre(xs, mod_lat, lw, rope_tabs, False)
        ck = cache_k[:, l].reshape(dec_batch, past, KV_W).astype(_BF16)
        cvt = jnp.transpose(cache_v[:, l], (0, 2, 3, 1)).astype(_BF16)
        cvt = jnp.concatenate([cvt, jnp.ones((dec_batch, N_KV_HEADS, ONES_ROWS, past), _BF16)], axis=2)
        cvt = cvt.reshape(dec_batch, VT_ROWS, past)
        q_norm = HEAD_DIM ** 0.5 * jnp.max(jnp.abs(q_norm_g[l])) * Q_SCALE
        k_norm = HEAD_DIM ** 0.5 * jnp.max(jnp.abs(k_norm_g[l]))
        ck_sq = jnp.square(ck.astype(_F32)).reshape(dec_batch, past, N_KV_HEADS, HEAD_DIM)
        k_norm = jnp.maximum(k_norm, jnp.sqrt(jnp.max(jnp.sum(ck_sq, axis=-1))))
        attn = _attention_cached(q, ck, cvt, k, vt, BOUND_MARGIN * q_norm * k_norm)
        xs = _mix_merge(xs, h, attn, ap, u, vn, mod_lat, lw, TOK_TILE)
        xs = _mlp(xs, mod_lat, lw, final_g, last)

    y_prompt = xp.reshape(batch, seq, D_MODEL)
    return (y_prompt, xs, jnp.stack(new_k, axis=1), jnp.stack(new_v, axis=1))
```
